```python
import math
import jax, jax.numpy as jnp
from jax import lax
import numpy as np

D_MODEL = 2048
BATCH = 4
SEQ = 4096
DEPTH = 2

CTX_LEN = 256
GRID_W = 64
N_HEADS = D_MODEL // 256
HEAD_DIM = 64
V_HEAD_DIM = 2 * HEAD_DIM
ATTN_W = N_HEADS * 2 * HEAD_DIM
ATTN_OUT_W = N_HEADS * V_HEAD_DIM
CONV_W = D_MODEL // 2
CONV_K = 3
FOURIER_GROUPS = 4
FOURIER_W = D_MODEL // 2
BRANCH_W = D_MODEL // 2
N_BRANCH = 3
D_FF = 4 * D_MODEL
ROPE_THETA = 10000.0
Q_BLOCK = 128
EPS = 1e-6

Q0 = 0
K0 = Q0 + ATTN_W
V0 = K0 + ATTN_W
CB0 = V0 + ATTN_OUT_W
CC0 = CB0 + CONV_W
CX0 = CC0 + CONV_W
F0 = CX0 + CONV_W
G0 = F0 + FOURIER_W
IN_W = G0 + N_BRANCH * D_MODEL

kernel_name = "hybrid_diffattn_shortconv_fourier_dit"


def rms_norm(x, g):
    x32 = x.astype(jnp.float32)
    y = x32 * lax.rsqrt(jnp.mean(jnp.square(x32), axis=-1, keepdims=True) + EPS)
    return (y * g).astype(x.dtype)


def modulate(h, shift, scale):
    return h * (1 + scale) + shift


def axial_rope_tables(n_tokens):
    rows = n_tokens // GRID_W
    t_row = jnp.repeat(jnp.arange(rows), GRID_W).astype(jnp.float32)
    t_col = jnp.tile(jnp.arange(GRID_W), rows).astype(jnp.float32)
    axis_dim = HEAD_DIM // 2
    inv_freq = 1.0 / (ROPE_THETA ** (jnp.arange(0, axis_dim, 2, dtype=jnp.float32) / axis_dim))
    ang_r = t_row[:, None] * inv_freq[None, :]
    ang_c = t_col[:, None] * inv_freq[None, :]
    shp = lambda a: a[None, :, None, None, :]
    return (shp(jnp.cos(ang_r)), shp(jnp.sin(ang_r)), shp(jnp.cos(ang_c)), shp(jnp.sin(ang_c)))


def rope_rotate(x, cos, sin):
    x1, x2 = jnp.split(x, 2, axis=-1)
    cos = cos.astype(x.dtype)
    sin = sin.astype(x.dtype)
    return jnp.concatenate([x1 * cos - x2 * sin, x2 * cos + x1 * sin], axis=-1)


def apply_axial_rope(x, tables):
    cos_r, sin_r, cos_c, sin_c = tables
    half = HEAD_DIM // 2
    return jnp.concatenate([rope_rotate(x[..., :half], cos_r, sin_r),
                            rope_rotate(x[..., half:], cos_c, sin_c)], axis=-1)


def q_from(pq, qn_g):
    lead = pq.shape[:2]
    return rms_norm(pq.reshape(lead + (N_HEADS, 2, HEAD_DIM)), qn_g)


def kv_from(pkv, kn_g):
    lead = pkv.shape[:2]
    k = rms_norm(pkv[..., :ATTN_W].reshape(lead + (N_HEADS, 2, HEAD_DIM)), kn_g)
    v = pkv[..., ATTN_W:].reshape(lead + (N_HEADS, V_HEAD_DIM))
    return k, v


def diff_attention(q, k, v, lam):
    s = jnp.einsum('bqhcd,bkhcd->bhcqk', q, k).astype(jnp.float32) * (HEAD_DIM ** -0.5)
    p = jax.nn.softmax(s, axis=-1)
    a = p[:, :, 0] - lam * p[:, :, 1]
    return jnp.einsum('bhqk,bkhe->bqhe', a.astype(v.dtype), v)


def blocked_diff_attention(q, k, v, lam):
    b, s = q.shape[:2]
    nb = s // Q_BLOCK
    qb = q.reshape((b, nb, Q_BLOCK) + q.shape[2:]).swapaxes(0, 1)
    ob = lax.map(lambda qi: diff_attention(qi, k, v, lam), qb)
    return ob.swapaxes(0, 1).reshape(b, s, N_HEADS, V_HEAD_DIM)


def head_out(o, subln_g, lam_init):
    o = rms_norm(o, subln_g) * (1.0 - lam_init)
    return o.reshape(o.shape[:2] + (ATTN_OUT_W,))


def short_conv(b_gate, c_gate, xin, w):
    u = c_gate * xin
    n = u.shape[1]
    pad = CONV_K // 2
    up = jnp.pad(u, ((0, 0), (pad, pad), (0, 0)))
    y = up[:, 0:n] * w[0]
    for j in range(1, CONV_K):
        y = y + up[:, j:j + n] * w[j]
    return b_gate * y


def fourier_mixer(u):
    b, n, _ = u.shape
    ug = u.astype(jnp.float32).reshape(b, n, FOURIER_GROUPS, FOURIER_W // FOURIER_GROUPS)
    f = jnp.fft.fft2(ug, axes=(1, 3), norm="ortho").real
    return f.reshape(b, n, FOURIER_W).astype(u.dtype)


def mixer_merge(p, attn_o, w_conv, w_br, w_o):
    conv_o = short_conv(p[..., CB0:CC0], p[..., CC0:CX0], p[..., CX0:F0], w_conv)
    four_o = fourier_mixer(p[..., F0:G0])
    gates = jax.nn.sigmoid(p[..., G0:].reshape(p.shape[:-1] + (N_BRANCH, D_MODEL)))
    m = (gates[..., 0, :] * (attn_o @ w_br[0])
         + gates[..., 1, :] * (conv_o @ w_br[1])
         + gates[..., 2, :] * (four_o @ w_br[2]))
    return m @ w_o


def sq_relu_mlp(h, w1, w2):
    return jnp.square(jax.nn.relu(h @ w1)) @ w2


def setup_inputs(seed: int = 0) -> dict:
    key = jax.random.key(seed)
    ks = jax.random.split(key, 20)
    nrm = lambda k, shape, scale: jax.random.normal(k, shape, jnp.float32) * scale
    return {
        "x": nrm(ks[0], (BATCH, SEQ, D_MODEL), 1.0),
        "c": nrm(ks[1], (BATCH, D_MODEL), 1.0),
        "ctx": nrm(ks[2], (BATCH, CTX_LEN, D_MODEL), 1.0),
        "c_ctx": nrm(ks[3], (D_MODEL,), 1.0),
        "ln1_g": 1.0 + nrm(ks[4], (DEPTH, D_MODEL), 0.02),
        "ln2_g": 1.0 + nrm(ks[5], (DEPTH, D_MODEL), 0.02),
        "w_ada": nrm(ks[6], (DEPTH, D_MODEL, 6 * D_MODEL), 0.5 * D_MODEL ** -0.5),
        "b_ada": nrm(ks[7], (DEPTH, 6 * D_MODEL), 0.02),
        "w_in": nrm(ks[8], (DEPTH, D_MODEL, IN_W), D_MODEL ** -0.5),
        "qn_g": 1.0 + nrm(ks[9], (DEPTH, HEAD_DIM), 0.02),
        "kn_g": 1.0 + nrm(ks[10], (DEPTH, HEAD_DIM), 0.02),
        "lam_q": nrm(ks[11], (DEPTH, 2, HEAD_DIM), 0.1),
        "lam_k": nrm(ks[12], (DEPTH, 2, HEAD_DIM), 0.1),
        "subln_g": 1.0 + nrm(ks[13], (DEPTH, V_HEAD_DIM), 0.02),
        "w_conv": nrm(ks[14], (DEPTH, CONV_K, CONV_W), CONV_K ** -0.5),
        "w_br": nrm(ks[15], (DEPTH, N_BRANCH, BRANCH_W, D_MODEL), BRANCH_W ** -0.5),
        "w_o": nrm(ks[16], (DEPTH, D_MODEL, D_MODEL), D_MODEL ** -0.5),
        "w1": nrm(ks[17], (DEPTH, D_MODEL, D_FF), D_MODEL ** -0.5),
        "w2": nrm(ks[18], (DEPTH, D_FF, D_MODEL), D_FF ** -0.5),
    }


def reference(x, c, ctx, c_ctx, ln1_g, ln2_g, w_ada, b_ada, w_in, qn_g, kn_g, lam_q, lam_k,
              subln_g, w_conv, w_br, w_o, w1, w2):
    b, s, _ = x.shape
    rope = axial_rope_tables(s)
    c_silu = jax.nn.silu(c)
    cc_silu = jax.nn.silu(c_ctx)
    xc = ctx
    for l in range(DEPTH):
        last = l == DEPTH - 1
        mod = (c_silu @ w_ada[l] + b_ada[l]).reshape(b, 1, 6, D_MODEL)
        modc = (cc_silu @ w_ada[l] + b_ada[l]).reshape(6, D_MODEL)
        lam_init = 0.8 - 0.6 * math.exp(-0.3 * l)
        lq = lam_q[l].astype(jnp.float32)
        lk = lam_k[l].astype(jnp.float32)
        lam = jnp.exp(jnp.sum(lq[0] * lk[0])) - jnp.exp(jnp.sum(lq[1] * lk[1])) + lam_init
        w_in_l = w_in[l]

        hc = modulate(rms_norm(xc, ln1_g[l]), modc[0], modc[1])
        if last:
            kc, vc = kv_from(hc @ w_in_l[:, K0:CB0], kn_g[l])
        else:
            pc = hc @ w_in_l
            qc = q_from(pc[..., Q0:K0], qn_g[l])
            kc, vc = kv_from(pc[..., K0:CB0], kn_g[l])

        h = modulate(rms_norm(x, ln1_g[l]), mod[:, :, 0], mod[:, :, 1])
        p = h @ w_in_l
        q = apply_axial_rope(q_from(p[..., Q0:K0], qn_g[l]), rope)
        k, v = kv_from(p[..., K0:CB0], kn_g[l])
        k = apply_axial_rope(k, rope)
        k_all = jnp.concatenate([kc, k], axis=1)
        v_all = jnp.concatenate([vc, v], axis=1)
        attn = head_out(blocked_diff_attention(q, k_all, v_all, lam), subln_g[l], lam_init)
        x = x + mod[:, :, 2] * mixer_merge(p, attn, w_conv[l], w_br[l], w_o[l])
        h2 = modulate(rms_norm(x, ln2_g[l]), mod[:, :, 3], mod[:, :, 4])
        x = x + mod[:, :, 5] * sq_relu_mlp(h2, w1[l], w2[l])

        if not last:
            attn_c = head_out(diff_attention(qc, kc, vc, lam), subln_g[l], lam_init)
            xc = xc + modc[2] * mixer_merge(pc, attn_c, w_conv[l], w_br[l], w_o[l])
            hc2 = modulate(rms_norm(xc, ln2_g[l]), modc[3], modc[4])
            xc = xc + modc[5] * sq_relu_mlp(hc2, w1[l], w2[l])
    return x
```

```python
import functools
import math

import jax
import jax.numpy as jnp
from jax import lax
from jax.experimental import pallas as pl
from jax.experimental.pallas import tpu as pltpu

D_MODEL = 2048
GRID_W = 64
N_HEADS = 8
HEAD_DIM = 64
V_HEAD_DIM = 128
BRANCH_W = 1024
FOURIER_GROUPS = 4
GROUP_W = BRANCH_W // FOURIER_GROUPS
N_BRANCH = 3
CONV_K = 3
D_FF = 4 * D_MODEL
IN_W = 7 * BRANCH_W + N_BRANCH * D_MODEL
ROPE_THETA = 10000.0
EPS = 1e-6

COL_TILE = 1024
TILE_Q, TILE_K, TILE_V, TILE_CB, TILE_CC, TILE_CX, TILE_F, TILE_G = range(8)
N_COL_TILES = IN_W // COL_TILE
GATE_TILES = D_MODEL // COL_TILE

LANES = 128
SUBLANES = 8
MOD_ROWS = 8
CTX_MOD_ROW = 4

F32 = jnp.float32
BF16 = jnp.bfloat16


def _vmem(nbytes):
    return pltpu.CompilerParams(vmem_limit_bytes=nbytes)


def _sigmoid(x):
    return 1.0 / (1.0 + jnp.exp(-x))


def _rms_modulate(x, g, shift, scale):
    y = x * lax.rsqrt(jnp.mean(x * x, axis=-1, keepdims=True) + EPS) * g
    return y * (1.0 + scale) + shift


def _dot(a, b):
    return jnp.dot(a, b, preferred_element_type=F32)


def _dot_t(a, b):
    return lax.dot_general(a, b, (((1,), (1,)), ((), ())), preferred_element_type=F32)


def _mod_kernel(c_ref, w_ref, b_ref, o_ref):
    c = c_ref[...]
    cs = c * _sigmoid(c)
    o_ref[0] = jnp.dot(cs, w_ref[0], preferred_element_type=F32,
                       precision=lax.Precision.HIGHEST) + b_ref[0]


def _mod_call(c_rows, w_ada, b_ada):
    depth, d, n = w_ada.shape
    tn = 1536
    return pl.pallas_call(
        _mod_kernel,
        grid=(depth, n // tn),
        in_specs=[
            pl.BlockSpec((MOD_ROWS, d), lambda l, j: (0, 0)),
            pl.BlockSpec((1, d, tn), lambda l, j: (l, 0, j)),
            pl.BlockSpec((1, 1, tn), lambda l, j: (l, 0, j)),
        ],
        out_specs=pl.BlockSpec((1, MOD_ROWS, tn), lambda l, j: (l, 0, j)),
        out_shape=jax.ShapeDtypeStruct((depth, MOD_ROWS, n), F32),
        compiler_params=_vmem(40 << 20),
        name="mod",
    )(c_rows, w_ada, b_ada.reshape(depth, 1, n))


def _inproj_kernel(x_ref, mod_ref, g_ref, w_ref, qg_ref, kg_ref, cos_ref, sin_ref,
                   gsum_ref, o_ref, h_ref, *, first_tile):
    j = pl.program_id(1)
    col = j + first_tile

    @pl.when(j == 0)
    def _():
        h = _rms_modulate(x_ref[...], g_ref[...], mod_ref[0, 0:1, :], mod_ref[0, 1:2, :])
        h_ref[...] = h.astype(BF16)

    acc = _dot(h_ref[...], w_ref[...])

    @pl.when(col <= TILE_K)
    def _():
        gain = jnp.where(col == TILE_Q, qg_ref[...] * (HEAD_DIM ** -0.5), kg_ref[...])
        cos = cos_ref[...]
        sin = sin_ref[...]
        gsum = gsum_ref[...]
        lane = lax.broadcasted_iota(jnp.int32, (1, LANES), 1)
        first_half = (lane % (HEAD_DIM // 2)) < (HEAD_DIM // 4)
        for hh in range(N_HEADS):
            a = acc[:, hh * LANES:(hh + 1) * LANES]
            sq = a * a
            sq_hi = sq.astype(BF16)
            sq_lo = (sq - sq_hi.astype(F32)).astype(BF16)
            ssq = _dot(sq_hi, gsum) + _dot(sq_lo, gsum)
            y = a * lax.rsqrt(ssq * (1.0 / HEAD_DIM) + EPS) * gain
            partner = jnp.where(first_half,
                                pltpu.roll(y, LANES - HEAD_DIM // 4, 1),
                                pltpu.roll(y, HEAD_DIM // 4, 1))
            o_ref[:, hh * LANES:(hh + 1) * LANES] = (y * cos + partner * sin).astype(BF16)

    @pl.when((col > TILE_K) & (col < TILE_G))
    def _():
        o_ref[...] = acc.astype(BF16)

    @pl.when(col >= TILE_G)
    def _():
        o_ref[...] = _sigmoid(acc).astype(BF16)


def _inproj_call(x2d, mod, mod_row, ln_g, w_in, qg, kg, cos_t, sin_t, gsum, *,
                 first_tile, n_tiles):
    m, d = x2d.shape
    tm = 1024
    seq_tiles = cos_t.shape[0] // tm
    return pl.pallas_call(
        functools.partial(_inproj_kernel, first_tile=first_tile),
        grid=(m // tm, n_tiles),
        in_specs=[
            pl.BlockSpec((tm, d), lambda i, j: (i, 0)),
            pl.BlockSpec((1, 6, d), lambda i, j: (mod_row(i), 0, 0)),
            pl.BlockSpec((1, d), lambda i, j: (0, 0)),
            pl.BlockSpec((d, COL_TILE), lambda i, j: (0, j + first_tile)),
            pl.BlockSpec((1, LANES), lambda i, j: (0, 0)),
            pl.BlockSpec((1, LANES), lambda i, j: (0, 0)),
            pl.BlockSpec((tm, LANES), lambda i, j: (i % seq_tiles, 0)),
            pl.BlockSpec((tm, LANES), lambda i, j: (i % seq_tiles, 0)),
            pl.BlockSpec((LANES, LANES), lambda i, j: (0, 0)),
        ],
        out_specs=pl.BlockSpec((tm, COL_TILE), lambda i, j: (i, j)),
        out_shape=jax.ShapeDtypeStruct((m, n_tiles * COL_TILE), BF16),
        scratch_shapes=[pltpu.VMEM((tm, d), BF16)],
        compiler_params=_vmem(52 << 20),
        name="inproj",
    )(x2d, mod, ln_g, w_in, qg, kg, cos_t, sin_t, gsum)


def _attn_kernel(lq_ref, lk_ref, sg_ref, q_ref, *refs, lam_init, n_kv):
    kv_refs, o_ref = refs[:-1], refs[-1]
    e = jnp.exp(jnp.sum(lq_ref[...] * lk_ref[...], axis=-1, keepdims=True))
    lam = e[0:1, :] - e[1:2, :] + lam_init

    q = q_ref[...]
    ks = [kv_refs[2 * t][...] for t in range(n_kv)]
    vs = [kv_refs[2 * t + 1][...] for t in range(n_kv)]

    probs, recips = [], []
    for c in range(2):
        sl = slice(c * HEAD_DIM, (c + 1) * HEAD_DIM)
        s = [_dot_t(q[:, sl], k[:, sl]) for k in ks]
        m = functools.reduce(jnp.maximum, [jnp.max(x, axis=-1, keepdims=True) for x in s])
        p = [jnp.exp(x - m) for x in s]
        l = functools.reduce(jnp.add, [jnp.sum(x, axis=-1, keepdims=True) for x in p])
        probs.append(p)
        recips.append(1.0 / l)

    r0 = recips[0]
    r1 = recips[1] * lam
    o = None
    for t in range(n_kv):
        a = (probs[0][t] * r0 - probs[1][t] * r1).astype(BF16)
        part = _dot(a, vs[t])
        o = part if o is None else o + part

    y = o * lax.rsqrt(jnp.mean(o * o, axis=-1, keepdims=True) + EPS) * sg_ref[...]
    o_ref[...] = (y * (1.0 - lam_init)).astype(BF16)


def _attn_call(lam_q, lam_k, subln_g, q_arr, q_tile0, kv_list, *, n_batch, q_len, lam_init):
    tq = min(256, q_len)
    q_tiles = q_len // tq
    in_specs = [
        pl.BlockSpec((2, HEAD_DIM), lambda b, h, i: (0, 0)),
        pl.BlockSpec((2, HEAD_DIM), lambda b, h, i: (0, 0)),
        pl.BlockSpec((1, V_HEAD_DIM), lambda b, h, i: (0, 0)),
        pl.BlockSpec((tq, LANES), lambda b, h, i: (b * q_tiles + i, q_tile0 + h)),
    ]
    args = [lam_q, lam_k, subln_g, q_arr]
    for arr, rows, k0, v0 in kv_list:
        in_specs.append(pl.BlockSpec((rows, LANES), lambda b, h, i, k0=k0: (b, k0 + h)))
        in_specs.append(pl.BlockSpec((rows, LANES), lambda b, h, i, v0=v0: (b, v0 + h)))
        args += [arr, arr]
    return pl.pallas_call(
        functools.partial(_attn_kernel, lam_init=lam_init, n_kv=len(kv_list)),
        grid=(n_batch, N_HEADS, q_tiles),
        in_specs=in_specs,
        out_specs=pl.BlockSpec((tq, LANES), lambda b, h, i: (b * q_tiles + i, h)),
        out_shape=jax.ShapeDtypeStruct((n_batch * q_len, N_HEADS * V_HEAD_DIM), BF16),
        compiler_params=_vmem(52 << 20),
        name="attn",
    )(*args)


def _fourier_kernel(x_ref, cc_ref, sc_ref, pos_ref, o_ref, y_ref, *, n_pos):
    @pl.when(pl.program_id(1) == 0)
    def _():
        cc = cc_ref[...]
        sc = sc_ref[...]
        for g in range(FOURIER_GROUPS):
            xg = x_ref[:, g * GROUP_W:(g + 1) * GROUP_W]
            y_ref[0:n_pos, g * GROUP_W:(g + 1) * GROUP_W] = _dot(xg, cc).astype(BF16)
            y_ref[n_pos:2 * n_pos, g * GROUP_W:(g + 1) * GROUP_W] = _dot(xg, sc).astype(BF16)

    o_ref[...] = _dot(pos_ref[...], y_ref[...]).astype(BF16)


def _fourier_call(p_arr, f_tile, chan_cos, chan_sin, pos_mat, *, n_batch, n_pos):
    tm = min(256, n_pos)
    row_tiles = n_pos // tm
    return pl.pallas_call(
        functools.partial(_fourier_kernel, n_pos=n_pos),
        grid=(n_batch, row_tiles),
        in_specs=[
            pl.BlockSpec((n_pos, BRANCH_W), lambda b, i: (b, f_tile),
                         pipeline_mode=pl.Buffered(1)),
            pl.BlockSpec((GROUP_W, GROUP_W), lambda b, i: (0, 0)),
            pl.BlockSpec((GROUP_W, GROUP_W), lambda b, i: (0, 0)),
            pl.BlockSpec((tm, 2 * n_pos), lambda b, i: (i, 0)),
        ],
        out_specs=pl.BlockSpec((tm, BRANCH_W), lambda b, i: (b * row_tiles + i, 0)),
        out_shape=jax.ShapeDtypeStruct((n_batch * n_pos, BRANCH_W), BF16),
        scratch_shapes=[pltpu.VMEM((2 * n_pos, BRANCH_W), BF16)],
        compiler_params=_vmem(52 << 20),
        name="fourier",
    )(p_arr, chan_cos, chan_sin, pos_mat)


def _merge_kernel(x_ref, mod_ref, attn_ref, cb_ref, cc_ref, cx_ref, ccp_ref, cxp_ref,
                  ccn_ref, cxn_ref, four_ref, g0a_ref, g0b_ref, g1a_ref, g1b_ref, g2a_ref,
                  g2b_ref, wconv_ref, wbr_ref, wo_ref, o_ref, *, seq_len):
    tm = x_ref.shape[0]
    i = pl.program_id(0)
    pos0 = (i * tm) % seq_len

    u = cc_ref[...].astype(F32) * cx_ref[...].astype(F32)
    u_before = (ccp_ref[SUBLANES - 1:SUBLANES, :].astype(F32)
                * cxp_ref[SUBLANES - 1:SUBLANES, :].astype(F32))
    u_after = ccn_ref[0:1, :].astype(F32) * cxn_ref[0:1, :].astype(F32)
    u_before = jnp.where(pos0 == 0, 0.0, u_before)
    u_after = jnp.where(pos0 + tm == seq_len, 0.0, u_after)
    row = lax.broadcasted_iota(jnp.int32, (tm, 1), 0)
    u_prev = jnp.where(row == 0, u_before, pltpu.roll(u, 1, 0))
    u_next = jnp.where(row == tm - 1, u_after, pltpu.roll(u, tm - 1, 0))
    w = wconv_ref[...]
    conv = cb_ref[...].astype(F32) * (u_prev * w[0:1, :] + u * w[1:2, :] + u_next * w[2:3, :])

    def gate(a_ref, b_ref):
        return jnp.concatenate([a_ref[...], b_ref[...]], axis=1).astype(F32)

    mix = (gate(g0a_ref, g0b_ref) * _dot(attn_ref[...], wbr_ref[0])
           + gate(g1a_ref, g1b_ref) * _dot(conv.astype(BF16), wbr_ref[1])
           + gate(g2a_ref, g2b_ref) * _dot(four_ref[...], wbr_ref[2]))
    y = _dot(mix.astype(BF16), wo_ref[...])
    o_ref[...] = x_ref[...] + mod_ref[0, 2:3, :] * y


def _merge_call(x2d, mod, mod_row, attn_o, p_arr, four_o, w_conv, w_br, w_o, *, seq_len):
    m, d = x2d.shape
    tm = 256
    halo_per_tile = tm // SUBLANES
    last_halo = m // SUBLANES - 1
    const = dict(pipeline_mode=pl.Buffered(1))

    def col(tile):
        return pl.BlockSpec((tm, COL_TILE), lambda i: (i, tile))

    def before(tile):
        return pl.BlockSpec((SUBLANES, COL_TILE),
                            lambda i: (jnp.maximum(i * halo_per_tile - 1, 0), tile))

    def after(tile):
        return pl.BlockSpec((SUBLANES, COL_TILE),
                            lambda i: (jnp.minimum((i + 1) * halo_per_tile, last_halo), tile))

    gates = [col(TILE_G + t) for t in range(N_BRANCH * GATE_TILES)]

    return pl.pallas_call(
        functools.partial(_merge_kernel, seq_len=seq_len),
        grid=(m // tm,),
        in_specs=[
            pl.BlockSpec((tm, d), lambda i: (i, 0)),
            pl.BlockSpec((1, 6, d), lambda i: (mod_row(i), 0, 0)),
            pl.BlockSpec((tm, BRANCH_W), lambda i: (i, 0)),
            col(TILE_CB), col(TILE_CC), col(TILE_CX),
            before(TILE_CC), before(TILE_CX), after(TILE_CC), after(TILE_CX),
            pl.BlockSpec((tm, BRANCH_W), lambda i: (i, 0)),
            *gates,
            pl.BlockSpec((CONV_K, BRANCH_W), lambda i: (0, 0)),
            pl.BlockSpec((N_BRANCH, BRANCH_W, d), lambda i: (0, 0, 0), **const),
            pl.BlockSpec((d, d), lambda i: (0, 0), **const),
        ],
        out_specs=pl.BlockSpec((tm, d), lambda i: (i, 0)),
        out_shape=jax.ShapeDtypeStruct((m, d), F32),
        compiler_params=_vmem(56 << 20),
        name="merge",
    )(x2d, mod, attn_o, p_arr, p_arr, p_arr, p_arr, p_arr, p_arr, p_arr, four_o,
      *([p_arr] * len(gates)), w_conv, w_br, w_o)


def _mlp_kernel(x_ref, mod_ref, g_ref, w1_ref, w2_ref, o_ref, h_ref, acc_ref):
    j = pl.program_id(1)

    @pl.when(j == 0)
    def _():
        h = _rms_modulate(x_ref[...], g_ref[...], mod_ref[0, 3:4, :], mod_ref[0, 4:5, :])
        h_ref[...] = h.astype(BF16)
        acc_ref[...] = jnp.zeros_like(acc_ref)

    hid = jnp.maximum(_dot(h_ref[...], w1_ref[...]), 0.0)
    acc_ref[...] += _dot((hid * hid).astype(BF16), w2_ref[...])

    @pl.when(j == pl.num_programs(1) - 1)
    def _():
        o_ref[...] = x_ref[...] + mod_ref[0, 5:6, :] * acc_ref[...]


def _mlp_call(x2d, mod, mod_row, ln_g, w1, w2):
    m, d = x2d.shape
    d_ff = w1.shape[1]
    tm, tf = 512, 512
    return pl.pallas_call(
        _mlp_kernel,
        grid=(m // tm, d_ff // tf),
        in_specs=[
            pl.BlockSpec((tm, d), lambda i, j: (i, 0)),
            pl.BlockSpec((1, 6, d), lambda i, j: (mod_row(i), 0, 0)),
            pl.BlockSpec((1, d), lambda i, j: (0, 0)),
            pl.BlockSpec((d, tf), lambda i, j: (0, j)),
            pl.BlockSpec((tf, d), lambda i, j: (j, 0)),
        ],
        out_specs=pl.BlockSpec((tm, d), lambda i, j: (i, 0)),
        out_shape=jax.ShapeDtypeStruct((m, d), F32),
        scratch_shapes=[pltpu.VMEM((tm, d), BF16), pltpu.VMEM((tm, d), F32)],
        compiler_params=_vmem(52 << 20),
        name="mlp",
    )(x2d, mod, ln_g, w1, w2)


def _rope_tables(seq_len):
    rows = seq_len // GRID_W
    t_row = jnp.repeat(jnp.arange(rows), GRID_W).astype(F32)
    t_col = jnp.tile(jnp.arange(GRID_W), rows).astype(F32)
    axis_dim = HEAD_DIM // 2
    inv_freq = 1.0 / (ROPE_THETA ** (jnp.arange(0, axis_dim, 2, dtype=F32) / axis_dim))
    ang_r = t_row[:, None] * inv_freq[None, :]
    ang_c = t_col[:, None] * inv_freq[None, :]
    cos = jnp.concatenate([jnp.cos(ang_r)] * 2 + [jnp.cos(ang_c)] * 2, axis=-1)
    sin = jnp.concatenate([-jnp.sin(ang_r), jnp.sin(ang_r), -jnp.sin(ang_c), jnp.sin(ang_c)],
                          axis=-1)
    return jnp.tile(cos, (1, LANES // HEAD_DIM)), jnp.tile(sin, (1, LANES // HEAD_DIM))


def _dft_cos_sin(n, scale):
    k = jnp.arange(n, dtype=jnp.int32)
    ang = ((k[:, None] * k[None, :]) % n).astype(F32) * (2.0 * math.pi / n)
    return jnp.cos(ang) * scale, jnp.sin(ang) * scale


def _pos_dft(n):
    c, s = _dft_cos_sin(n, n ** -0.5)
    return jnp.concatenate([c, -s], axis=1).astype(BF16)


def kernel(x, c, ctx, c_ctx, ln1_g, ln2_g, w_ada, b_ada, w_in, qn_g, kn_g, lam_q, lam_k,
           subln_g, w_conv, w_br, w_o, w1, w2):
    n_batch, seq, d = x.shape
    ctx_len = ctx.shape[1]
    depth = w_ada.shape[0]
    assert seq % 1024 == 0 and ctx_len == 256 and n_batch <= CTX_MOD_ROW

    c_rows = jnp.zeros((MOD_ROWS, d), F32).at[:n_batch].set(c).at[CTX_MOD_ROW].set(c_ctx)
    mod = _mod_call(c_rows, w_ada, b_ada).reshape(depth, MOD_ROWS, 6, d)

    cos_l, sin_l = _rope_tables(seq)
    cos_c = jnp.ones((n_batch * ctx_len, LANES), F32)
    sin_c = jnp.zeros((n_batch * ctx_len, LANES), F32)
    lane = jnp.arange(LANES)
    gsum = (lane[:, None] // HEAD_DIM == lane[None, :] // HEAD_DIM).astype(BF16)
    chan_cos, chan_sin = _dft_cos_sin(GROUP_W, GROUP_W ** -0.5)
    chan_cos, chan_sin = chan_cos.astype(BF16), chan_sin.astype(BF16)
    pos_l = _pos_dft(seq)
    pos_c = _pos_dft(ctx_len)

    lat_tiles_per_batch = seq // 1024
    lat_row_1024 = lambda i: i // lat_tiles_per_batch
    lat_row_512 = lambda i: i // (seq // 512)
    lat_row_256 = lambda i: i // (seq // 256)
    ctx_row = lambda i: CTX_MOD_ROW

    xl = x.reshape(n_batch * seq, d)
    xc = ctx.reshape(n_batch * ctx_len, d)
    tile = lambda g: jnp.tile(g.reshape(1, -1), (1, LANES // g.shape[-1]))

    for l in range(depth):
        last = l == depth - 1
        lam_init = 0.8 - 0.6 * math.exp(-0.3 * l)
        w_in_l = w_in[l].astype(BF16)
        w_br_l = w_br[l].astype(BF16)
        w_o_l = w_o[l].astype(BF16)
        w1_l = w1[l].astype(BF16)
        w2_l = w2[l].astype(BF16)
        ln1 = ln1_g[l].reshape(1, d)
        ln2 = ln2_g[l].reshape(1, d)
        qg, kg, sg = tile(qn_g[l]), tile(kn_g[l]), subln_g[l].reshape(1, -1)
        mod_l = mod[l]

        c_first, c_tiles = (TILE_K, 2) if last else (TILE_Q, N_COL_TILES)
        pc = _inproj_call(xc, mod_l, ctx_row, ln1, w_in_l, qg, kg, cos_c, sin_c, gsum,
                          first_tile=c_first, n_tiles=c_tiles)
        pl_ = _inproj_call(xl, mod_l, lat_row_1024, ln1, w_in_l, qg, kg, cos_l, sin_l, gsum,
                           first_tile=TILE_Q, n_tiles=N_COL_TILES)

        heads = COL_TILE // LANES
        kc0, vc0 = (TILE_K - c_first) * heads, (TILE_V - c_first) * heads
        kv_ctx = (pc, ctx_len, kc0, vc0)
        kv_lat = (pl_, seq, TILE_K * heads, TILE_V * heads)
        attn_l = _attn_call(lam_q[l], lam_k[l], sg, pl_, TILE_Q * heads, [kv_ctx, kv_lat],
                            n_batch=n_batch, q_len=seq, lam_init=lam_init)
        four_l = _fourier_call(pl_, TILE_F, chan_cos, chan_sin, pos_l,
                               n_batch=n_batch, n_pos=seq)
        xl = _merge_call(xl, mod_l, lat_row_256, attn_l, pl_, four_l, w_conv[l], w_br_l, w_o_l,
                         seq_len=seq)
        xl = _mlp_call(xl, mod_l, lat_row_512, ln2, w1_l, w2_l)

        if not last:
            attn_c = _attn_call(lam_q[l], lam_k[l], sg, pc, TILE_Q * heads, [kv_ctx],
                                n_batch=n_batch, q_len=ctx_len, lam_init=lam_init)
            four_c = _fourier_call(pc, TILE_F, chan_cos, chan_sin, pos_c,
                                   n_batch=n_batch, n_pos=ctx_len)
            xc = _merge_call(xc, mod_l, ctx_row, attn_c, pc, four_c, w_conv[l], w_br_l, w_o_l,
                             seq_len=ctx_len)
            xc = _mlp_call(xc, mod_l, ctx_row, ln2, w1_l, w2_l)

    return xl.reshape(n_batch, seq, d)
```

```python
import functools
import math

import jax
import jax.numpy as jnp
from jax import lax
from jax.experimental import pallas as pl
from jax.experimental.pallas import tpu as pltpu

D_MODEL = 2048
GRID_W = 64
N_HEADS = 8
HEAD_DIM = 64
V_HEAD_DIM = 128
BRANCH_W = 1024
FOURIER_GROUPS = 4
GROUP_W = BRANCH_W // FOURIER_GROUPS
N_BRANCH = 3
CONV_K = 3
D_FF = 4 * D_MODEL
IN_W = 7 * BRANCH_W + N_BRANCH * D_MODEL
ROPE_THETA = 10000.0
EPS = 1e-6

COL_TILE = 1024
TILE_Q, TILE_K, TILE_V, TILE_CB, TILE_CC, TILE_CX, TILE_F, TILE_G = range(8)
N_COL_TILES = IN_W // COL_TILE
GATE_TILES = D_MODEL // COL_TILE

LANES = 128
SUBLANES = 8
ONES_ROWS = 16
LOG2_E = math.log2(math.e)
MOD_ROWS = 8
CTX_MOD_ROW = 4

F32 = jnp.float32
BF16 = jnp.bfloat16


def _vmem(nbytes):
    return pltpu.CompilerParams(vmem_limit_bytes=nbytes)


def _sigmoid(x):
    return 1.0 / (1.0 + jnp.exp(-x))


def _rms_modulate(x, g, shift, scale):
    y = x * lax.rsqrt(jnp.mean(x * x, axis=-1, keepdims=True) + EPS) * g
    return y * (1.0 + scale) + shift


def _dot(a, b):
    return jnp.dot(a, b, preferred_element_type=F32)


def _dot_t(a, b):
    return lax.dot_general(a, b, (((1,), (1,)), ((), ())), preferred_element_type=F32)


def _mod_kernel(c_ref, w_ref, b_ref, o_ref):
    c = c_ref[...]
    cs = c * _sigmoid(c)
    o_ref[0] = jnp.dot(cs, w_ref[0], preferred_element_type=F32,
                       precision=lax.Precision.HIGHEST) + b_ref[0]


def _mod_call(c_rows, w_ada, b_ada):
    depth, d, n = w_ada.shape
    tn = 1536
    return pl.pallas_call(
        _mod_kernel,
        grid=(depth, n // tn),
        in_specs=[
            pl.BlockSpec((MOD_ROWS, d), lambda l, j: (0, 0)),
            pl.BlockSpec((1, d, tn), lambda l, j: (l, 0, j)),
            pl.BlockSpec((1, 1, tn), lambda l, j: (l, 0, j)),
        ],
        out_specs=pl.BlockSpec((1, MOD_ROWS, tn), lambda l, j: (l, 0, j)),
        out_shape=jax.ShapeDtypeStruct((depth, MOD_ROWS, n), F32),
        compiler_params=_vmem(40 << 20),
        name="mod",
    )(c_rows, w_ada, b_ada.reshape(depth, 1, n))


def _inproj_kernel(x_ref, mod_ref, g_ref, w_ref, qg_ref, kg_ref, cos_ref, sin_ref,
                   gsum_ref, o_ref, h_ref, *, first_tile):
    j = pl.program_id(1)
    col = j + first_tile

    @pl.when(j == 0)
    def _():
        h = _rms_modulate(x_ref[...], g_ref[...], mod_ref[0, 0:1, :], mod_ref[0, 1:2, :])
        h_ref[...] = h.astype(BF16)

    acc = _dot(h_ref[...], w_ref[...])

    @pl.when(col <= TILE_K)
    def _():
        gain = jnp.where(col == TILE_Q, qg_ref[...] * (LOG2_E * HEAD_DIM ** -0.5), kg_ref[...])
        cos = cos_ref[...]
        sin = sin_ref[...]
        gsum = gsum_ref[...]
        lane = lax.broadcasted_iota(jnp.int32, (1, LANES), 1)
        first_half = (lane % (HEAD_DIM // 2)) < (HEAD_DIM // 4)
        for hh in range(N_HEADS):
            a = acc[:, hh * LANES:(hh + 1) * LANES]
            sq = a * a
            sq_hi = sq.astype(BF16)
            sq_lo = (sq - sq_hi.astype(F32)).astype(BF16)
            ssq = _dot(sq_hi, gsum) + _dot(sq_lo, gsum)
            y = a * lax.rsqrt(ssq * (1.0 / HEAD_DIM) + EPS) * gain
            partner = jnp.where(first_half,
                                pltpu.roll(y, LANES - HEAD_DIM // 4, 1),
                                pltpu.roll(y, HEAD_DIM // 4, 1))
            o_ref[:, hh * LANES:(hh + 1) * LANES] = (y * cos + partner * sin).astype(BF16)

    @pl.when((col > TILE_K) & (col < TILE_G))
    def _():
        o_ref[...] = acc.astype(BF16)

    @pl.when(col >= TILE_G)
    def _():
        o_ref[...] = _sigmoid(acc).astype(BF16)


def _inproj_call(x2d, mod, mod_row, ln_g, w_in, qg, kg, cos_t, sin_t, gsum, *,
                 first_tile, n_tiles):
    m, d = x2d.shape
    tm = 1024
    seq_tiles = cos_t.shape[0] // tm
    return pl.pallas_call(
        functools.partial(_inproj_kernel, first_tile=first_tile),
        grid=(m // tm, n_tiles),
        in_specs=[
            pl.BlockSpec((tm, d), lambda i, j: (i, 0)),
            pl.BlockSpec((1, 6, d), lambda i, j: (mod_row(i), 0, 0)),
            pl.BlockSpec((1, d), lambda i, j: (0, 0)),
            pl.BlockSpec((d, COL_TILE), lambda i, j: (0, j + first_tile)),
            pl.BlockSpec((1, LANES), lambda i, j: (0, 0)),
            pl.BlockSpec((1, LANES), lambda i, j: (0, 0)),
            pl.BlockSpec((tm, LANES), lambda i, j: (i % seq_tiles, 0)),
            pl.BlockSpec((tm, LANES), lambda i, j: (i % seq_tiles, 0)),
            pl.BlockSpec((LANES, LANES), lambda i, j: (0, 0)),
        ],
        out_specs=pl.BlockSpec((tm, COL_TILE), lambda i, j: (i, j)),
        out_shape=jax.ShapeDtypeStruct((m, n_tiles * COL_TILE), BF16),
        scratch_shapes=[pltpu.VMEM((tm, d), BF16)],
        compiler_params=_vmem(52 << 20),
        name="inproj",
    )(x2d, mod, ln_g, w_in, qg, kg, cos_t, sin_t, gsum)


def _attn_kernel(lq_ref, lk_ref, sg_ref, q_ref, *refs, lam_init, pieces, key_block):
    n = len(pieces)
    k_refs, v_refs = refs[0:2 * n:2], refs[1:2 * n:2]
    o_ref, vt_ref, kmax_ref = refs[2 * n:]
    tq = q_ref.shape[0]
    lane = lax.broadcasted_iota(jnp.int32, (1, LANES), 1)

    @pl.when(pl.program_id(2) == 0)
    def _():
        group = lax.broadcasted_iota(jnp.int32, (LANES, LANES), 0) // HEAD_DIM
        gsum = (group == lane // HEAD_DIM).astype(BF16)
        kmax = None
        off = 0
        for t in range(n):
            rows = pieces[t]
            vt_ref[0:V_HEAD_DIM, off:off + rows] = v_refs[t][...].astype(F32).T.astype(BF16)
            kf = k_refs[t][...].astype(F32)
            norms = _dot((kf * kf).astype(BF16), gsum)
            piece_max = jnp.max(norms, axis=0, keepdims=True)
            kmax = piece_max if kmax is None else jnp.maximum(kmax, piece_max)
            off += rows
        vt_ref[V_HEAD_DIM:, :] = jnp.ones((ONES_ROWS, off), BF16)
        kmax_ref[...] = kmax

    e = jnp.exp(jnp.sum(lq_ref[...] * lk_ref[...], axis=-1, keepdims=True))
    lam = e[0:1, :] - e[1:2, :] + lam_init

    q = q_ref[...]
    zero = jnp.zeros_like(q)
    lo = lane < HEAD_DIM
    q_both = jnp.concatenate([jnp.where(lo, q, zero), jnp.where(lo, zero, q)], axis=0)

    qf = q.astype(F32)
    sel = (lax.broadcasted_iota(jnp.int32, (SUBLANES, LANES), 0) == lane // HEAD_DIM).astype(BF16)
    q_norms = _dot_t(sel, (qf * qf).astype(BF16))
    kmax = kmax_ref[...]
    bounds = jnp.concatenate(
        [jnp.sqrt(q_norms[c:c + 1, :] * kmax[:, c * HEAD_DIM:c * HEAD_DIM + 1]) * 1.01 + 1e-6
         for c in range(2)], axis=1)

    def key_blocks():
        off = 0
        for t in range(n):
            for s in range(0, pieces[t], key_block):
                size = min(key_block, pieces[t] - s)
                yield k_refs[t], s, off + s, size
            off += pieces[t]

    def accumulate(shifts):
        blocks = list(key_blocks())
        scores = lambda blk: _dot_t(blk[0][blk[1]:blk[1] + blk[3], :], q_both)
        acc = jnp.zeros((V_HEAD_DIM + ONES_ROWS, 2 * tq), F32)
        st_next = scores(blocks[0])
        for idx, (_, _, pos, size) in enumerate(blocks):
            st = st_next
            if idx + 1 < len(blocks):
                st_next = scores(blocks[idx + 1])
            acc = acc + _dot(vt_ref[:, pos:pos + size], jnp.exp2(st - shifts).astype(BF16))
        return acc

    def finish(acc):
        sums = acc[V_HEAD_DIM:V_HEAD_DIM + 1, :]
        ot = (acc[0:V_HEAD_DIM, 0:tq] * (1.0 / sums[:, 0:tq])
              - acc[0:V_HEAD_DIM, tq:] * (lam / sums[:, tq:]))
        yt = ot * lax.rsqrt(jnp.mean(ot * ot, axis=0, keepdims=True) + EPS)
        o_ref[...] = (yt.T * (sg_ref[...] * (1.0 - lam_init))).astype(BF16)

    acc = accumulate(bounds)
    finish(acc)

    @pl.when(jnp.logical_not(jnp.min(acc[V_HEAD_DIM:V_HEAD_DIM + 1, :]) >= 1e-30))
    def _():
        m = None
        for k_ref, s, _, size in key_blocks():
            bm = jnp.max(_dot_t(k_ref[s:s + size, :], q_both), axis=0, keepdims=True)
            m = bm if m is None else jnp.maximum(m, bm)
        finish(accumulate(m))


def _attn_call(lam_q, lam_k, subln_g, q_arr, q_tile0, kv_list, *, n_batch, q_len, lam_init):
    tq = min(512, q_len)
    q_tiles = q_len // tq
    in_specs = [
        pl.BlockSpec((2, HEAD_DIM), lambda b, h, i: (0, 0)),
        pl.BlockSpec((2, HEAD_DIM), lambda b, h, i: (0, 0)),
        pl.BlockSpec((1, V_HEAD_DIM), lambda b, h, i: (0, 0)),
        pl.BlockSpec((tq, LANES), lambda b, h, i: (b * q_tiles + i, q_tile0 + h)),
    ]
    args = [lam_q, lam_k, subln_g, q_arr]
    for arr, rows, k0, v0 in kv_list:
        in_specs.append(pl.BlockSpec((rows, LANES), lambda b, h, i, k0=k0: (b, k0 + h)))
        in_specs.append(pl.BlockSpec((rows, LANES), lambda b, h, i, v0=v0: (b, v0 + h)))
        args += [arr, arr]
    pieces = tuple(rows for _, rows, _, _ in kv_list)
    return pl.pallas_call(
        functools.partial(_attn_kernel, lam_init=lam_init, pieces=pieces, key_block=256),
        grid=(n_batch, N_HEADS, q_tiles),
        in_specs=in_specs,
        out_specs=pl.BlockSpec((tq, LANES), lambda b, h, i: (b * q_tiles + i, h)),
        out_shape=jax.ShapeDtypeStruct((n_batch * q_len, N_HEADS * V_HEAD_DIM), BF16),
        scratch_shapes=[pltpu.VMEM((V_HEAD_DIM + ONES_ROWS, sum(pieces)), BF16),
                        pltpu.VMEM((1, LANES), F32)],
        compiler_params=_vmem(32 << 20),
        name="attn",
    )(*args)


def _fourier_kernel(x_ref, cc_ref, sc_ref, pos_ref, o_ref, y_ref, *, n_pos):
    @pl.when(pl.program_id(1) == 0)
    def _():
        cc = cc_ref[...]
        sc = sc_ref[...]
        for g in range(FOURIER_GROUPS):
            xg = x_ref[:, g * GROUP_W:(g + 1) * GROUP_W]
            y_ref[0:n_pos, g * GROUP_W:(g + 1) * GROUP_W] = _dot(xg, cc).astype(BF16)
            y_ref[n_pos:2 * n_pos, g * GROUP_W:(g + 1) * GROUP_W] = _dot(xg, sc).astype(BF16)

    o_ref[...] = _dot(pos_ref[...], y_ref[...]).astype(BF16)


def _fourier_call(p_arr, f_tile, chan_cos, chan_sin, pos_mat, *, n_batch, n_pos):
    tm = min(256, n_pos)
    row_tiles = n_pos // tm
    return pl.pallas_call(
        functools.partial(_fourier_kernel, n_pos=n_pos),
        grid=(n_batch, row_tiles),
        in_specs=[
            pl.BlockSpec((n_pos, BRANCH_W), lambda b, i: (b, f_tile),
                         pipeline_mode=pl.Buffered(1)),
            pl.BlockSpec((GROUP_W, GROUP_W), lambda b, i: (0, 0)),
            pl.BlockSpec((GROUP_W, GROUP_W), lambda b, i: (0, 0)),
            pl.BlockSpec((tm, 2 * n_pos), lambda b, i: (i, 0)),
        ],
        out_specs=pl.BlockSpec((tm, BRANCH_W), lambda b, i: (b * row_tiles + i, 0)),
        out_shape=jax.ShapeDtypeStruct((n_batch * n_pos, BRANCH_W), BF16),
        scratch_shapes=[pltpu.VMEM((2 * n_pos, BRANCH_W), BF16)],
        compiler_params=_vmem(52 << 20),
        name="fourier",
    )(p_arr, chan_cos, chan_sin, pos_mat)


def _merge_kernel(x_ref, mod_ref, attn_ref, cb_ref, cc_ref, cx_ref, ccp_ref, cxp_ref,
                  ccn_ref, cxn_ref, four_ref, g0a_ref, g0b_ref, g1a_ref, g1b_ref, g2a_ref,
                  g2b_ref, wconv_ref, wbr_ref, wo_ref, o_ref, *, seq_len):
    tm = x_ref.shape[0]
    i = pl.program_id(0)
    pos0 = (i * tm) % seq_len

    u = cc_ref[...].astype(F32) * cx_ref[...].astype(F32)
    u_before = (ccp_ref[SUBLANES - 1:SUBLANES, :].astype(F32)
                * cxp_ref[SUBLANES - 1:SUBLANES, :].astype(F32))
    u_after = ccn_ref[0:1, :].astype(F32) * cxn_ref[0:1, :].astype(F32)
    u_before = jnp.where(pos0 == 0, 0.0, u_before)
    u_after = jnp.where(pos0 + tm == seq_len, 0.0, u_after)
    row = lax.broadcasted_iota(jnp.int32, (tm, 1), 0)
    u_prev = jnp.where(row == 0, u_before, pltpu.roll(u, 1, 0))
    u_next = jnp.where(row == tm - 1, u_after, pltpu.roll(u, tm - 1, 0))
    w = wconv_ref[...]
    conv = cb_ref[...].astype(F32) * (u_prev * w[0:1, :] + u * w[1:2, :] + u_next * w[2:3, :])

    def gate(a_ref, b_ref):
        return jnp.concatenate([a_ref[...], b_ref[...]], axis=1).astype(F32)

    mix = (gate(g0a_ref, g0b_ref) * _dot(attn_ref[...], wbr_ref[0])
           + gate(g1a_ref, g1b_ref) * _dot(conv.astype(BF16), wbr_ref[1])
           + gate(g2a_ref, g2b_ref) * _dot(four_ref[...], wbr_ref[2]))
    y = _dot(mix.astype(BF16), wo_ref[...])
    o_ref[...] = x_ref[...] + mod_ref[0, 2:3, :] * y


def _merge_call(x2d, mod, mod_row, attn_o, p_arr, four_o, w_conv, w_br, w_o, *, seq_len):
    m, d = x2d.shape
    tm = 256
    halo_per_tile = tm // SUBLANES
    last_halo = m // SUBLANES - 1
    const = dict(pipeline_mode=pl.Buffered(1))

    def col(tile):
        return pl.BlockSpec((tm, COL_TILE), lambda i: (i, tile))

    def before(tile):
        return pl.BlockSpec((SUBLANES, COL_TILE),
                            lambda i: (jnp.maximum(i * halo_per_tile - 1, 0), tile))

    def after(tile):
        return pl.BlockSpec((SUBLANES, COL_TILE),
                            lambda i: (jnp.minimum((i + 1) * halo_per_tile, last_halo), tile))

    gates = [col(TILE_G + t) for t in range(N_BRANCH * GATE_TILES)]

    return pl.pallas_call(
        functools.partial(_merge_kernel, seq_len=seq_len),
        grid=(m // tm,),
        in_specs=[
            pl.BlockSpec((tm, d), lambda i: (i, 0)),
            pl.BlockSpec((1, 6, d), lambda i: (mod_row(i), 0, 0)),
            pl.BlockSpec((tm, BRANCH_W), lambda i: (i, 0)),
            col(TILE_CB), col(TILE_CC), col(TILE_CX),
            before(TILE_CC), before(TILE_CX), after(TILE_CC), after(TILE_CX),
            pl.BlockSpec((tm, BRANCH_W), lambda i: (i, 0)),
            *gates,
            pl.BlockSpec((CONV_K, BRANCH_W), lambda i: (0, 0)),
            pl.BlockSpec((N_BRANCH, BRANCH_W, d), lambda i: (0, 0, 0), **const),
            pl.BlockSpec((d, d), lambda i: (0, 0), **const),
        ],
        out_specs=pl.BlockSpec((tm, d), lambda i: (i, 0)),
        out_shape=jax.ShapeDtypeStruct((m, d), F32),
        compiler_params=_vmem(56 << 20),
        name="merge",
    )(x2d, mod, attn_o, p_arr, p_arr, p_arr, p_arr, p_arr, p_arr, p_arr, four_o,
      *([p_arr] * len(gates)), w_conv, w_br, w_o)


def _mlp_kernel(x_ref, mod_ref, g_ref, w1_ref, w2_ref, o_ref, h_ref, acc_ref):
    j = pl.program_id(1)

    @pl.when(j == 0)
    def _():
        h = _rms_modulate(x_ref[...], g_ref[...], mod_ref[0, 3:4, :], mod_ref[0, 4:5, :])
        h_ref[...] = h.astype(BF16)
        acc_ref[...] = jnp.zeros_like(acc_ref)

    hid = jnp.maximum(_dot(h_ref[...], w1_ref[...]), 0.0)
    acc_ref[...] += _dot((hid * hid).astype(BF16), w2_ref[...])

    @pl.when(j == pl.num_programs(1) - 1)
    def _():
        o_ref[...] = x_ref[...] + mod_ref[0, 5:6, :] * acc_ref[...]


def _mlp_call(x2d, mod, mod_row, ln_g, w1, w2):
    m, d = x2d.shape
    d_ff = w1.shape[1]
    tm, tf = 512, 512
    return pl.pallas_call(
        _mlp_kernel,
        grid=(m // tm, d_ff // tf),
        in_specs=[
            pl.BlockSpec((tm, d), lambda i, j: (i, 0)),
            pl.BlockSpec((1, 6, d), lambda i, j: (mod_row(i), 0, 0)),
            pl.BlockSpec((1, d), lambda i, j: (0, 0)),
            pl.BlockSpec((d, tf), lambda i, j: (0, j)),
            pl.BlockSpec((tf, d), lambda i, j: (j, 0)),
        ],
        out_specs=pl.BlockSpec((tm, d), lambda i, j: (i, 0)),
        out_shape=jax.ShapeDtypeStruct((m, d), F32),
        scratch_shapes=[pltpu.VMEM((tm, d), BF16), pltpu.VMEM((tm, d), F32)],
        compiler_params=_vmem(52 << 20),
        name="mlp",
    )(x2d, mod, ln_g, w1, w2)


def _rope_tables(seq_len):
    rows = seq_len // GRID_W
    t_row = jnp.repeat(jnp.arange(rows), GRID_W).astype(F32)
    t_col = jnp.tile(jnp.arange(GRID_W), rows).astype(F32)
    axis_dim = HEAD_DIM // 2
    inv_freq = 1.0 / (ROPE_THETA ** (jnp.arange(0, axis_dim, 2, dtype=F32) / axis_dim))
    ang_r = t_row[:, None] * inv_freq[None, :]
    ang_c = t_col[:, None] * inv_freq[None, :]
    cos = jnp.concatenate([jnp.cos(ang_r)] * 2 + [jnp.cos(ang_c)] * 2, axis=-1)
    sin = jnp.concatenate([-jnp.sin(ang_r), jnp.sin(ang_r), -jnp.sin(ang_c), jnp.sin(ang_c)],
                          axis=-1)
    return jnp.tile(cos, (1, LANES // HEAD_DIM)), jnp.tile(sin, (1, LANES // HEAD_DIM))


def _dft_cos_sin(n, scale):
    k = jnp.arange(n, dtype=jnp.int32)
    ang = ((k[:, None] * k[None, :]) % n).astype(F32) * (2.0 * math.pi / n)
    return jnp.cos(ang) * scale, jnp.sin(ang) * scale


def _pos_dft(n):
    c, s = _dft_cos_sin(n, n ** -0.5)
    return jnp.concatenate([c, -s], axis=1).astype(BF16)


def kernel(x, c, ctx, c_ctx, ln1_g, ln2_g, w_ada, b_ada, w_in, qn_g, kn_g, lam_q, lam_k,
           subln_g, w_conv, w_br, w_o, w1, w2):
    n_batch, seq, d = x.shape
    ctx_len = ctx.shape[1]
    depth = w_ada.shape[0]
    assert seq % 1024 == 0 and ctx_len == 256 and n_batch <= CTX_MOD_ROW

    c_rows = jnp.zeros((MOD_ROWS, d), F32).at[:n_batch].set(c).at[CTX_MOD_ROW].set(c_ctx)
    mod = _mod_call(c_rows, w_ada, b_ada).reshape(depth, MOD_ROWS, 6, d)

    cos_l, sin_l = _rope_tables(seq)
    cos_c = jnp.ones((n_batch * ctx_len, LANES), F32)
    sin_c = jnp.zeros((n_batch * ctx_len, LANES), F32)
    lane = jnp.arange(LANES)
    gsum = (lane[:, None] // HEAD_DIM == lane[None, :] // HEAD_DIM).astype(BF16)
    chan_cos, chan_sin = _dft_cos_sin(GROUP_W, GROUP_W ** -0.5)
    chan_cos, chan_sin = chan_cos.astype(BF16), chan_sin.astype(BF16)
    pos_l = _pos_dft(seq)
    pos_c = _pos_dft(ctx_len)

    lat_tiles_per_batch = seq // 1024
    lat_row_1024 = lambda i: i // lat_tiles_per_batch
    lat_row_512 = lambda i: i // (seq // 512)
    lat_row_256 = lambda i: i // (seq // 256)
    ctx_row = lambda i: CTX_MOD_ROW

    xl = x.reshape(n_batch * seq, d)
    xc = ctx.reshape(n_batch * ctx_len, d)
    tile = lambda g: jnp.tile(g.reshape(1, -1), (1, LANES // g.shape[-1]))

    for l in range(depth):
        last = l == depth - 1
        lam_init = 0.8 - 0.6 * math.exp(-0.3 * l)
        w_in_l = w_in[l].astype(BF16)
        w_br_l = w_br[l].astype(BF16)
        w_o_l = w_o[l].astype(BF16)
        w1_l = w1[l].astype(BF16)
        w2_l = w2[l].astype(BF16)
        ln1 = ln1_g[l].reshape(1, d)
        ln2 = ln2_g[l].reshape(1, d)
        qg, kg, sg = tile(qn_g[l]), tile(kn_g[l]), subln_g[l].reshape(1, -1)
        mod_l = mod[l]

        c_first, c_tiles = (TILE_K, 2) if last else (TILE_Q, N_COL_TILES)
        pc = _inproj_call(xc, mod_l, ctx_row, ln1, w_in_l, qg, kg, cos_c, sin_c, gsum,
                          first_tile=c_first, n_tiles=c_tiles)
        pl_ = _inproj_call(xl, mod_l, lat_row_1024, ln1, w_in_l, qg, kg, cos_l, sin_l, gsum,
                           first_tile=TILE_Q, n_tiles=N_COL_TILES)

        heads = COL_TILE // LANES
        kc0, vc0 = (TILE_K - c_first) * heads, (TILE_V - c_first) * heads
        kv_ctx = (pc, ctx_len, kc0, vc0)
        kv_lat = (pl_, seq, TILE_K * heads, TILE_V * heads)
        attn_l = _attn_call(lam_q[l], lam_k[l], sg, pl_, TILE_Q * heads, [kv_ctx, kv_lat],
                            n_batch=n_batch, q_len=seq, lam_init=lam_init)
        four_l = _fourier_call(pl_, TILE_F, chan_cos, chan_sin, pos_l,
                               n_batch=n_batch, n_pos=seq)
        xl = _merge_call(xl, mod_l, lat_row_256, attn_l, pl_, four_l, w_conv[l], w_br_l, w_o_l,
                         seq_len=seq)
        xl = _mlp_call(xl, mod_l, lat_row_512, ln2, w1_l, w2_l)

        if not last:
            attn_c = _attn_call(lam_q[l], lam_k[l], sg, pc, TILE_Q * heads, [kv_ctx],
                                n_batch=n_batch, q_len=ctx_len, lam_init=lam_init)
            four_c = _fourier_call(pc, TILE_F, chan_cos, chan_sin, pos_c,
                                   n_batch=n_batch, n_pos=ctx_len)
            xc = _merge_call(xc, mod_l, ctx_row, attn_c, pc, four_c, w_conv[l], w_br_l, w_o_l,
                             seq_len=ctx_len)
            xc = _mlp_call(xc, mod_l, ctx_row, ln2, w1_l, w2_l)

    return xl.reshape(n_batch, seq, d)
```

```python
import functools
import math

import jax
import jax.numpy as jnp
from jax import lax
from jax.experimental import pallas as pl
from jax.experimental.pallas import tpu as pltpu

D_MODEL = 2048
GRID_W = 64
N_HEADS = 8
HEAD_DIM = 64
V_HEAD_DIM = 128
BRANCH_W = 1024
FOURIER_GROUPS = 4
GROUP_W = BRANCH_W // FOURIER_GROUPS
N_BRANCH = 3
CONV_K = 3
EPS = 1e-6
ROPE_THETA = 10000.0

COL_TILE = 1024
TILE_Q, TILE_K, TILE_V, TILE_CB, TILE_CC, TILE_CX, TILE_F, TILE_G = range(8)
N_GATE_TILES = N_BRANCH * D_MODEL // COL_TILE
PLAIN_V, PLAIN_CB, PLAIN_CC, PLAIN_CX, PLAIN_F = range(5)

LANES = 128
SUBLANES = 8
MXU_W = 256
ONES_ROWS = 16
LOG2_E = math.log2(math.e)
MOD_ROWS = 8
CTX_MOD_ROW = 4
FFT_R = 64
FFT_GROUP = 8

F32 = jnp.float32
BF16 = jnp.bfloat16


def _vmem(nbytes):
    return pltpu.CompilerParams(vmem_limit_bytes=nbytes)


def _sigmoid(x):
    return 1.0 / (1.0 + jnp.exp(-x))


def _rms_modulate(x, g, shift, scale):
    y = x * lax.rsqrt(jnp.mean(x * x, axis=-1, keepdims=True) + EPS) * g
    return y * (1.0 + scale) + shift


def _dot(a, b):
    return jnp.dot(a, b, preferred_element_type=F32)


def _dot_t(a, b):
    return lax.dot_general(a, b, (((1,), (1,)), ((), ())), preferred_element_type=F32)


def _mod_kernel(c_ref, w_ref, b_ref, o_ref):
    c = c_ref[...]
    cs = c * _sigmoid(c)
    o_ref[0] = jnp.dot(cs, w_ref[0], preferred_element_type=F32,
                       precision=lax.Precision.HIGHEST) + b_ref[0]


def _mod_call(c_rows, w_ada, b_ada):
    depth, d, n = w_ada.shape
    tn = 1536
    return pl.pallas_call(
        _mod_kernel,
        grid=(depth, n // tn),
        in_specs=[
            pl.BlockSpec((MOD_ROWS, d), lambda l, j: (0, 0)),
            pl.BlockSpec((1, d, tn), lambda l, j: (l, 0, j)),
            pl.BlockSpec((1, 1, tn), lambda l, j: (l, 0, j)),
        ],
        out_specs=pl.BlockSpec((1, MOD_ROWS, tn), lambda l, j: (l, 0, j)),
        out_shape=jax.ShapeDtypeStruct((depth, MOD_ROWS, n), F32),
        compiler_params=_vmem(40 << 20),
        name="mod",
    )(c_rows, w_ada, b_ada.reshape(depth, 1, n))


def _normmod_kernel(x_ref, mod_ref, g_ref, o_ref):
    h = _rms_modulate(x_ref[...], g_ref[0], mod_ref[0, 0, 0:1, :], mod_ref[0, 0, 1:2, :])
    o_ref[...] = h.astype(BF16)


def _normmod_call(x2d, mod, layer, mod_row, ln_g):
    m, d = x2d.shape
    tm = 512
    return pl.pallas_call(
        _normmod_kernel,
        grid=(m // tm,),
        in_specs=[
            pl.BlockSpec((tm, d), lambda i: (i, 0)),
            pl.BlockSpec((1, 1, 6, d), lambda i: (layer, mod_row(i * tm), 0, 0)),
            pl.BlockSpec((1, 1, d), lambda i: (layer, 0, 0)),
        ],
        out_specs=pl.BlockSpec((tm, d), lambda i: (i, 0)),
        out_shape=jax.ShapeDtypeStruct((m, d), BF16),
        compiler_params=_vmem(32 << 20),
        name="normmod",
    )(x2d, mod, ln_g.reshape(ln_g.shape[0], 1, d))


def _cast_weights(w_ref, wb_ref):
    @pl.when(pl.program_id(1) == 0)
    def _():
        wb_ref[...] = w_ref[0].astype(BF16)


def _proj_plain_kernel(h_ref, w_ref, o_ref, wb_ref):
    _cast_weights(w_ref, wb_ref)
    o_ref[...] = _dot(h_ref[...], wb_ref[...]).astype(BF16)


def _proj_gate_kernel(h_ref, w_ref, o_ref, wb_ref):
    _cast_weights(w_ref, wb_ref)
    acc = _dot(h_ref[...], wb_ref[...])
    o_ref[...] = (0.5 * jnp.tanh(0.5 * acc) + 0.5).astype(BF16)


def _proj_qk_kernel(h_ref, w_ref, qg_ref, kg_ref, cos_ref, sin_ref, o_ref, wb_ref):
    _cast_weights(w_ref, wb_ref)
    acc = _dot(h_ref[...], wb_ref[...])
    is_q = pl.program_id(0) == TILE_Q
    gain = jnp.where(is_q, qg_ref[0] * (LOG2_E * HEAD_DIM ** -0.5), kg_ref[0])
    cos = cos_ref[...]
    sin = sin_ref[...]
    lane = lax.broadcasted_iota(jnp.int32, (1, LANES), 1)
    first_half = (lane % (HEAD_DIM // 2)) < (HEAD_DIM // 4)
    gsum = (lax.broadcasted_iota(jnp.int32, (MXU_W, MXU_W), 0) // HEAD_DIM
            == lax.broadcasted_iota(jnp.int32, (MXU_W, MXU_W), 1) // HEAD_DIM).astype(BF16)
    for t in range(COL_TILE // MXU_W):
        a = acc[:, t * MXU_W:(t + 1) * MXU_W]
        ssq = _dot((a * a).astype(BF16), gsum)
        y = a * lax.rsqrt(ssq * (1.0 / HEAD_DIM) + EPS) * gain
        for u in range(MXU_W // LANES):
            yh = y[:, u * LANES:(u + 1) * LANES]
            partner = jnp.where(first_half,
                                pltpu.roll(yh, LANES - HEAD_DIM // 4, 1),
                                pltpu.roll(yh, HEAD_DIM // 4, 1))
            c0 = t * MXU_W + u * LANES
            o_ref[:, c0:c0 + LANES] = (yh * cos + partner * sin).astype(BF16)


def _proj_call(kind, h, w_in, layer, first_tile, n_tiles, extra=()):
    m, d = h.shape
    tm = 1024
    in_specs = [
        pl.BlockSpec((tm, d), lambda j, i: (i, 0)),
        pl.BlockSpec((1, d, COL_TILE), lambda j, i: (layer, 0, first_tile + j)),
    ]
    args = [h, w_in]
    if kind == "qk":
        qg, kg, cos_t, sin_t = extra
        seq_tiles = cos_t.shape[0] // tm
        in_specs += [
            pl.BlockSpec((1, 1, MXU_W), lambda j, i: (layer, 0, 0)),
            pl.BlockSpec((1, 1, MXU_W), lambda j, i: (layer, 0, 0)),
            pl.BlockSpec((tm, LANES), lambda j, i: (i % seq_tiles, 0)),
            pl.BlockSpec((tm, LANES), lambda j, i: (i % seq_tiles, 0)),
        ]
        args += [qg, kg, cos_t, sin_t]
    body = {"qk": _proj_qk_kernel, "plain": _proj_plain_kernel, "gate": _proj_gate_kernel}[kind]
    return pl.pallas_call(
        body,
        grid=(n_tiles, m // tm),
        in_specs=in_specs,
        out_specs=pl.BlockSpec((tm, COL_TILE), lambda j, i: (i, j)),
        out_shape=jax.ShapeDtypeStruct((m, n_tiles * COL_TILE), BF16),
        scratch_shapes=[pltpu.VMEM((d, COL_TILE), BF16)],
        compiler_params=_vmem(52 << 20),
        name="proj_" + kind,
    )(*args)


def _attn_kernel(lq_ref, lk_ref, sg_ref, q_ref, *refs, lam_init, pieces, key_block):
    n = len(pieces)
    k_refs, v_refs = refs[0:2 * n:2], refs[1:2 * n:2]
    o_ref, vt_ref, kmax_ref = refs[2 * n:]
    tq = q_ref.shape[0]
    lane = lax.broadcasted_iota(jnp.int32, (1, LANES), 1)

    @pl.when(pl.program_id(2) == 0)
    def _():
        group = lax.broadcasted_iota(jnp.int32, (LANES, LANES), 0) // HEAD_DIM
        gsum = (group == lane // HEAD_DIM).astype(BF16)
        kmax = None
        off = 0
        for t in range(n):
            rows = pieces[t]
            vt_ref[0:V_HEAD_DIM, off:off + rows] = v_refs[t][...].astype(F32).T.astype(BF16)
            kf = k_refs[t][...].astype(F32)
            norms = _dot((kf * kf).astype(BF16), gsum)
            piece_max = jnp.max(norms, axis=0, keepdims=True)
            kmax = piece_max if kmax is None else jnp.maximum(kmax, piece_max)
            off += rows
        vt_ref[V_HEAD_DIM:, :] = jnp.ones((ONES_ROWS, off), BF16)
        kmax_ref[...] = kmax

    e = jnp.exp(jnp.sum(lq_ref[0] * lk_ref[0], axis=-1, keepdims=True))
    lam = e[0:1, :] - e[1:2, :] + lam_init

    q = q_ref[...]
    zero = jnp.zeros_like(q)
    lo = lane < HEAD_DIM
    q_both = jnp.concatenate([jnp.where(lo, q, zero), jnp.where(lo, zero, q)], axis=0)

    qf = q.astype(F32)
    sel = (lax.broadcasted_iota(jnp.int32, (SUBLANES, LANES), 0) == lane // HEAD_DIM).astype(BF16)
    q_norms = _dot_t(sel, (qf * qf).astype(BF16))
    kmax = kmax_ref[...]
    bounds = jnp.concatenate(
        [jnp.sqrt(q_norms[c:c + 1, :] * kmax[:, c * HEAD_DIM:c * HEAD_DIM + 1]) * 1.01 + 1e-6
         for c in range(2)], axis=1)

    def key_blocks():
        off = 0
        for t in range(n):
            for s in range(0, pieces[t], key_block):
                size = min(key_block, pieces[t] - s)
                yield k_refs[t], s, off + s, size
            off += pieces[t]

    def accumulate(shifts):
        blocks = list(key_blocks())
        scores = lambda blk: _dot_t(blk[0][blk[1]:blk[1] + blk[3], :], q_both)
        acc = jnp.zeros((V_HEAD_DIM + ONES_ROWS, 2 * tq), F32)
        st_next = scores(blocks[0])
        for idx, (_, _, pos, size) in enumerate(blocks):
            st = st_next
            if idx + 1 < len(blocks):
                st_next = scores(blocks[idx + 1])
            acc = acc + _dot(vt_ref[:, pos:pos + size], jnp.exp2(st - shifts).astype(BF16))
        return acc

    def finish(acc):
        sums = acc[V_HEAD_DIM:V_HEAD_DIM + 1, :]
        ot = (acc[0:V_HEAD_DIM, 0:tq] * (1.0 / sums[:, 0:tq])
              - acc[0:V_HEAD_DIM, tq:] * (lam / sums[:, tq:]))
        yt = ot * lax.rsqrt(jnp.mean(ot * ot, axis=0, keepdims=True) + EPS)
        o_ref[...] = (yt.T * (sg_ref[0] * (1.0 - lam_init))).astype(BF16)

    acc = accumulate(bounds)
    finish(acc)

    @pl.when(jnp.logical_not(jnp.min(acc[V_HEAD_DIM:V_HEAD_DIM + 1, :]) >= 1e-30))
    def _():
        m = None
        for k_ref, s, _, size in key_blocks():
            bm = jnp.max(_dot_t(k_ref[s:s + size, :], q_both), axis=0, keepdims=True)
            m = bm if m is None else jnp.maximum(m, bm)
        finish(accumulate(m))


def _attn_call(lam_q, lam_k, subln_g, layer, q_arr, q_tile0, kv_list, *, n_batch, q_len,
               lam_init):
    tq = min(512, q_len)
    q_tiles = q_len // tq
    in_specs = [
        pl.BlockSpec((1, 2, HEAD_DIM), lambda b, h, i: (layer, 0, 0)),
        pl.BlockSpec((1, 2, HEAD_DIM), lambda b, h, i: (layer, 0, 0)),
        pl.BlockSpec((1, 1, V_HEAD_DIM), lambda b, h, i: (layer, 0, 0)),
        pl.BlockSpec((tq, LANES), lambda b, h, i: (b * q_tiles + i, q_tile0 + h)),
    ]
    args = [lam_q, lam_k, subln_g.reshape(subln_g.shape[0], 1, V_HEAD_DIM), q_arr]
    for k_arr, k0, v_arr, v0, rows in kv_list:
        in_specs.append(pl.BlockSpec((rows, LANES), lambda b, h, i, k0=k0: (b, k0 + h)))
        in_specs.append(pl.BlockSpec((rows, LANES), lambda b, h, i, v0=v0: (b, v0 + h)))
        args += [k_arr, v_arr]
    pieces = tuple(kv[4] for kv in kv_list)
    return pl.pallas_call(
        functools.partial(_attn_kernel, lam_init=lam_init, pieces=pieces, key_block=256),
        grid=(n_batch, N_HEADS, q_tiles),
        in_specs=in_specs,
        out_specs=pl.BlockSpec((tq, LANES), lambda b, h, i: (b * q_tiles + i, h)),
        out_shape=jax.ShapeDtypeStruct((n_batch * q_len, N_HEADS * V_HEAD_DIM), BF16),
        scratch_shapes=[pltpu.VMEM((V_HEAD_DIM + ONES_ROWS, sum(pieces)), BF16),
                        pltpu.VMEM((1, LANES), F32)],
        compiler_params=_vmem(32 << 20),
        name="attn",
    )(*args)


def _fourier_dense_kernel(x_ref, cc_ref, sc_ref, pos_ref, o_ref, y_ref, *, n_pos):
    @pl.when(pl.program_id(1) == 0)
    def _():
        cc = cc_ref[...]
        sc = sc_ref[...]
        for g in range(FOURIER_GROUPS):
            xg = x_ref[:, g * GROUP_W:(g + 1) * GROUP_W]
            y_ref[0:n_pos, g * GROUP_W:(g + 1) * GROUP_W] = _dot(xg, cc).astype(BF16)
            y_ref[n_pos:2 * n_pos, g * GROUP_W:(g + 1) * GROUP_W] = _dot(xg, sc).astype(BF16)

    o_ref[...] = _dot(pos_ref[...], y_ref[...]).astype(BF16)


def _fourier_dense_call(p_arr, f_tile, chan_cos, chan_sin, pos_mat, *, n_batch, n_pos):
    tm = min(256, n_pos)
    row_tiles = n_pos // tm
    return pl.pallas_call(
        functools.partial(_fourier_dense_kernel, n_pos=n_pos),
        grid=(n_batch, row_tiles),
        in_specs=[
            pl.BlockSpec((n_pos, BRANCH_W), lambda b, i: (b, f_tile),
                         pipeline_mode=pl.Buffered(1)),
            pl.BlockSpec((GROUP_W, GROUP_W), lambda b, i: (0, 0)),
            pl.BlockSpec((GROUP_W, GROUP_W), lambda b, i: (0, 0)),
            pl.BlockSpec((tm, 2 * n_pos), lambda b, i: (i, 0)),
        ],
        out_specs=pl.BlockSpec((tm, BRANCH_W), lambda b, i: (b * row_tiles + i, 0)),
        out_shape=jax.ShapeDtypeStruct((n_batch * n_pos, BRANCH_W), BF16),
        scratch_shapes=[pltpu.VMEM((2 * n_pos, BRANCH_W), BF16)],
        compiler_params=_vmem(32 << 20),
        name="fourier_dense",
    )(p_arr, chan_cos, chan_sin, pos_mat)


def _fourier_pos1_kernel(*refs):
    x_refs = refs[:FFT_GROUP]
    wa_ref, tc_ref, ts_ref, or_ref, oi_ref = refs[FFT_GROUP:]
    wa = wa_ref[...]
    for g in range(FFT_GROUP):
        a = _dot(wa, x_refs[g][...])
        tc = tc_ref[:, g * LANES:(g + 1) * LANES]
        ts = ts_ref[:, g * LANES:(g + 1) * LANES]
        for t in range(BRANCH_W // LANES):
            re = a[0:FFT_R, t * LANES:(t + 1) * LANES]
            im = a[FFT_R:, t * LANES:(t + 1) * LANES]
            c0 = g * BRANCH_W + t * LANES
            or_ref[:, c0:c0 + LANES] = (re * tc + im * ts).astype(BF16)
            oi_ref[:, c0:c0 + LANES] = (im * tc - re * ts).astype(BF16)


def _fourier_pos1_call(plain, wa, tw_cos, tw_sin, *, n_batch):
    tiles_per_row = plain.shape[1] // BRANCH_W
    x3 = plain.reshape(n_batch, FFT_R, FFT_R * plain.shape[1])
    chunks = FFT_R // FFT_GROUP

    def x_spec(g):
        return pl.BlockSpec((None, FFT_R, BRANCH_W),
                            lambda n, ch: (n, 0, (ch * FFT_GROUP + g) * tiles_per_row + PLAIN_F))

    out = jax.ShapeDtypeStruct((n_batch, FFT_R, FFT_R * BRANCH_W), BF16)
    out_spec = pl.BlockSpec((None, FFT_R, FFT_GROUP * BRANCH_W), lambda n, ch: (n, 0, ch))
    return pl.pallas_call(
        _fourier_pos1_kernel,
        grid=(n_batch, chunks),
        in_specs=[x_spec(g) for g in range(FFT_GROUP)] + [
            pl.BlockSpec((2 * FFT_R, FFT_R), lambda n, ch: (0, 0)),
            pl.BlockSpec((FFT_R, FFT_GROUP * LANES), lambda n, ch: (0, ch)),
            pl.BlockSpec((FFT_R, FFT_GROUP * LANES), lambda n, ch: (0, ch)),
        ],
        out_specs=[out_spec, out_spec],
        out_shape=[out, out],
        compiler_params=_vmem(32 << 20),
        name="fourier_pos1",
    )(*([x3] * FFT_GROUP), wa, tw_cos, tw_sin)


def _fourier_pos2_kernel(ar_ref, ai_ref, wc_ref, cs_ref, o_ref):
    wc = wc_ref[...]
    zr, zi = [], []
    for g in range(FFT_GROUP):
        rows = slice(g * FFT_R, (g + 1) * FFT_R)
        z = _dot(wc, jnp.concatenate([ar_ref[rows, :], ai_ref[rows, :]], axis=0))
        zr.append(z[0:FFT_R].astype(BF16))
        zi.append(z[FFT_R:].astype(BF16))
    zr = jnp.concatenate(zr, axis=0)
    zi = jnp.concatenate(zi, axis=0)
    cs = cs_ref[...]
    for q in range(FOURIER_GROUPS):
        cols = slice(q * GROUP_W, (q + 1) * GROUP_W)
        y = _dot(jnp.concatenate([zr[:, cols], zi[:, cols]], axis=1), cs)
        for g in range(FFT_GROUP):
            c0 = g * BRANCH_W + q * GROUP_W
            o_ref[:, c0:c0 + GROUP_W] = y[g * FFT_R:(g + 1) * FFT_R].astype(BF16)


def _fourier_pos2_call(a_re, a_im, wc, chan_cs, *, n_batch):
    n = FFT_R * FFT_R
    chunks = FFT_R // FFT_GROUP
    a_spec = pl.BlockSpec((FFT_GROUP * FFT_R, BRANCH_W), lambda n_, ch: (n_ * chunks + ch, 0))
    out = pl.pallas_call(
        _fourier_pos2_kernel,
        grid=(n_batch, chunks),
        in_specs=[
            a_spec, a_spec,
            pl.BlockSpec((2 * FFT_R, 2 * FFT_R), lambda n_, ch: (0, 0)),
            pl.BlockSpec((2 * GROUP_W, GROUP_W), lambda n_, ch: (0, 0)),
        ],
        out_specs=pl.BlockSpec((None, FFT_R, FFT_GROUP * BRANCH_W), lambda n_, ch: (n_, 0, ch)),
        out_shape=jax.ShapeDtypeStruct((n_batch, FFT_R, FFT_R * BRANCH_W), BF16),
        compiler_params=_vmem(32 << 20),
        name="fourier_pos2",
    )(a_re.reshape(n_batch * n, BRANCH_W), a_im.reshape(n_batch * n, BRANCH_W), wc, chan_cs)
    return out.reshape(n_batch * n, BRANCH_W)


def _merge_kernel(x_ref, mod_ref, attn_ref, cb_ref, cc_ref, cx_ref, ccp_ref, cxp_ref,
                  ccn_ref, cxn_ref, four_ref, g0a_ref, g0b_ref, g1a_ref, g1b_ref, g2a_ref,
                  g2b_ref, wconv_ref, wbr_ref, wo_ref, o_ref, *, seq_len):
    tm = x_ref.shape[0]
    i = pl.program_id(0)
    pos0 = (i * tm) % seq_len

    u = cc_ref[...].astype(F32) * cx_ref[...].astype(F32)
    u_before = (ccp_ref[SUBLANES - 1:SUBLANES, :].astype(F32)
                * cxp_ref[SUBLANES - 1:SUBLANES, :].astype(F32))
    u_after = ccn_ref[0:1, :].astype(F32) * cxn_ref[0:1, :].astype(F32)
    u_before = jnp.where(pos0 == 0, 0.0, u_before)
    u_after = jnp.where(pos0 + tm == seq_len, 0.0, u_after)
    row = lax.broadcasted_iota(jnp.int32, (tm, 1), 0)
    u_prev = jnp.where(row == 0, u_before, pltpu.roll(u, 1, 0))
    u_next = jnp.where(row == tm - 1, u_after, pltpu.roll(u, tm - 1, 0))
    w = wconv_ref[0]
    conv = cb_ref[...].astype(F32) * (u_prev * w[0:1, :] + u * w[1:2, :] + u_next * w[2:3, :])

    def gate(a_ref, b_ref):
        return jnp.concatenate([a_ref[...], b_ref[...]], axis=1).astype(F32)

    mix = (gate(g0a_ref, g0b_ref) * _dot(attn_ref[...], wbr_ref[0, 0])
           + gate(g1a_ref, g1b_ref) * _dot(conv.astype(BF16), wbr_ref[0, 1])
           + gate(g2a_ref, g2b_ref) * _dot(four_ref[...], wbr_ref[0, 2]))
    y = _dot(mix.astype(BF16), wo_ref[0])
    o_ref[...] = x_ref[...] + mod_ref[0, 0, 2:3, :] * y


def _merge_call(x2d, mod, layer, mod_row, attn_o, plain, gates, four_o, w_conv, w_br, w_o, *,
                seq_len):
    m, d = x2d.shape
    tm = 256
    halo_per_tile = tm // SUBLANES
    last_halo = m // SUBLANES - 1
    const = dict(pipeline_mode=pl.Buffered(1))

    def col(tile):
        return pl.BlockSpec((tm, COL_TILE), lambda i: (i, tile))

    def before(tile):
        return pl.BlockSpec((SUBLANES, COL_TILE),
                            lambda i: (jnp.maximum(i * halo_per_tile - 1, 0), tile))

    def after(tile):
        return pl.BlockSpec((SUBLANES, COL_TILE),
                            lambda i: (jnp.minimum((i + 1) * halo_per_tile, last_halo), tile))

    return pl.pallas_call(
        functools.partial(_merge_kernel, seq_len=seq_len),
        grid=(m // tm,),
        in_specs=[
            pl.BlockSpec((tm, d), lambda i: (i, 0)),
            pl.BlockSpec((1, 1, 6, d), lambda i: (layer, mod_row(i * tm), 0, 0)),
            pl.BlockSpec((tm, BRANCH_W), lambda i: (i, 0)),
            col(PLAIN_CB), col(PLAIN_CC), col(PLAIN_CX),
            before(PLAIN_CC), before(PLAIN_CX), after(PLAIN_CC), after(PLAIN_CX),
            pl.BlockSpec((tm, BRANCH_W), lambda i: (i, 0)),
            *[col(t) for t in range(N_GATE_TILES)],
            pl.BlockSpec((1, CONV_K, BRANCH_W), lambda i: (layer, 0, 0)),
            pl.BlockSpec((1, N_BRANCH, BRANCH_W, d), lambda i: (layer, 0, 0, 0), **const),
            pl.BlockSpec((1, d, d), lambda i: (layer, 0, 0), **const),
        ],
        out_specs=pl.BlockSpec((tm, d), lambda i: (i, 0)),
        out_shape=jax.ShapeDtypeStruct((m, d), F32),
        compiler_params=_vmem(56 << 20),
        name="merge",
    )(x2d, mod, attn_o, plain, plain, plain, plain, plain, plain, plain, four_o,
      *([gates] * N_GATE_TILES), w_conv, w_br, w_o)


def _mlp_kernel(x_ref, mod_ref, g_ref, w1_ref, w2_ref, o_ref, h_ref):
    j = pl.program_id(1)

    @pl.when(j == 0)
    def _():
        h = _rms_modulate(x_ref[...], g_ref[0], mod_ref[0, 0, 3:4, :], mod_ref[0, 0, 4:5, :])
        h_ref[...] = h.astype(BF16)
        o_ref[...] = jnp.zeros_like(o_ref)

    hid = jnp.maximum(_dot(h_ref[...], w1_ref[0]), 0.0)
    o_ref[...] += _dot((hid * hid).astype(BF16), w2_ref[0])

    @pl.when(j == pl.num_programs(1) - 1)
    def _():
        o_ref[...] = x_ref[...] + mod_ref[0, 0, 5:6, :] * o_ref[...]


def _mlp_call(x2d, mod, layer, mod_row, ln_g, w1, w2):
    m, d = x2d.shape
    d_ff = w1.shape[2]
    tm, tf = 1024, 512
    return pl.pallas_call(
        _mlp_kernel,
        grid=(m // tm, d_ff // tf),
        in_specs=[
            pl.BlockSpec((tm, d), lambda i, j: (i, 0)),
            pl.BlockSpec((1, 1, 6, d), lambda i, j: (layer, mod_row(i * tm), 0, 0)),
            pl.BlockSpec((1, 1, d), lambda i, j: (layer, 0, 0)),
            pl.BlockSpec((1, d, tf), lambda i, j: (layer, 0, j)),
            pl.BlockSpec((1, tf, d), lambda i, j: (layer, j, 0)),
        ],
        out_specs=pl.BlockSpec((tm, d), lambda i, j: (i, 0)),
        out_shape=jax.ShapeDtypeStruct((m, d), F32),
        scratch_shapes=[pltpu.VMEM((tm, d), BF16)],
        compiler_params=_vmem(56 << 20),
        name="mlp",
    )(x2d, mod, ln_g.reshape(ln_g.shape[0], 1, d), w1, w2)


def _rope_tables(seq_len):
    rows = seq_len // GRID_W
    t_row = jnp.repeat(jnp.arange(rows), GRID_W).astype(F32)
    t_col = jnp.tile(jnp.arange(GRID_W), rows).astype(F32)
    axis_dim = HEAD_DIM // 2
    inv_freq = 1.0 / (ROPE_THETA ** (jnp.arange(0, axis_dim, 2, dtype=F32) / axis_dim))
    ang_r = t_row[:, None] * inv_freq[None, :]
    ang_c = t_col[:, None] * inv_freq[None, :]
    cos = jnp.concatenate([jnp.cos(ang_r)] * 2 + [jnp.cos(ang_c)] * 2, axis=-1)
    sin = jnp.concatenate([-jnp.sin(ang_r), jnp.sin(ang_r), -jnp.sin(ang_c), jnp.sin(ang_c)],
                          axis=-1)
    return jnp.tile(cos, (1, LANES // HEAD_DIM)), jnp.tile(sin, (1, LANES // HEAD_DIM))


def _dft_cos_sin(rows, cols, period, scale):
    r = jnp.arange(rows, dtype=jnp.int32)
    c = jnp.arange(cols, dtype=jnp.int32)
    ang = ((r[:, None] * c[None, :]) % period).astype(F32) * (2.0 * math.pi / period)
    return jnp.cos(ang) * scale, jnp.sin(ang) * scale


def kernel(x, c, ctx, c_ctx, ln1_g, ln2_g, w_ada, b_ada, w_in, qn_g, kn_g, lam_q, lam_k,
           subln_g, w_conv, w_br, w_o, w1, w2):
    n_batch, seq, d = x.shape
    ctx_len = ctx.shape[1]
    depth = w_ada.shape[0]
    assert seq == FFT_R * FFT_R and ctx_len == 256 and n_batch <= CTX_MOD_ROW and d == D_MODEL

    c_rows = jnp.zeros((MOD_ROWS, d), F32).at[:n_batch].set(c).at[CTX_MOD_ROW].set(c_ctx)
    mod = _mod_call(c_rows, w_ada, b_ada).reshape(depth, MOD_ROWS, 6, d)

    cos_l, sin_l = _rope_tables(seq)
    cos_c = jnp.ones((n_batch * ctx_len, LANES), F32)
    sin_c = jnp.zeros((n_batch * ctx_len, LANES), F32)
    qg = jnp.tile(qn_g, (1, MXU_W // HEAD_DIM)).reshape(depth, 1, MXU_W)
    kg = jnp.tile(kn_g, (1, MXU_W // HEAD_DIM)).reshape(depth, 1, MXU_W)

    ch_c, ch_s = _dft_cos_sin(GROUP_W, GROUP_W, GROUP_W, GROUP_W ** -0.5)
    chan_cos, chan_sin = ch_c.astype(BF16), ch_s.astype(BF16)
    chan_cs = jnp.concatenate([chan_cos, chan_sin], axis=0)
    pc_c, pc_s = _dft_cos_sin(ctx_len, ctx_len, ctx_len, ctx_len ** -0.5)
    pos_c = jnp.concatenate([pc_c, -pc_s], axis=1).astype(BF16)
    r_c, r_s = _dft_cos_sin(FFT_R, FFT_R, FFT_R, FFT_R ** -0.5)
    wa = jnp.concatenate([r_c, -r_s], axis=0).astype(BF16)
    wc = jnp.concatenate([jnp.concatenate([r_c, r_s], axis=1),
                          jnp.concatenate([-r_s, r_c], axis=1)], axis=0).astype(BF16)
    tw_c, tw_s = _dft_cos_sin(FFT_R, FFT_R, seq, 1.0)
    tw_cos = jnp.repeat(tw_c, LANES, axis=1)
    tw_sin = jnp.repeat(tw_s, LANES, axis=1)

    w_br_b = w_br.astype(BF16)
    w_o_b = w_o.astype(BF16)
    w1_b = w1.astype(BF16)
    w2_b = w2.astype(BF16)

    lat_row = lambda row: row // seq
    ctx_row = lambda row: CTX_MOD_ROW
    heads = COL_TILE // LANES

    xl = x.reshape(n_batch * seq, d)
    xc = ctx.reshape(n_batch * ctx_len, d)

    for l in range(depth):
        last = l == depth - 1
        lam_init = 0.8 - 0.6 * math.exp(-0.3 * l)

        hc = _normmod_call(xc, mod, l, ctx_row, ln1_g)
        hl = _normmod_call(xl, mod, l, lat_row, ln1_g)
        qk_c = _proj_call("qk", hc, w_in, l, TILE_Q, 2, (qg, kg, cos_c, sin_c))
        qk_l = _proj_call("qk", hl, w_in, l, TILE_Q, 2, (qg, kg, cos_l, sin_l))
        plain_c = _proj_call("plain", hc, w_in, l, TILE_V, 1 if last else TILE_G - TILE_V)
        plain_l = _proj_call("plain", hl, w_in, l, TILE_V, TILE_G - TILE_V)
        gates_l = _proj_call("gate", hl, w_in, l, TILE_G, N_GATE_TILES)

        kv_ctx = (qk_c, heads, plain_c, PLAIN_V * heads, ctx_len)
        kv_lat = (qk_l, heads, plain_l, PLAIN_V * heads, seq)
        attn_l = _attn_call(lam_q, lam_k, subln_g, l, qk_l, 0, [kv_ctx, kv_lat],
                            n_batch=n_batch, q_len=seq, lam_init=lam_init)
        a_re, a_im = _fourier_pos1_call(plain_l, wa, tw_cos, tw_sin, n_batch=n_batch)
        four_l = _fourier_pos2_call(a_re, a_im, wc, chan_cs, n_batch=n_batch)
        xl = _merge_call(xl, mod, l, lat_row, attn_l, plain_l, gates_l, four_l, w_conv, w_br_b,
                         w_o_b, seq_len=seq)
        xl = _mlp_call(xl, mod, l, lat_row, ln2_g, w1_b, w2_b)

        if not last:
            gates_c = _proj_call("gate", hc, w_in, l, TILE_G, N_GATE_TILES)
            attn_c = _attn_call(lam_q, lam_k, subln_g, l, qk_c, 0, [kv_ctx],
                                n_batch=n_batch, q_len=ctx_len, lam_init=lam_init)
            four_c = _fourier_dense_call(plain_c, PLAIN_F, chan_cos, chan_sin, pos_c,
                                         n_batch=n_batch, n_pos=ctx_len)
            xc = _merge_call(xc, mod, l, ctx_row, attn_c, plain_c, gates_c, four_c, w_conv,
                             w_br_b, w_o_b, seq_len=ctx_len)
            xc = _mlp_call(xc, mod, l, ctx_row, ln2_g, w1_b, w2_b)

    return xl.reshape(n_batch, seq, d)
```

```python
import functools
import math

import jax
import jax.numpy as jnp
from jax import lax
from jax.experimental import pallas as pl
from jax.experimental.pallas import tpu as pltpu

D_MODEL = 2048
GRID_W = 64
N_HEADS = 8
HEAD_DIM = 64
V_HEAD_DIM = 128
BRANCH_W = 1024
FOURIER_GROUPS = 4
GROUP_W = BRANCH_W // FOURIER_GROUPS
N_BRANCH = 3
CONV_K = 3
EPS = 1e-6
ROPE_THETA = 10000.0

COL_TILE = 1024
TILE_Q, TILE_K, TILE_V, TILE_CB, TILE_CC, TILE_CX, TILE_F, TILE_G = range(8)
N_GATE_TILES = N_BRANCH * D_MODEL // COL_TILE
PLAIN_V, PLAIN_CB, PLAIN_CC, PLAIN_CX, PLAIN_F = range(5)

LANES = 128
SUBLANES = 8
MXU_W = 256
ONES_ROWS = 16
LOG2_E = math.log2(math.e)
MOD_ROWS = 8
CTX_MOD_ROW = 4
FFT_R = 64

F32 = jnp.float32
BF16 = jnp.bfloat16


def _vmem(nbytes):
    return pltpu.CompilerParams(vmem_limit_bytes=nbytes)


def _sigmoid(x):
    return 1.0 / (1.0 + jnp.exp(-x))


def _rms_modulate(x, g, shift, scale):
    y = x * lax.rsqrt(jnp.mean(x * x, axis=-1, keepdims=True) + EPS) * g
    return y * (1.0 + scale) + shift


def _dot(a, b):
    return jnp.dot(a, b, preferred_element_type=F32)


def _dot_t(a, b):
    return lax.dot_general(a, b, (((1,), (1,)), ((), ())), preferred_element_type=F32)


def _mod_kernel(c_ref, w_ref, b_ref, o_ref):
    c = c_ref[...]
    cs = c * _sigmoid(c)
    o_ref[0] = jnp.dot(cs, w_ref[0], preferred_element_type=F32,
                       precision=lax.Precision.HIGHEST) + b_ref[0]


def _mod_call(c_rows, w_ada, b_ada):
    depth, d, n = w_ada.shape
    tn = 1536
    return pl.pallas_call(
        _mod_kernel,
        grid=(depth, n // tn),
        in_specs=[
            pl.BlockSpec((MOD_ROWS, d), lambda l, j: (0, 0)),
            pl.BlockSpec((1, d, tn), lambda l, j: (l, 0, j)),
            pl.BlockSpec((1, 1, tn), lambda l, j: (l, 0, j)),
        ],
        out_specs=pl.BlockSpec((1, MOD_ROWS, tn), lambda l, j: (l, 0, j)),
        out_shape=jax.ShapeDtypeStruct((depth, MOD_ROWS, n), F32),
        compiler_params=_vmem(40 << 20),
        name="mod",
    )(c_rows, w_ada, b_ada.reshape(depth, 1, n))


def _normmod_kernel(x_ref, mod_ref, g_ref, o_ref):
    h = _rms_modulate(x_ref[...], g_ref[0], mod_ref[0, 0, 0:1, :], mod_ref[0, 0, 1:2, :])
    o_ref[...] = h.astype(BF16)


def _normmod_call(x2d, mod, layer, mod_row, ln_g):
    m, d = x2d.shape
    tm = 512
    return pl.pallas_call(
        _normmod_kernel,
        grid=(m // tm,),
        in_specs=[
            pl.BlockSpec((tm, d), lambda i: (i, 0)),
            pl.BlockSpec((1, 1, 6, d), lambda i: (layer, mod_row(i * tm), 0, 0)),
            pl.BlockSpec((1, 1, d), lambda i: (layer, 0, 0)),
        ],
        out_specs=pl.BlockSpec((tm, d), lambda i: (i, 0)),
        out_shape=jax.ShapeDtypeStruct((m, d), BF16),
        compiler_params=_vmem(32 << 20),
        name="normmod",
    )(x2d, mod, ln_g.reshape(ln_g.shape[0], 1, d))


def _cast_weights(w_ref, wb_ref):
    @pl.when(pl.program_id(1) == 0)
    def _():
        wb_ref[...] = w_ref[0].astype(BF16)


def _proj_plain_kernel(h_ref, w_ref, o_ref, wb_ref):
    _cast_weights(w_ref, wb_ref)
    o_ref[...] = _dot(h_ref[...], wb_ref[...]).astype(BF16)


def _proj_gate_kernel(h_ref, w_ref, o_ref, wb_ref):
    _cast_weights(w_ref, wb_ref)
    acc = _dot(h_ref[...], wb_ref[...])
    o_ref[...] = (0.5 * jnp.tanh(0.5 * acc) + 0.5).astype(BF16)


def _proj_qk_kernel(h_ref, w_ref, qg_ref, kg_ref, cos_ref, sin_ref, o_ref, wb_ref):
    _cast_weights(w_ref, wb_ref)
    acc = _dot(h_ref[...], wb_ref[...])
    is_q = pl.program_id(0) == TILE_Q
    gain = jnp.where(is_q, qg_ref[0] * (LOG2_E * HEAD_DIM ** -0.5), kg_ref[0])
    cos = cos_ref[...]
    sin = sin_ref[...]
    lane = lax.broadcasted_iota(jnp.int32, (1, LANES), 1)
    first_half = (lane % (HEAD_DIM // 2)) < (HEAD_DIM // 4)
    gsum = (lax.broadcasted_iota(jnp.int32, (MXU_W, MXU_W), 0) // HEAD_DIM
            == lax.broadcasted_iota(jnp.int32, (MXU_W, MXU_W), 1) // HEAD_DIM).astype(BF16)
    for t in range(COL_TILE // MXU_W):
        a = acc[:, t * MXU_W:(t + 1) * MXU_W]
        ssq = _dot((a * a).astype(BF16), gsum)
        y = a * lax.rsqrt(ssq * (1.0 / HEAD_DIM) + EPS) * gain
        for u in range(MXU_W // LANES):
            yh = y[:, u * LANES:(u + 1) * LANES]
            partner = jnp.where(first_half,
                                pltpu.roll(yh, LANES - HEAD_DIM // 4, 1),
                                pltpu.roll(yh, HEAD_DIM // 4, 1))
            c0 = t * MXU_W + u * LANES
            o_ref[:, c0:c0 + LANES] = (yh * cos + partner * sin).astype(BF16)


def _proj_call(kind, h, w_in, layer, first_tile, n_tiles, extra=()):
    m, d = h.shape
    tm = 1024
    in_specs = [
        pl.BlockSpec((tm, d), lambda j, i: (i, 0)),
        pl.BlockSpec((1, d, COL_TILE), lambda j, i: (layer, 0, first_tile + j)),
    ]
    args = [h, w_in]
    if kind == "qk":
        qg, kg, cos_t, sin_t = extra
        seq_tiles = cos_t.shape[0] // tm
        in_specs += [
            pl.BlockSpec((1, 1, MXU_W), lambda j, i: (layer, 0, 0)),
            pl.BlockSpec((1, 1, MXU_W), lambda j, i: (layer, 0, 0)),
            pl.BlockSpec((tm, LANES), lambda j, i: (i % seq_tiles, 0)),
            pl.BlockSpec((tm, LANES), lambda j, i: (i % seq_tiles, 0)),
        ]
        args += [qg, kg, cos_t, sin_t]
    body = {"qk": _proj_qk_kernel, "plain": _proj_plain_kernel, "gate": _proj_gate_kernel}[kind]
    return pl.pallas_call(
        body,
        grid=(n_tiles, m // tm),
        in_specs=in_specs,
        out_specs=pl.BlockSpec((tm, COL_TILE), lambda j, i: (i, j)),
        out_shape=jax.ShapeDtypeStruct((m, n_tiles * COL_TILE), BF16),
        scratch_shapes=[pltpu.VMEM((d, COL_TILE), BF16)],
        compiler_params=_vmem(52 << 20),
        name="proj_" + kind,
    )(*args)


def _attn_kernel(lq_ref, lk_ref, sg_ref, q_ref, *refs, lam_init, pieces, key_block):
    n = len(pieces)
    k_refs, v_refs = refs[0:2 * n:2], refs[1:2 * n:2]
    o_ref, vt_ref, kmax_ref = refs[2 * n:]
    tq = q_ref.shape[0]
    lane = lax.broadcasted_iota(jnp.int32, (1, LANES), 1)

    @pl.when(pl.program_id(2) == 0)
    def _():
        group = lax.broadcasted_iota(jnp.int32, (LANES, LANES), 0) // HEAD_DIM
        gsum = (group == lane // HEAD_DIM).astype(BF16)
        kmax = None
        off = 0
        for t in range(n):
            rows = pieces[t]
            vt_ref[0:V_HEAD_DIM, off:off + rows] = v_refs[t][...].astype(F32).T.astype(BF16)
            kf = k_refs[t][...].astype(F32)
            norms = _dot((kf * kf).astype(BF16), gsum)
            piece_max = jnp.max(norms, axis=0, keepdims=True)
            kmax = piece_max if kmax is None else jnp.maximum(kmax, piece_max)
            off += rows
        vt_ref[V_HEAD_DIM:, :] = jnp.ones((ONES_ROWS, off), BF16)
        kmax_ref[...] = kmax

    e = jnp.exp(jnp.sum(lq_ref[0] * lk_ref[0], axis=-1, keepdims=True))
    lam = e[0:1, :] - e[1:2, :] + lam_init

    q = q_ref[...]
    zero = jnp.zeros_like(q)
    lo = lane < HEAD_DIM
    q_both = jnp.concatenate([jnp.where(lo, q, zero), jnp.where(lo, zero, q)], axis=0)

    qf = q.astype(F32)
    sel = (lax.broadcasted_iota(jnp.int32, (SUBLANES, LANES), 0) == lane // HEAD_DIM).astype(BF16)
    q_norms = _dot_t(sel, (qf * qf).astype(BF16))
    kmax = kmax_ref[...]
    bounds = jnp.concatenate(
        [jnp.sqrt(q_norms[c:c + 1, :] * kmax[:, c * HEAD_DIM:c * HEAD_DIM + 1]) * 1.01 + 1e-6
         for c in range(2)], axis=1)

    def key_blocks():
        off = 0
        for t in range(n):
            for s in range(0, pieces[t], key_block):
                size = min(key_block, pieces[t] - s)
                yield k_refs[t], s, off + s, size
            off += pieces[t]

    def accumulate(shifts):
        blocks = list(key_blocks())
        scores = lambda blk: _dot_t(blk[0][blk[1]:blk[1] + blk[3], :], q_both)
        acc = jnp.zeros((V_HEAD_DIM + ONES_ROWS, 2 * tq), F32)
        st_next = scores(blocks[0])
        for idx, (_, _, pos, size) in enumerate(blocks):
            st = st_next
            if idx + 1 < len(blocks):
                st_next = scores(blocks[idx + 1])
            acc = acc + _dot(vt_ref[:, pos:pos + size], jnp.exp2(st - shifts).astype(BF16))
        return acc

    def finish(acc):
        sums = acc[V_HEAD_DIM:V_HEAD_DIM + 1, :]
        ot = (acc[0:V_HEAD_DIM, 0:tq] * (1.0 / sums[:, 0:tq])
              - acc[0:V_HEAD_DIM, tq:] * (lam / sums[:, tq:]))
        yt = ot * lax.rsqrt(jnp.mean(ot * ot, axis=0, keepdims=True) + EPS)
        o_ref[...] = (yt.T * (sg_ref[0] * (1.0 - lam_init))).astype(BF16)

    acc = accumulate(bounds)
    finish(acc)

    @pl.when(jnp.logical_not(jnp.min(acc[V_HEAD_DIM:V_HEAD_DIM + 1, :]) >= 1e-30))
    def _():
        m = None
        for k_ref, s, _, size in key_blocks():
            bm = jnp.max(_dot_t(k_ref[s:s + size, :], q_both), axis=0, keepdims=True)
            m = bm if m is None else jnp.maximum(m, bm)
        finish(accumulate(m))


def _attn_call(lam_q, lam_k, subln_g, layer, q_arr, q_tile0, kv_list, *, n_batch, q_len,
               lam_init):
    tq = min(512, q_len)
    q_tiles = q_len // tq
    in_specs = [
        pl.BlockSpec((1, 2, HEAD_DIM), lambda b, h, i: (layer, 0, 0)),
        pl.BlockSpec((1, 2, HEAD_DIM), lambda b, h, i: (layer, 0, 0)),
        pl.BlockSpec((1, 1, V_HEAD_DIM), lambda b, h, i: (layer, 0, 0)),
        pl.BlockSpec((tq, LANES), lambda b, h, i: (b * q_tiles + i, q_tile0 + h)),
    ]
    args = [lam_q, lam_k, subln_g.reshape(subln_g.shape[0], 1, V_HEAD_DIM), q_arr]
    for k_arr, k0, v_arr, v0, rows in kv_list:
        in_specs.append(pl.BlockSpec((rows, LANES), lambda b, h, i, k0=k0: (b, k0 + h)))
        in_specs.append(pl.BlockSpec((rows, LANES), lambda b, h, i, v0=v0: (b, v0 + h)))
        args += [k_arr, v_arr]
    pieces = tuple(kv[4] for kv in kv_list)
    return pl.pallas_call(
        functools.partial(_attn_kernel, lam_init=lam_init, pieces=pieces, key_block=256),
        grid=(n_batch, N_HEADS, q_tiles),
        in_specs=in_specs,
        out_specs=pl.BlockSpec((tq, LANES), lambda b, h, i: (b * q_tiles + i, h)),
        out_shape=jax.ShapeDtypeStruct((n_batch * q_len, N_HEADS * V_HEAD_DIM), BF16),
        scratch_shapes=[pltpu.VMEM((V_HEAD_DIM + ONES_ROWS, sum(pieces)), BF16),
                        pltpu.VMEM((1, LANES), F32)],
        compiler_params=_vmem(32 << 20),
        name="attn",
    )(*args)


def _fourier_dense_kernel(x_ref, cc_ref, sc_ref, pos_ref, o_ref, y_ref, *, n_pos):
    @pl.when(pl.program_id(1) == 0)
    def _():
        cc = cc_ref[...]
        sc = sc_ref[...]
        for g in range(FOURIER_GROUPS):
            xg = x_ref[:, g * GROUP_W:(g + 1) * GROUP_W]
            y_ref[0:n_pos, g * GROUP_W:(g + 1) * GROUP_W] = _dot(xg, cc).astype(BF16)
            y_ref[n_pos:2 * n_pos, g * GROUP_W:(g + 1) * GROUP_W] = _dot(xg, sc).astype(BF16)

    o_ref[...] = _dot(pos_ref[...], y_ref[...]).astype(BF16)


def _fourier_dense_call(p_arr, f_tile, chan_cos, chan_sin, pos_mat, *, n_batch, n_pos):
    tm = min(256, n_pos)
    row_tiles = n_pos // tm
    return pl.pallas_call(
        functools.partial(_fourier_dense_kernel, n_pos=n_pos),
        grid=(n_batch, row_tiles),
        in_specs=[
            pl.BlockSpec((n_pos, BRANCH_W), lambda b, i: (b, f_tile),
                         pipeline_mode=pl.Buffered(1)),
            pl.BlockSpec((GROUP_W, GROUP_W), lambda b, i: (0, 0)),
            pl.BlockSpec((GROUP_W, GROUP_W), lambda b, i: (0, 0)),
            pl.BlockSpec((tm, 2 * n_pos), lambda b, i: (i, 0)),
        ],
        out_specs=pl.BlockSpec((tm, BRANCH_W), lambda b, i: (b * row_tiles + i, 0)),
        out_shape=jax.ShapeDtypeStruct((n_batch * n_pos, BRANCH_W), BF16),
        scratch_shapes=[pltpu.VMEM((2 * n_pos, BRANCH_W), BF16)],
        compiler_params=_vmem(32 << 20),
        name="fourier_dense",
    )(p_arr, chan_cos, chan_sin, pos_mat)


def _fourier_latent_kernel(x_ref, wa_ref, wc_ref, twc_ref, tws_ref, cs_ref, o_ref,
                           xs_ref, ar_ref, ai_ref, zr_ref, zi_ref):
    tiles = GROUP_W // LANES
    lane_tile = lambda v, t: v[:, t * LANES:(t + 1) * LANES]
    gather = lambda ref, rows: jnp.concatenate([ref[t, rows, :] for t in range(tiles)], axis=1)

    for t in range(tiles):
        xs_ref[t] = lane_tile(x_ref, t).astype(F32)
    wa = wa_ref[...]
    wc = wc_ref[...]

    def strided(i):
        return pl.ds(i, FFT_R, stride=FFT_R)

    def block_rows(i):
        return pl.ds(pl.multiple_of(i * FFT_R, FFT_R), FFT_R)

    def stage1(b, carry):
        a = _dot(wa, gather(xs_ref, strided(b)).astype(BF16))
        tc = twc_ref[block_rows(b), :]
        ts = tws_ref[block_rows(b), :]
        for t in range(tiles):
            re = lane_tile(a[0:FFT_R], t)
            im = lane_tile(a[FFT_R:], t)
            ar_ref[t, block_rows(b), :] = re * tc + im * ts
            ai_ref[t, block_rows(b), :] = im * tc - re * ts
        return carry

    lax.fori_loop(0, FFT_R, stage1, 0, unroll=8)

    def stage2(c, carry):
        a = jnp.concatenate([gather(ar_ref, strided(c)), gather(ai_ref, strided(c))], axis=0)
        z = _dot(wc, a.astype(BF16))
        for t in range(tiles):
            zr_ref[t, strided(c), :] = lane_tile(z[0:FFT_R], t)
            zi_ref[t, strided(c), :] = lane_tile(z[FFT_R:], t)
        return carry

    lax.fori_loop(0, FFT_R, stage2, 0, unroll=8)

    cs = cs_ref[...]
    block = 512
    for r in range(0, x_ref.shape[0], block):
        rows = slice(r, r + block)
        z = jnp.concatenate([gather(zr_ref, rows), gather(zi_ref, rows)], axis=1)
        o_ref[rows, :] = _dot(z.astype(BF16), cs).astype(BF16)


def _fourier_latent_call(plain, wa, wc, tw_cos, tw_sin, chan_cs, *, n_batch):
    n_pos = FFT_R * FFT_R
    tiles = GROUP_W // LANES
    const = lambda shape: pl.BlockSpec(shape, lambda n, q: (0,) * len(shape))
    scratch = pltpu.VMEM((tiles, n_pos, LANES), F32)
    return pl.pallas_call(
        _fourier_latent_kernel,
        grid=(n_batch, FOURIER_GROUPS),
        in_specs=[
            pl.BlockSpec((n_pos, GROUP_W), lambda n, q: (n, PLAIN_F * FOURIER_GROUPS + q)),
            const((2 * FFT_R, FFT_R)),
            const((2 * FFT_R, 2 * FFT_R)),
            const((n_pos, LANES)),
            const((n_pos, LANES)),
            const((2 * GROUP_W, GROUP_W)),
        ],
        out_specs=pl.BlockSpec((n_pos, GROUP_W), lambda n, q: (n, q)),
        out_shape=jax.ShapeDtypeStruct((n_batch * n_pos, BRANCH_W), BF16),
        scratch_shapes=[scratch] * 5,
        compiler_params=_vmem(48 << 20),
        name="fourier_latent",
    )(plain, wa, wc, tw_cos, tw_sin, chan_cs)


def _merge_kernel(x_ref, mod_ref, attn_ref, cb_ref, cc_ref, cx_ref, ccp_ref, cxp_ref,
                  ccn_ref, cxn_ref, four_ref, g0a_ref, g0b_ref, g1a_ref, g1b_ref, g2a_ref,
                  g2b_ref, wconv_ref, wbr_ref, wo_ref, o_ref, *, seq_len):
    tm = x_ref.shape[0]
    i = pl.program_id(0)
    pos0 = (i * tm) % seq_len

    u = cc_ref[...].astype(F32) * cx_ref[...].astype(F32)
    u_before = (ccp_ref[SUBLANES - 1:SUBLANES, :].astype(F32)
                * cxp_ref[SUBLANES - 1:SUBLANES, :].astype(F32))
    u_after = ccn_ref[0:1, :].astype(F32) * cxn_ref[0:1, :].astype(F32)
    u_before = jnp.where(pos0 == 0, 0.0, u_before)
    u_after = jnp.where(pos0 + tm == seq_len, 0.0, u_after)
    row = lax.broadcasted_iota(jnp.int32, (tm, 1), 0)
    u_prev = jnp.where(row == 0, u_before, pltpu.roll(u, 1, 0))
    u_next = jnp.where(row == tm - 1, u_after, pltpu.roll(u, tm - 1, 0))
    w = wconv_ref[0]
    conv = cb_ref[...].astype(F32) * (u_prev * w[0:1, :] + u * w[1:2, :] + u_next * w[2:3, :])

    def gate(a_ref, b_ref):
        return jnp.concatenate([a_ref[...], b_ref[...]], axis=1).astype(F32)

    mix = (gate(g0a_ref, g0b_ref) * _dot(attn_ref[...], wbr_ref[0, 0])
           + gate(g1a_ref, g1b_ref) * _dot(conv.astype(BF16), wbr_ref[0, 1])
           + gate(g2a_ref, g2b_ref) * _dot(four_ref[...], wbr_ref[0, 2]))
    y = _dot(mix.astype(BF16), wo_ref[0])
    o_ref[...] = x_ref[...] + mod_ref[0, 0, 2:3, :] * y


def _merge_call(x2d, mod, layer, mod_row, attn_o, plain, gates, four_o, w_conv, w_br, w_o, *,
                seq_len):
    m, d = x2d.shape
    tm = 256
    halo_per_tile = tm // SUBLANES
    last_halo = m // SUBLANES - 1
    const = dict(pipeline_mode=pl.Buffered(1))

    def col(tile):
        return pl.BlockSpec((tm, COL_TILE), lambda i: (i, tile))

    def before(tile):
        return pl.BlockSpec((SUBLANES, COL_TILE),
                            lambda i: (jnp.maximum(i * halo_per_tile - 1, 0), tile))

    def after(tile):
        return pl.BlockSpec((SUBLANES, COL_TILE),
                            lambda i: (jnp.minimum((i + 1) * halo_per_tile, last_halo), tile))

    return pl.pallas_call(
        functools.partial(_merge_kernel, seq_len=seq_len),
        grid=(m // tm,),
        in_specs=[
            pl.BlockSpec((tm, d), lambda i: (i, 0)),
            pl.BlockSpec((1, 1, 6, d), lambda i: (layer, mod_row(i * tm), 0, 0)),
            pl.BlockSpec((tm, BRANCH_W), lambda i: (i, 0)),
            col(PLAIN_CB), col(PLAIN_CC), col(PLAIN_CX),
            before(PLAIN_CC), before(PLAIN_CX), after(PLAIN_CC), after(PLAIN_CX),
            pl.BlockSpec((tm, BRANCH_W), lambda i: (i, 0)),
            *[col(t) for t in range(N_GATE_TILES)],
            pl.BlockSpec((1, CONV_K, BRANCH_W), lambda i: (layer, 0, 0)),
            pl.BlockSpec((1, N_BRANCH, BRANCH_W, d), lambda i: (layer, 0, 0, 0), **const),
            pl.BlockSpec((1, d, d), lambda i: (layer, 0, 0), **const),
        ],
        out_specs=pl.BlockSpec((tm, d), lambda i: (i, 0)),
        out_shape=jax.ShapeDtypeStruct((m, d), F32),
        compiler_params=_vmem(56 << 20),
        name="merge",
    )(x2d, mod, attn_o, plain, plain, plain, plain, plain, plain, plain, four_o,
      *([gates] * N_GATE_TILES), w_conv, w_br, w_o)


def _mlp_kernel(x_ref, mod_ref, g_ref, w1_ref, w2_ref, o_ref, h_ref):
    j = pl.program_id(1)

    @pl.when(j == 0)
    def _():
        h = _rms_modulate(x_ref[...], g_ref[0], mod_ref[0, 0, 3:4, :], mod_ref[0, 0, 4:5, :])
        h_ref[...] = h.astype(BF16)
        o_ref[...] = jnp.zeros_like(o_ref)

    hid = jnp.maximum(_dot(h_ref[...], w1_ref[0]), 0.0)
    o_ref[...] += _dot((hid * hid).astype(BF16), w2_ref[0])

    @pl.when(j == pl.num_programs(1) - 1)
    def _():
        o_ref[...] = x_ref[...] + mod_ref[0, 0, 5:6, :] * o_ref[...]


def _mlp_call(x2d, mod, layer, mod_row, ln_g, w1, w2):
    m, d = x2d.shape
    d_ff = w1.shape[2]
    tm, tf = 1024, 512
    return pl.pallas_call(
        _mlp_kernel,
        grid=(m // tm, d_ff // tf),
        in_specs=[
            pl.BlockSpec((tm, d), lambda i, j: (i, 0)),
            pl.BlockSpec((1, 1, 6, d), lambda i, j: (layer, mod_row(i * tm), 0, 0)),
            pl.BlockSpec((1, 1, d), lambda i, j: (layer, 0, 0)),
            pl.BlockSpec((1, d, tf), lambda i, j: (layer, 0, j)),
            pl.BlockSpec((1, tf, d), lambda i, j: (layer, j, 0)),
        ],
        out_specs=pl.BlockSpec((tm, d), lambda i, j: (i, 0)),
        out_shape=jax.ShapeDtypeStruct((m, d), F32),
        scratch_shapes=[pltpu.VMEM((tm, d), BF16)],
        compiler_params=_vmem(56 << 20),
        name="mlp",
    )(x2d, mod, ln_g.reshape(ln_g.shape[0], 1, d), w1, w2)


def _rope_tables(seq_len):
    rows = seq_len // GRID_W
    t_row = jnp.repeat(jnp.arange(rows), GRID_W).astype(F32)
    t_col = jnp.tile(jnp.arange(GRID_W), rows).astype(F32)
    axis_dim = HEAD_DIM // 2
    inv_freq = 1.0 / (ROPE_THETA ** (jnp.arange(0, axis_dim, 2, dtype=F32) / axis_dim))
    ang_r = t_row[:, None] * inv_freq[None, :]
    ang_c = t_col[:, None] * inv_freq[None, :]
    cos = jnp.concatenate([jnp.cos(ang_r)] * 2 + [jnp.cos(ang_c)] * 2, axis=-1)
    sin = jnp.concatenate([-jnp.sin(ang_r), jnp.sin(ang_r), -jnp.sin(ang_c), jnp.sin(ang_c)],
                          axis=-1)
    return jnp.tile(cos, (1, LANES // HEAD_DIM)), jnp.tile(sin, (1, LANES // HEAD_DIM))


def _dft_cos_sin(rows, cols, period, scale):
    r = jnp.arange(rows, dtype=jnp.int32)
    c = jnp.arange(cols, dtype=jnp.int32)
    ang = ((r[:, None] * c[None, :]) % period).astype(F32) * (2.0 * math.pi / period)
    return jnp.cos(ang) * scale, jnp.sin(ang) * scale


def kernel(x, c, ctx, c_ctx, ln1_g, ln2_g, w_ada, b_ada, w_in, qn_g, kn_g, lam_q, lam_k,
           subln_g, w_conv, w_br, w_o, w1, w2):
    n_batch, seq, d = x.shape
    ctx_len = ctx.shape[1]
    depth = w_ada.shape[0]
    assert seq == FFT_R * FFT_R and ctx_len == 256 and n_batch <= CTX_MOD_ROW and d == D_MODEL

    c_rows = jnp.zeros((MOD_ROWS, d), F32).at[:n_batch].set(c).at[CTX_MOD_ROW].set(c_ctx)
    mod = _mod_call(c_rows, w_ada, b_ada).reshape(depth, MOD_ROWS, 6, d)

    cos_l, sin_l = _rope_tables(seq)
    cos_c = jnp.ones((n_batch * ctx_len, LANES), F32)
    sin_c = jnp.zeros((n_batch * ctx_len, LANES), F32)
    qg = jnp.tile(qn_g, (1, MXU_W // HEAD_DIM)).reshape(depth, 1, MXU_W)
    kg = jnp.tile(kn_g, (1, MXU_W // HEAD_DIM)).reshape(depth, 1, MXU_W)

    ch_c, ch_s = _dft_cos_sin(GROUP_W, GROUP_W, GROUP_W, GROUP_W ** -0.5)
    chan_cos, chan_sin = ch_c.astype(BF16), ch_s.astype(BF16)
    chan_cs = jnp.concatenate([chan_cos, chan_sin], axis=0)
    pc_c, pc_s = _dft_cos_sin(ctx_len, ctx_len, ctx_len, ctx_len ** -0.5)
    pos_c = jnp.concatenate([pc_c, -pc_s], axis=1).astype(BF16)
    r_c, r_s = _dft_cos_sin(FFT_R, FFT_R, FFT_R, FFT_R ** -0.5)
    wa = jnp.concatenate([r_c, -r_s], axis=0).astype(BF16)
    wc = jnp.concatenate([jnp.concatenate([r_c, r_s], axis=1),
                          jnp.concatenate([-r_s, r_c], axis=1)], axis=0).astype(BF16)
    tw_c, tw_s = _dft_cos_sin(FFT_R, FFT_R, seq, 1.0)
    tw_cos = jnp.broadcast_to(tw_c.reshape(seq, 1), (seq, LANES))
    tw_sin = jnp.broadcast_to(tw_s.reshape(seq, 1), (seq, LANES))

    w_br_b = w_br.astype(BF16)
    w_o_b = w_o.astype(BF16)
    w1_b = w1.astype(BF16)
    w2_b = w2.astype(BF16)

    lat_row = lambda row: row // seq
    ctx_row = lambda row: CTX_MOD_ROW
    heads = COL_TILE // LANES

    xl = x.reshape(n_batch * seq, d)
    xc = ctx.reshape(n_batch * ctx_len, d)

    for l in range(depth):
        last = l == depth - 1
        lam_init = 0.8 - 0.6 * math.exp(-0.3 * l)

        hc = _normmod_call(xc, mod, l, ctx_row, ln1_g)
        hl = _normmod_call(xl, mod, l, lat_row, ln1_g)
        qk_c = _proj_call("qk", hc, w_in, l, TILE_Q, 2, (qg, kg, cos_c, sin_c))
        qk_l = _proj_call("qk", hl, w_in, l, TILE_Q, 2, (qg, kg, cos_l, sin_l))
        plain_c = _proj_call("plain", hc, w_in, l, TILE_V, 1 if last else TILE_G - TILE_V)
        plain_l = _proj_call("plain", hl, w_in, l, TILE_V, TILE_G - TILE_V)
        gates_l = _proj_call("gate", hl, w_in, l, TILE_G, N_GATE_TILES)

        kv_ctx = (qk_c, heads, plain_c, PLAIN_V * heads, ctx_len)
        kv_lat = (qk_l, heads, plain_l, PLAIN_V * heads, seq)
        attn_l = _attn_call(lam_q, lam_k, subln_g, l, qk_l, 0, [kv_ctx, kv_lat],
                            n_batch=n_batch, q_len=seq, lam_init=lam_init)
        four_l = _fourier_latent_call(plain_l, wa, wc, tw_cos, tw_sin, chan_cs, n_batch=n_batch)
        xl = _merge_call(xl, mod, l, lat_row, attn_l, plain_l, gates_l, four_l, w_conv, w_br_b,
                         w_o_b, seq_len=seq)
        xl = _mlp_call(xl, mod, l, lat_row, ln2_g, w1_b, w2_b)

        if not last:
            gates_c = _proj_call("gate", hc, w_in, l, TILE_G, N_GATE_TILES)
            attn_c = _attn_call(lam_q, lam_k, subln_g, l, qk_c, 0, [kv_ctx],
                                n_batch=n_batch, q_len=ctx_len, lam_init=lam_init)
            four_c = _fourier_dense_call(plain_c, PLAIN_F, chan_cos, chan_sin, pos_c,
                                         n_batch=n_batch, n_pos=ctx_len)
            xc = _merge_call(xc, mod, l, ctx_row, attn_c, plain_c, gates_c, four_c, w_conv,
                             w_br_b, w_o_b, seq_len=ctx_len)
            xc = _mlp_call(xc, mod, l, ctx_row, ln2_g, w1_b, w2_b)

    return xl.reshape(n_batch, seq, d)
```

```python
import functools
import math

import jax
import jax.numpy as jnp
from jax import lax
from jax.experimental import pallas as pl
from jax.experimental.pallas import tpu as pltpu

D_MODEL = 2048
GRID_W = 64
N_HEADS = 8
HEAD_DIM = 64
V_HEAD_DIM = 128
BRANCH_W = 1024
FOURIER_GROUPS = 4
GROUP_W = BRANCH_W // FOURIER_GROUPS
N_BRANCH = 3
CONV_K = 3
EPS = 1e-6
ROPE_THETA = 10000.0

COL_TILE = 1024
TILE_Q, TILE_K, TILE_V, TILE_CB, TILE_CC, TILE_CX, TILE_F, TILE_G = range(8)
QK_TILES = (TILE_Q, TILE_K)
PLAIN_TILES = (TILE_CC, TILE_CX, TILE_CB, TILE_V, TILE_F)
PLAIN_CC, PLAIN_CX, PLAIN_CB, PLAIN_V, PLAIN_F = range(5)
GATE_TILES = tuple(range(TILE_G, TILE_G + N_BRANCH * D_MODEL // COL_TILE))

LANES = 128
SUBLANES = 8
MXU_W = 256
ONES_ROWS = 16
LOG2_E = math.log2(math.e)
MOD_ROWS = 8
CTX_MOD_ROW = 4
FFT_R = 64

F32 = jnp.float32
BF16 = jnp.bfloat16


def _vmem(nbytes):
    return pltpu.CompilerParams(vmem_limit_bytes=nbytes)


def _sigmoid(x):
    return 1.0 / (1.0 + jnp.exp(-x))


def _rms_modulate(x, g, shift, scale):
    y = x * lax.rsqrt(jnp.mean(x * x, axis=-1, keepdims=True) + EPS) * g
    return y * (1.0 + scale) + shift


def _dot(a, b):
    return jnp.dot(a, b, preferred_element_type=F32)


def _dot_t(a, b):
    return lax.dot_general(a, b, (((1,), (1,)), ((), ())), preferred_element_type=F32)


def _mod_kernel(c_ref, w_ref, b_ref, o_ref):
    c = c_ref[...]
    cs = c * _sigmoid(c)
    o_ref[0] = jnp.dot(cs, w_ref[0], preferred_element_type=F32,
                       precision=lax.Precision.HIGHEST) + b_ref[0]


def _mod_call(c_rows, w_ada, b_ada):
    depth, d, n = w_ada.shape
    tn = 1536
    return pl.pallas_call(
        _mod_kernel,
        grid=(depth, n // tn),
        in_specs=[
            pl.BlockSpec((MOD_ROWS, d), lambda l, j: (0, 0)),
            pl.BlockSpec((1, d, tn), lambda l, j: (l, 0, j)),
            pl.BlockSpec((1, 1, tn), lambda l, j: (l, 0, j)),
        ],
        out_specs=pl.BlockSpec((1, MOD_ROWS, tn), lambda l, j: (l, 0, j)),
        out_shape=jax.ShapeDtypeStruct((depth, MOD_ROWS, n), F32),
        compiler_params=_vmem(40 << 20),
        name="mod",
    )(c_rows, w_ada, b_ada.reshape(depth, 1, n))


def _normmod_kernel(x_ref, mod_ref, g_ref, o_ref):
    h = _rms_modulate(x_ref[...], g_ref[0], mod_ref[0, 0, 0:1, :], mod_ref[0, 0, 1:2, :])
    o_ref[...] = h.astype(BF16)


def _normmod_call(x2d, mod, layer, mod_row, ln_g):
    m, d = x2d.shape
    tm = 512
    return pl.pallas_call(
        _normmod_kernel,
        grid=(m // tm,),
        in_specs=[
            pl.BlockSpec((tm, d), lambda i: (i, 0)),
            pl.BlockSpec((1, 1, 6, d), lambda i: (layer, mod_row(i * tm), 0, 0)),
            pl.BlockSpec((1, 1, d), lambda i: (layer, 0, 0)),
        ],
        out_specs=pl.BlockSpec((tm, d), lambda i: (i, 0)),
        out_shape=jax.ShapeDtypeStruct((m, d), BF16),
        compiler_params=_vmem(32 << 20),
        name="normmod",
    )(x2d, mod, ln_g.reshape(ln_g.shape[0], 1, d))


def _cast_weights(w_ref, wb_ref):
    @pl.when(pl.program_id(1) == 0)
    def _():
        wb_ref[...] = w_ref[0].astype(BF16)


def _proj_plain_kernel(h_ref, w_ref, o_ref, wb_ref):
    _cast_weights(w_ref, wb_ref)
    o_ref[...] = _dot(h_ref[...], wb_ref[...]).astype(BF16)


def _proj_gate_kernel(h_ref, w_ref, o_ref, wb_ref):
    _cast_weights(w_ref, wb_ref)
    acc = _dot(h_ref[...], wb_ref[...])
    o_ref[...] = (0.5 * jnp.tanh(0.5 * acc) + 0.5).astype(BF16)


def _proj_qk_kernel(h_ref, w_ref, qg_ref, kg_ref, cos_ref, sin_ref, o_ref, wb_ref):
    _cast_weights(w_ref, wb_ref)
    acc = _dot(h_ref[...], wb_ref[...])
    is_q = pl.program_id(0) == TILE_Q
    gain = jnp.where(is_q, qg_ref[0] * (LOG2_E * HEAD_DIM ** -0.5), kg_ref[0])
    cos = cos_ref[...]
    sin = sin_ref[...]
    lane = lax.broadcasted_iota(jnp.int32, (1, LANES), 1)
    first_half = (lane % (HEAD_DIM // 2)) < (HEAD_DIM // 4)
    gsum = (lax.broadcasted_iota(jnp.int32, (MXU_W, MXU_W), 0) // HEAD_DIM
            == lax.broadcasted_iota(jnp.int32, (MXU_W, MXU_W), 1) // HEAD_DIM).astype(BF16)
    for t in range(COL_TILE // MXU_W):
        a = acc[:, t * MXU_W:(t + 1) * MXU_W]
        ssq = _dot((a * a).astype(BF16), gsum)
        y = a * lax.rsqrt(ssq * (1.0 / HEAD_DIM) + EPS) * gain
        for u in range(MXU_W // LANES):
            yh = y[:, u * LANES:(u + 1) * LANES]
            partner = jnp.where(first_half,
                                pltpu.roll(yh, LANES - HEAD_DIM // 4, 1),
                                pltpu.roll(yh, HEAD_DIM // 4, 1))
            c0 = t * MXU_W + u * LANES
            o_ref[:, c0:c0 + LANES] = (yh * cos + partner * sin).astype(BF16)


def _proj_call(kind, h, w_in, layer, w_tiles, extra=()):
    m, d = h.shape
    tm = 1024
    n_tiles = len(w_tiles)

    def w_tile(j):
        tile = w_tiles[0]
        for idx in range(1, n_tiles):
            tile = jnp.where(j == idx, w_tiles[idx], tile)
        return tile

    in_specs = [
        pl.BlockSpec((tm, d), lambda j, i: (i, 0)),
        pl.BlockSpec((1, d, COL_TILE), lambda j, i: (layer, 0, w_tile(j))),
    ]
    args = [h, w_in]
    if kind == "qk":
        qg, kg, cos_t, sin_t = extra
        seq_tiles = cos_t.shape[0] // tm
        in_specs += [
            pl.BlockSpec((1, 1, MXU_W), lambda j, i: (layer, 0, 0)),
            pl.BlockSpec((1, 1, MXU_W), lambda j, i: (layer, 0, 0)),
            pl.BlockSpec((tm, LANES), lambda j, i: (i % seq_tiles, 0)),
            pl.BlockSpec((tm, LANES), lambda j, i: (i % seq_tiles, 0)),
        ]
        args += [qg, kg, cos_t, sin_t]
    body = {"qk": _proj_qk_kernel, "plain": _proj_plain_kernel, "gate": _proj_gate_kernel}[kind]
    return pl.pallas_call(
        body,
        grid=(n_tiles, m // tm),
        in_specs=in_specs,
        out_specs=pl.BlockSpec((tm, COL_TILE), lambda j, i: (i, j)),
        out_shape=jax.ShapeDtypeStruct((m, n_tiles * COL_TILE), BF16),
        scratch_shapes=[pltpu.VMEM((d, COL_TILE), BF16)],
        compiler_params=_vmem(52 << 20),
        name="proj_" + kind,
    )(*args)


def _attn_kernel(lq_ref, lk_ref, sg_ref, q_ref, *refs, lam_init, pieces, key_block):
    n = len(pieces)
    k_refs, v_refs = refs[0:2 * n:2], refs[1:2 * n:2]
    o_ref, vt_ref, kmax_ref = refs[2 * n:]
    tq = q_ref.shape[0]
    lane = lax.broadcasted_iota(jnp.int32, (1, LANES), 1)

    @pl.when(pl.program_id(2) == 0)
    def _():
        group = lax.broadcasted_iota(jnp.int32, (LANES, LANES), 0) // HEAD_DIM
        gsum = (group == lane // HEAD_DIM).astype(BF16)
        kmax = None
        off = 0
        for t in range(n):
            rows = pieces[t]
            vt_ref[0:V_HEAD_DIM, off:off + rows] = v_refs[t][...].astype(F32).T.astype(BF16)
            kf = k_refs[t][...].astype(F32)
            norms = _dot((kf * kf).astype(BF16), gsum)
            piece_max = jnp.max(norms, axis=0, keepdims=True)
            kmax = piece_max if kmax is None else jnp.maximum(kmax, piece_max)
            off += rows
        vt_ref[V_HEAD_DIM:, :] = jnp.ones((ONES_ROWS, off), BF16)
        kmax_ref[...] = kmax

    e = jnp.exp(jnp.sum(lq_ref[0] * lk_ref[0], axis=-1, keepdims=True))
    lam = e[0:1, :] - e[1:2, :] + lam_init

    q = q_ref[...]
    zero = jnp.zeros_like(q)
    lo = lane < HEAD_DIM
    q_both = jnp.concatenate([jnp.where(lo, q, zero), jnp.where(lo, zero, q)], axis=0)

    qf = q.astype(F32)
    sel = (lax.broadcasted_iota(jnp.int32, (SUBLANES, LANES), 0) == lane // HEAD_DIM).astype(BF16)
    q_norms = _dot_t(sel, (qf * qf).astype(BF16))
    kmax = kmax_ref[...]
    bounds = jnp.concatenate(
        [jnp.sqrt(q_norms[c:c + 1, :] * kmax[:, c * HEAD_DIM:c * HEAD_DIM + 1]) * 1.01 + 1e-6
         for c in range(2)], axis=1)

    def key_blocks():
        off = 0
        for t in range(n):
            for s in range(0, pieces[t], key_block):
                size = min(key_block, pieces[t] - s)
                yield k_refs[t], s, off + s, size
            off += pieces[t]

    def accumulate(shifts):
        blocks = list(key_blocks())
        scores = lambda blk: _dot_t(blk[0][blk[1]:blk[1] + blk[3], :], q_both)
        acc = jnp.zeros((V_HEAD_DIM + ONES_ROWS, 2 * tq), F32)
        st_next = scores(blocks[0])
        for idx, (_, _, pos, size) in enumerate(blocks):
            st = st_next
            if idx + 1 < len(blocks):
                st_next = scores(blocks[idx + 1])
            acc = acc + _dot(vt_ref[:, pos:pos + size], jnp.exp2(st - shifts).astype(BF16))
        return acc

    def finish(acc):
        sums = acc[V_HEAD_DIM:V_HEAD_DIM + 1, :]
        ot = (acc[0:V_HEAD_DIM, 0:tq] * (1.0 / sums[:, 0:tq])
              - acc[0:V_HEAD_DIM, tq:] * (lam / sums[:, tq:]))
        yt = ot * lax.rsqrt(jnp.mean(ot * ot, axis=0, keepdims=True) + EPS)
        o_ref[...] = (yt.T * (sg_ref[0] * (1.0 - lam_init))).astype(BF16)

    acc = accumulate(bounds)
    finish(acc)

    @pl.when(jnp.logical_not(jnp.min(acc[V_HEAD_DIM:V_HEAD_DIM + 1, :]) >= 1e-30))
    def _():
        m = None
        for k_ref, s, _, size in key_blocks():
            bm = jnp.max(_dot_t(k_ref[s:s + size, :], q_both), axis=0, keepdims=True)
            m = bm if m is None else jnp.maximum(m, bm)
        finish(accumulate(m))


def _attn_call(lam_q, lam_k, subln_g, layer, q_arr, q_tile0, kv_list, *, n_batch, q_len,
               lam_init):
    tq = min(512, q_len)
    q_tiles = q_len // tq
    in_specs = [
        pl.BlockSpec((1, 2, HEAD_DIM), lambda b, h, i: (layer, 0, 0)),
        pl.BlockSpec((1, 2, HEAD_DIM), lambda b, h, i: (layer, 0, 0)),
        pl.BlockSpec((1, 1, V_HEAD_DIM), lambda b, h, i: (layer, 0, 0)),
        pl.BlockSpec((tq, LANES), lambda b, h, i: (b * q_tiles + i, q_tile0 + h)),
    ]
    args = [lam_q, lam_k, subln_g.reshape(subln_g.shape[0], 1, V_HEAD_DIM), q_arr]
    for k_arr, k0, v_arr, v0, rows in kv_list:
        in_specs.append(pl.BlockSpec((rows, LANES), lambda b, h, i, k0=k0: (b, k0 + h)))
        in_specs.append(pl.BlockSpec((rows, LANES), lambda b, h, i, v0=v0: (b, v0 + h)))
        args += [k_arr, v_arr]
    pieces = tuple(kv[4] for kv in kv_list)
    return pl.pallas_call(
        functools.partial(_attn_kernel, lam_init=lam_init, pieces=pieces, key_block=256),
        grid=(n_batch, N_HEADS, q_tiles),
        in_specs=in_specs,
        out_specs=pl.BlockSpec((tq, LANES), lambda b, h, i: (b * q_tiles + i, h)),
        out_shape=jax.ShapeDtypeStruct((n_batch * q_len, N_HEADS * V_HEAD_DIM), BF16),
        scratch_shapes=[pltpu.VMEM((V_HEAD_DIM + ONES_ROWS, sum(pieces)), BF16),
                        pltpu.VMEM((1, LANES), F32)],
        compiler_params=_vmem(32 << 20),
        name="attn",
    )(*args)


def _fourier_dense_kernel(x_ref, cc_ref, sc_ref, pos_ref, o_ref, y_ref, *, n_pos):
    @pl.when(pl.program_id(1) == 0)
    def _():
        cc = cc_ref[...]
        sc = sc_ref[...]
        for g in range(FOURIER_GROUPS):
            xg = x_ref[:, g * GROUP_W:(g + 1) * GROUP_W]
            y_ref[0:n_pos, g * GROUP_W:(g + 1) * GROUP_W] = _dot(xg, cc).astype(BF16)
            y_ref[n_pos:2 * n_pos, g * GROUP_W:(g + 1) * GROUP_W] = _dot(xg, sc).astype(BF16)

    o_ref[...] = _dot(pos_ref[...], y_ref[...]).astype(BF16)


def _fourier_dense_call(p_arr, f_tile, chan_cos, chan_sin, pos_mat, *, n_batch, n_pos):
    tm = min(256, n_pos)
    row_tiles = n_pos // tm
    return pl.pallas_call(
        functools.partial(_fourier_dense_kernel, n_pos=n_pos),
        grid=(n_batch, row_tiles),
        in_specs=[
            pl.BlockSpec((n_pos, BRANCH_W), lambda b, i: (b, f_tile),
                         pipeline_mode=pl.Buffered(1)),
            pl.BlockSpec((GROUP_W, GROUP_W), lambda b, i: (0, 0)),
            pl.BlockSpec((GROUP_W, GROUP_W), lambda b, i: (0, 0)),
            pl.BlockSpec((tm, 2 * n_pos), lambda b, i: (i, 0)),
        ],
        out_specs=pl.BlockSpec((tm, BRANCH_W), lambda b, i: (b * row_tiles + i, 0)),
        out_shape=jax.ShapeDtypeStruct((n_batch * n_pos, BRANCH_W), BF16),
        scratch_shapes=[pltpu.VMEM((2 * n_pos, BRANCH_W), BF16)],
        compiler_params=_vmem(32 << 20),
        name="fourier_dense",
    )(p_arr, chan_cos, chan_sin, pos_mat)


def _fourier_latent_kernel(x_ref, wa_ref, wc_ref, twc_ref, tws_ref, cs_ref, o_ref,
                           xs_ref, ar_ref, ai_ref, zr_ref, zi_ref):
    tiles = GROUP_W // LANES
    lane_tile = lambda v, t: v[:, t * LANES:(t + 1) * LANES]
    gather = lambda ref, rows: jnp.concatenate([ref[t, rows, :] for t in range(tiles)], axis=1)

    for t in range(tiles):
        xs_ref[t] = lane_tile(x_ref, t).astype(F32)
    wa = wa_ref[...]
    wc = wc_ref[...]

    def strided(i):
        return pl.ds(i, FFT_R, stride=FFT_R)

    def block_rows(i):
        return pl.ds(pl.multiple_of(i * FFT_R, FFT_R), FFT_R)

    def stage1(b, carry):
        a = _dot(wa, gather(xs_ref, strided(b)).astype(BF16))
        tc = twc_ref[block_rows(b), :]
        ts = tws_ref[block_rows(b), :]
        for t in range(tiles):
            re = lane_tile(a[0:FFT_R], t)
            im = lane_tile(a[FFT_R:], t)
            ar_ref[t, block_rows(b), :] = re * tc + im * ts
            ai_ref[t, block_rows(b), :] = im * tc - re * ts
        return carry

    lax.fori_loop(0, FFT_R, stage1, 0, unroll=8)

    def stage2(c, carry):
        a = jnp.concatenate([gather(ar_ref, strided(c)), gather(ai_ref, strided(c))], axis=0)
        z = _dot(wc, a.astype(BF16))
        for t in range(tiles):
            zr_ref[t, strided(c), :] = lane_tile(z[0:FFT_R], t)
            zi_ref[t, strided(c), :] = lane_tile(z[FFT_R:], t)
        return carry

    lax.fori_loop(0, FFT_R, stage2, 0, unroll=8)

    cs = cs_ref[...]
    block = 512
    for r in range(0, x_ref.shape[0], block):
        rows = slice(r, r + block)
        z = jnp.concatenate([gather(zr_ref, rows), gather(zi_ref, rows)], axis=1)
        o_ref[rows, :] = _dot(z.astype(BF16), cs).astype(BF16)


def _fourier_latent_call(plain, wa, wc, tw_cos, tw_sin, chan_cs, *, n_batch):
    n_pos = FFT_R * FFT_R
    tiles = GROUP_W // LANES
    const = lambda shape: pl.BlockSpec(shape, lambda n, q: (0,) * len(shape))
    scratch = pltpu.VMEM((tiles, n_pos, LANES), F32)
    return pl.pallas_call(
        _fourier_latent_kernel,
        grid=(n_batch, FOURIER_GROUPS),
        in_specs=[
            pl.BlockSpec((n_pos, GROUP_W), lambda n, q: (n, PLAIN_F * FOURIER_GROUPS + q)),
            const((2 * FFT_R, FFT_R)),
            const((2 * FFT_R, 2 * FFT_R)),
            const((n_pos, LANES)),
            const((n_pos, LANES)),
            const((2 * GROUP_W, GROUP_W)),
        ],
        out_specs=pl.BlockSpec((n_pos, GROUP_W), lambda n, q: (n, q)),
        out_shape=jax.ShapeDtypeStruct((n_batch * n_pos, BRANCH_W), BF16),
        scratch_shapes=[scratch] * 5,
        compiler_params=_vmem(48 << 20),
        name="fourier_latent",
    )(plain, wa, wc, tw_cos, tw_sin, chan_cs)


def _merge_kernel(x_ref, mod_ref, attn_ref, conv_ref, before_ref, after_ref, four_ref, gate_ref,
                  wconv_ref, wbr_ref, wo_ref, ln2_ref, o_ref, h2_ref, *, seq_len):
    tm, d = x_ref.shape
    i = pl.program_id(0)
    pos0 = (i * tm) % seq_len
    cc_cols = slice(PLAIN_CC * BRANCH_W, (PLAIN_CC + 1) * BRANCH_W)
    cx_cols = slice(PLAIN_CX * BRANCH_W, (PLAIN_CX + 1) * BRANCH_W)
    cb_cols = slice(PLAIN_CB * BRANCH_W, (PLAIN_CB + 1) * BRANCH_W)

    u = conv_ref[:, cc_cols].astype(F32) * conv_ref[:, cx_cols].astype(F32)
    last = slice(SUBLANES - 1, SUBLANES)
    u_before = before_ref[last, cc_cols].astype(F32) * before_ref[last, cx_cols].astype(F32)
    u_after = after_ref[0:1, cc_cols].astype(F32) * after_ref[0:1, cx_cols].astype(F32)
    u_before = jnp.where(pos0 == 0, 0.0, u_before)
    u_after = jnp.where(pos0 + tm == seq_len, 0.0, u_after)
    row = lax.broadcasted_iota(jnp.int32, (tm, 1), 0)
    u_prev = jnp.where(row == 0, u_before, pltpu.roll(u, 1, 0))
    u_next = jnp.where(row == tm - 1, u_after, pltpu.roll(u, tm - 1, 0))
    w = wconv_ref[0]
    conv = (conv_ref[:, cb_cols].astype(F32)
            * (u_prev * w[0:1, :] + u * w[1:2, :] + u_next * w[2:3, :]))

    gate = lambda branch: gate_ref[:, branch * d:(branch + 1) * d].astype(F32)
    mix = (gate(0) * _dot(attn_ref[...], wbr_ref[0, 0])
           + gate(1) * _dot(conv.astype(BF16), wbr_ref[0, 1])
           + gate(2) * _dot(four_ref[...], wbr_ref[0, 2]))
    y = _dot(mix.astype(BF16), wo_ref[0])
    x_new = x_ref[...] + mod_ref[0, 0, 2:3, :] * y
    o_ref[...] = x_new
    h2 = _rms_modulate(x_new, ln2_ref[0], mod_ref[0, 0, 3:4, :], mod_ref[0, 0, 4:5, :])
    h2_ref[...] = h2.astype(BF16)


def _merge_call(x2d, mod, layer, mod_row, attn_o, plain, gates, four_o, w_conv, w_br, w_o,
                ln2_g, *, seq_len):
    m, d = x2d.shape
    tm = 256
    halo_per_tile = tm // SUBLANES
    last_halo = m // SUBLANES - 1
    const = dict(pipeline_mode=pl.Buffered(1))
    assert (PLAIN_CC, PLAIN_CX, PLAIN_CB) == (0, 1, 2)

    return pl.pallas_call(
        functools.partial(_merge_kernel, seq_len=seq_len),
        grid=(m // tm,),
        in_specs=[
            pl.BlockSpec((tm, d), lambda i: (i, 0)),
            pl.BlockSpec((1, 1, 6, d), lambda i: (layer, mod_row(i * tm), 0, 0)),
            pl.BlockSpec((tm, BRANCH_W), lambda i: (i, 0)),
            pl.BlockSpec((tm, 3 * BRANCH_W), lambda i: (i, 0)),
            pl.BlockSpec((SUBLANES, 2 * BRANCH_W),
                         lambda i: (jnp.maximum(i * halo_per_tile - 1, 0), 0)),
            pl.BlockSpec((SUBLANES, 2 * BRANCH_W),
                         lambda i: (jnp.minimum((i + 1) * halo_per_tile, last_halo), 0)),
            pl.BlockSpec((tm, BRANCH_W), lambda i: (i, 0)),
            pl.BlockSpec((tm, N_BRANCH * d), lambda i: (i, 0)),
            pl.BlockSpec((1, CONV_K, BRANCH_W), lambda i: (layer, 0, 0)),
            pl.BlockSpec((1, N_BRANCH, BRANCH_W, d), lambda i: (layer, 0, 0, 0), **const),
            pl.BlockSpec((1, d, d), lambda i: (layer, 0, 0), **const),
            pl.BlockSpec((1, 1, d), lambda i: (layer, 0, 0)),
        ],
        out_specs=[pl.BlockSpec((tm, d), lambda i: (i, 0))] * 2,
        out_shape=[jax.ShapeDtypeStruct((m, d), F32), jax.ShapeDtypeStruct((m, d), BF16)],
        compiler_params=_vmem(56 << 20),
        name="merge",
    )(x2d, mod, attn_o, plain, plain, plain, four_o, gates, w_conv, w_br, w_o,
      ln2_g.reshape(ln2_g.shape[0], 1, d))


def _mlp_kernel(x_ref, h_ref, mod_ref, w1_ref, w2_ref, o_ref, *, out_chunk):
    j = pl.program_id(1)

    @pl.when(j == 0)
    def _():
        o_ref[...] = jnp.zeros_like(o_ref)

    hid = jnp.maximum(_dot(h_ref[...], w1_ref[0]), 0.0)
    hid2 = (hid * hid).astype(BF16)
    for n0 in range(0, o_ref.shape[1], out_chunk):
        cols = slice(n0, n0 + out_chunk)
        o_ref[:, cols] += mod_ref[0, 0, 5:6, cols] * _dot(hid2, w2_ref[0, :, cols])

    @pl.when(j == pl.num_programs(1) - 1)
    def _():
        o_ref[...] += x_ref[...]


def _mlp_call(x2d, h2, mod, layer, mod_row, w1, w2):
    m, d = x2d.shape
    d_ff = w1.shape[2]
    tm, tf = 1024, 512
    steps = d_ff // tf
    x_rows = lambda i, j: (jnp.where(j >= steps // 2, i, jnp.maximum(i - 1, 0)), 0)
    return pl.pallas_call(
        functools.partial(_mlp_kernel, out_chunk=512),
        grid=(m // tm, steps),
        in_specs=[
            pl.BlockSpec((tm, d), x_rows),
            pl.BlockSpec((tm, d), lambda i, j: (i, 0)),
            pl.BlockSpec((1, 1, 6, d), lambda i, j: (layer, mod_row(i * tm), 0, 0)),
            pl.BlockSpec((1, d, tf), lambda i, j: (layer, 0, j)),
            pl.BlockSpec((1, tf, d), lambda i, j: (layer, j, 0)),
        ],
        out_specs=pl.BlockSpec((tm, d), lambda i, j: (i, 0)),
        out_shape=jax.ShapeDtypeStruct((m, d), F32),
        compiler_params=_vmem(56 << 20),
        name="mlp",
    )(x2d, h2, mod, w1, w2)


def _rope_tables(seq_len):
    rows = seq_len // GRID_W
    t_row = jnp.repeat(jnp.arange(rows), GRID_W).astype(F32)
    t_col = jnp.tile(jnp.arange(GRID_W), rows).astype(F32)
    axis_dim = HEAD_DIM // 2
    inv_freq = 1.0 / (ROPE_THETA ** (jnp.arange(0, axis_dim, 2, dtype=F32) / axis_dim))
    ang_r = t_row[:, None] * inv_freq[None, :]
    ang_c = t_col[:, None] * inv_freq[None, :]
    cos = jnp.concatenate([jnp.cos(ang_r)] * 2 + [jnp.cos(ang_c)] * 2, axis=-1)
    sin = jnp.concatenate([-jnp.sin(ang_r), jnp.sin(ang_r), -jnp.sin(ang_c), jnp.sin(ang_c)],
                          axis=-1)
    return jnp.tile(cos, (1, LANES // HEAD_DIM)), jnp.tile(sin, (1, LANES // HEAD_DIM))


def _dft_cos_sin(rows, cols, period, scale):
    r = jnp.arange(rows, dtype=jnp.int32)
    c = jnp.arange(cols, dtype=jnp.int32)
    ang = ((r[:, None] * c[None, :]) % period).astype(F32) * (2.0 * math.pi / period)
    return jnp.cos(ang) * scale, jnp.sin(ang) * scale


def kernel(x, c, ctx, c_ctx, ln1_g, ln2_g, w_ada, b_ada, w_in, qn_g, kn_g, lam_q, lam_k,
           subln_g, w_conv, w_br, w_o, w1, w2):
    n_batch, seq, d = x.shape
    ctx_len = ctx.shape[1]
    depth = w_ada.shape[0]
    assert seq == FFT_R * FFT_R and ctx_len == 256 and n_batch <= CTX_MOD_ROW and d == D_MODEL

    c_rows = jnp.zeros((MOD_ROWS, d), F32).at[:n_batch].set(c).at[CTX_MOD_ROW].set(c_ctx)
    mod = _mod_call(c_rows, w_ada, b_ada).reshape(depth, MOD_ROWS, 6, d)

    cos_l, sin_l = _rope_tables(seq)
    cos_c = jnp.ones((n_batch * ctx_len, LANES), F32)
    sin_c = jnp.zeros((n_batch * ctx_len, LANES), F32)
    qg = jnp.tile(qn_g, (1, MXU_W // HEAD_DIM)).reshape(depth, 1, MXU_W)
    kg = jnp.tile(kn_g, (1, MXU_W // HEAD_DIM)).reshape(depth, 1, MXU_W)

    ch_c, ch_s = _dft_cos_sin(GROUP_W, GROUP_W, GROUP_W, GROUP_W ** -0.5)
    chan_cos, chan_sin = ch_c.astype(BF16), ch_s.astype(BF16)
    chan_cs = jnp.concatenate([chan_cos, chan_sin], axis=0)
    pc_c, pc_s = _dft_cos_sin(ctx_len, ctx_len, ctx_len, ctx_len ** -0.5)
    pos_c = jnp.concatenate([pc_c, -pc_s], axis=1).astype(BF16)
    r_c, r_s = _dft_cos_sin(FFT_R, FFT_R, FFT_R, FFT_R ** -0.5)
    wa = jnp.concatenate([r_c, -r_s], axis=0).astype(BF16)
    wc = jnp.concatenate([jnp.concatenate([r_c, r_s], axis=1),
                          jnp.concatenate([-r_s, r_c], axis=1)], axis=0).astype(BF16)
    tw_c, tw_s = _dft_cos_sin(FFT_R, FFT_R, seq, 1.0)
    tw_cos = jnp.broadcast_to(tw_c.reshape(seq, 1), (seq, LANES))
    tw_sin = jnp.broadcast_to(tw_s.reshape(seq, 1), (seq, LANES))

    w_br_b = w_br.astype(BF16)
    w_o_b = w_o.astype(BF16)
    w1_b = w1.astype(BF16)
    w2_b = w2.astype(BF16)

    lat_row = lambda row: row // seq
    ctx_row = lambda row: CTX_MOD_ROW
    heads = COL_TILE // LANES

    xl = x.reshape(n_batch * seq, d)
    xc = ctx.reshape(n_batch * ctx_len, d)

    for l in range(depth):
        last = l == depth - 1
        lam_init = 0.8 - 0.6 * math.exp(-0.3 * l)

        hc = _normmod_call(xc, mod, l, ctx_row, ln1_g)
        hl = _normmod_call(xl, mod, l, lat_row, ln1_g)
        qk_c = _proj_call("qk", hc, w_in, l, QK_TILES, (qg, kg, cos_c, sin_c))
        qk_l = _proj_call("qk", hl, w_in, l, QK_TILES, (qg, kg, cos_l, sin_l))
        plain_c = _proj_call("plain", hc, w_in, l, (TILE_V,) if last else PLAIN_TILES)
        plain_l = _proj_call("plain", hl, w_in, l, PLAIN_TILES)
        gates_l = _proj_call("gate", hl, w_in, l, GATE_TILES)

        kv_ctx = (qk_c, heads, plain_c, (0 if last else PLAIN_V) * heads, ctx_len)
        kv_lat = (qk_l, heads, plain_l, PLAIN_V * heads, seq)
        attn_l = _attn_call(lam_q, lam_k, subln_g, l, qk_l, 0, [kv_ctx, kv_lat],
                            n_batch=n_batch, q_len=seq, lam_init=lam_init)
        four_l = _fourier_latent_call(plain_l, wa, wc, tw_cos, tw_sin, chan_cs, n_batch=n_batch)
        xl, h2l = _merge_call(xl, mod, l, lat_row, attn_l, plain_l, gates_l, four_l, w_conv,
                              w_br_b, w_o_b, ln2_g, seq_len=seq)
        xl = _mlp_call(xl, h2l, mod, l, lat_row, w1_b, w2_b)

        if not last:
            gates_c = _proj_call("gate", hc, w_in, l, GATE_TILES)
            attn_c = _attn_call(lam_q, lam_k, subln_g, l, qk_c, 0, [kv_ctx],
                                n_batch=n_batch, q_len=ctx_len, lam_init=lam_init)
            four_c = _fourier_dense_call(plain_c, PLAIN_F, chan_cos, chan_sin, pos_c,
                                         n_batch=n_batch, n_pos=ctx_len)
            xc, h2c = _merge_call(xc, mod, l, ctx_row, attn_c, plain_c, gates_c, four_c, w_conv,
                                  w_br_b, w_o_b, ln2_g, seq_len=ctx_len)
            xc = _mlp_call(xc, h2c, mod, l, ctx_row, w1_b, w2_b)

    return xl.reshape(n_batch, seq, d)
```

```python
import functools
import math

import jax
import jax.numpy as jnp
from jax import lax
from jax.experimental import pallas as pl
from jax.experimental.pallas import tpu as pltpu

D_MODEL = 2048
GRID_W = 64
N_HEADS = 8
HEAD_DIM = 64
V_HEAD_DIM = 128
BRANCH_W = 1024
FOURIER_GROUPS = 4
GROUP_W = BRANCH_W // FOURIER_GROUPS
N_BRANCH = 3
CONV_K = 3
EPS = 1e-6
ROPE_THETA = 10000.0

COL_TILE = 1024
TILE_Q, TILE_K, TILE_V, TILE_CB, TILE_CC, TILE_CX, TILE_F, TILE_G = range(8)
QK_TILES = (TILE_Q, TILE_K)
PLAIN_TILES = (TILE_CC, TILE_CX, TILE_CB, TILE_V, TILE_F)
PLAIN_CC, PLAIN_CX, PLAIN_CB, PLAIN_V, PLAIN_F = range(5)
GATE_TILES = tuple(range(TILE_G, TILE_G + N_BRANCH * D_MODEL // COL_TILE))

LANES = 128
SUBLANES = 8
MXU_W = 256
ONES_ROWS = 16
LOG2_E = math.log2(math.e)
MOD_ROWS = 8
CTX_MOD_ROW = 4
FFT_R = 64

F32 = jnp.float32
BF16 = jnp.bfloat16


def _vmem(nbytes):
    return pltpu.CompilerParams(vmem_limit_bytes=nbytes)


def _sigmoid(x):
    return 1.0 / (1.0 + jnp.exp(-x))


def _rms_modulate(x, g, shift, scale):
    y = x * lax.rsqrt(jnp.mean(x * x, axis=-1, keepdims=True) + EPS) * g
    return y * (1.0 + scale) + shift


def _dot(a, b):
    return jnp.dot(a, b, preferred_element_type=F32)


def _dot_t(a, b):
    return lax.dot_general(a, b, (((1,), (1,)), ((), ())), preferred_element_type=F32)


def _mod_kernel(c_ref, w_ref, b_ref, o_ref):
    c = c_ref[...]
    cs = c * _sigmoid(c)
    o_ref[0] = jnp.dot(cs, w_ref[0], preferred_element_type=F32,
                       precision=lax.Precision.HIGHEST) + b_ref[0]


def _mod_call(c_rows, w_ada, b_ada):
    depth, d, n = w_ada.shape
    tn = 1536
    return pl.pallas_call(
        _mod_kernel,
        grid=(depth, n // tn),
        in_specs=[
            pl.BlockSpec((MOD_ROWS, d), lambda l, j: (0, 0)),
            pl.BlockSpec((1, d, tn), lambda l, j: (l, 0, j)),
            pl.BlockSpec((1, 1, tn), lambda l, j: (l, 0, j)),
        ],
        out_specs=pl.BlockSpec((1, MOD_ROWS, tn), lambda l, j: (l, 0, j)),
        out_shape=jax.ShapeDtypeStruct((depth, MOD_ROWS, n), F32),
        compiler_params=_vmem(40 << 20),
        name="mod",
    )(c_rows, w_ada, b_ada.reshape(depth, 1, n))


def _normmod_kernel(x_ref, mod_ref, g_ref, o_ref):
    h = _rms_modulate(x_ref[...], g_ref[0], mod_ref[0, 0, 0:1, :], mod_ref[0, 0, 1:2, :])
    o_ref[...] = h.astype(BF16)


def _normmod_call(x2d, mod, layer, mod_row, ln_g):
    m, d = x2d.shape
    tm = 512
    return pl.pallas_call(
        _normmod_kernel,
        grid=(m // tm,),
        in_specs=[
            pl.BlockSpec((tm, d), lambda i: (i, 0)),
            pl.BlockSpec((1, 1, 6, d), lambda i: (layer, mod_row(i * tm), 0, 0)),
            pl.BlockSpec((1, 1, d), lambda i: (layer, 0, 0)),
        ],
        out_specs=pl.BlockSpec((tm, d), lambda i: (i, 0)),
        out_shape=jax.ShapeDtypeStruct((m, d), BF16),
        compiler_params=_vmem(32 << 20),
        name="normmod",
    )(x2d, mod, ln_g.reshape(ln_g.shape[0], 1, d))


def _cast_weights(w_ref, wb_ref):
    @pl.when(pl.program_id(1) == 0)
    def _():
        wb_ref[...] = w_ref[0].astype(BF16)


def _proj_plain_kernel(h_ref, w_ref, o_ref, wb_ref):
    _cast_weights(w_ref, wb_ref)
    o_ref[...] = _dot(h_ref[...], wb_ref[...]).astype(BF16)


def _proj_gate_kernel(h_ref, w_ref, o_ref, wb_ref):
    _cast_weights(w_ref, wb_ref)
    acc = _dot(h_ref[...], wb_ref[...])
    o_ref[...] = (0.5 * jnp.tanh(0.5 * acc) + 0.5).astype(BF16)


def _proj_qk_kernel(h_ref, w_ref, qg_ref, kg_ref, cos_ref, sin_ref, o_ref, wb_ref):
    _cast_weights(w_ref, wb_ref)
    h = h_ref[...]
    n_chunks = COL_TILE // MXU_W
    project = lambda t: _dot(h, wb_ref[:, t * MXU_W:(t + 1) * MXU_W])
    is_q = pl.program_id(0) == TILE_Q
    gain = jnp.where(is_q, qg_ref[0] * (LOG2_E * HEAD_DIM ** -0.5), kg_ref[0])
    cos = cos_ref[...]
    sin = sin_ref[...]
    lane = lax.broadcasted_iota(jnp.int32, (1, LANES), 1)
    first_half = (lane % (HEAD_DIM // 2)) < (HEAD_DIM // 4)
    gsum = (lax.broadcasted_iota(jnp.int32, (MXU_W, MXU_W), 0) // HEAD_DIM
            == lax.broadcasted_iota(jnp.int32, (MXU_W, MXU_W), 1) // HEAD_DIM).astype(BF16)
    a_next = project(0)
    for t in range(n_chunks):
        a = a_next
        if t + 1 < n_chunks:
            a_next = project(t + 1)
        ssq = _dot((a * a).astype(BF16), gsum)
        y = a * lax.rsqrt(ssq * (1.0 / HEAD_DIM) + EPS) * gain
        for u in range(MXU_W // LANES):
            yh = y[:, u * LANES:(u + 1) * LANES]
            partner = jnp.where(first_half,
                                pltpu.roll(yh, LANES - HEAD_DIM // 4, 1),
                                pltpu.roll(yh, HEAD_DIM // 4, 1))
            c0 = t * MXU_W + u * LANES
            o_ref[:, c0:c0 + LANES] = (yh * cos + partner * sin).astype(BF16)


def _proj_call(kind, h, w_in, layer, w_tiles, extra=()):
    m, d = h.shape
    tm = 1024
    n_tiles = len(w_tiles)

    def w_tile(j):
        tile = w_tiles[0]
        for idx in range(1, n_tiles):
            tile = jnp.where(j == idx, w_tiles[idx], tile)
        return tile

    in_specs = [
        pl.BlockSpec((tm, d), lambda j, i: (i, 0)),
        pl.BlockSpec((1, d, COL_TILE), lambda j, i: (layer, 0, w_tile(j))),
    ]
    args = [h, w_in]
    if kind == "qk":
        qg, kg, cos_t, sin_t = extra
        seq_tiles = cos_t.shape[0] // tm
        in_specs += [
            pl.BlockSpec((1, 1, MXU_W), lambda j, i: (layer, 0, 0)),
            pl.BlockSpec((1, 1, MXU_W), lambda j, i: (layer, 0, 0)),
            pl.BlockSpec((tm, LANES), lambda j, i: (i % seq_tiles, 0)),
            pl.BlockSpec((tm, LANES), lambda j, i: (i % seq_tiles, 0)),
        ]
        args += [qg, kg, cos_t, sin_t]
    body = {"qk": _proj_qk_kernel, "plain": _proj_plain_kernel, "gate": _proj_gate_kernel}[kind]
    return pl.pallas_call(
        body,
        grid=(n_tiles, m // tm),
        in_specs=in_specs,
        out_specs=pl.BlockSpec((tm, COL_TILE), lambda j, i: (i, j)),
        out_shape=jax.ShapeDtypeStruct((m, n_tiles * COL_TILE), BF16),
        scratch_shapes=[pltpu.VMEM((d, COL_TILE), BF16)],
        compiler_params=_vmem(52 << 20),
        name="proj_" + kind,
    )(*args)


def _attn_kernel(lq_ref, lk_ref, sg_ref, q_ref, *refs, lam_init, pieces, key_block):
    n = len(pieces)
    k_refs, v_refs = refs[0:2 * n:2], refs[1:2 * n:2]
    o_ref, vt_ref, kmax_ref = refs[2 * n:]
    tq = q_ref.shape[0]
    lane = lax.broadcasted_iota(jnp.int32, (1, LANES), 1)

    @pl.when(pl.program_id(2) == 0)
    def _():
        group = lax.broadcasted_iota(jnp.int32, (LANES, LANES), 0) // HEAD_DIM
        gsum = (group == lane // HEAD_DIM).astype(BF16)
        kmax = None
        off = 0
        for t in range(n):
            rows = pieces[t]
            vt_ref[0:V_HEAD_DIM, off:off + rows] = v_refs[t][...].astype(F32).T.astype(BF16)
            kf = k_refs[t][...].astype(F32)
            norms = _dot((kf * kf).astype(BF16), gsum)
            piece_max = jnp.max(norms, axis=0, keepdims=True)
            kmax = piece_max if kmax is None else jnp.maximum(kmax, piece_max)
            off += rows
        vt_ref[V_HEAD_DIM:, :] = jnp.ones((ONES_ROWS, off), BF16)
        kmax_ref[...] = kmax

    e = jnp.exp(jnp.sum(lq_ref[0] * lk_ref[0], axis=-1, keepdims=True))
    lam = e[0:1, :] - e[1:2, :] + lam_init

    q = q_ref[...]
    zero = jnp.zeros_like(q)
    lo = lane < HEAD_DIM
    q_both = jnp.concatenate([jnp.where(lo, q, zero), jnp.where(lo, zero, q)], axis=0)

    qf = q.astype(F32)
    sel = (lax.broadcasted_iota(jnp.int32, (SUBLANES, LANES), 0) == lane // HEAD_DIM).astype(BF16)
    q_norms = _dot_t(sel, (qf * qf).astype(BF16))
    kmax = kmax_ref[...]
    bounds = jnp.concatenate(
        [jnp.sqrt(q_norms[c:c + 1, :] * kmax[:, c * HEAD_DIM:c * HEAD_DIM + 1]) * 1.01 + 1e-6
         for c in range(2)], axis=1)

    def key_blocks():
        off = 0
        for t in range(n):
            for s in range(0, pieces[t], key_block):
                size = min(key_block, pieces[t] - s)
                yield k_refs[t], s, off + s, size
            off += pieces[t]

    def accumulate(shifts):
        blocks = list(key_blocks())
        scores = lambda blk: _dot_t(blk[0][blk[1]:blk[1] + blk[3], :], q_both)
        acc = jnp.zeros((V_HEAD_DIM + ONES_ROWS, 2 * tq), F32)
        st_next = scores(blocks[0])
        for idx, (_, _, pos, size) in enumerate(blocks):
            st = st_next
            if idx + 1 < len(blocks):
                st_next = scores(blocks[idx + 1])
            acc = acc + _dot(vt_ref[:, pos:pos + size], jnp.exp2(st - shifts).astype(BF16))
        return acc

    def finish(acc):
        sums = acc[V_HEAD_DIM:V_HEAD_DIM + 1, :]
        ot = (acc[0:V_HEAD_DIM, 0:tq] * (1.0 / sums[:, 0:tq])
              - acc[0:V_HEAD_DIM, tq:] * (lam / sums[:, tq:]))
        yt = ot * lax.rsqrt(jnp.mean(ot * ot, axis=0, keepdims=True) + EPS)
        o_ref[...] = (yt.T * (sg_ref[0] * (1.0 - lam_init))).astype(BF16)

    acc = accumulate(bounds)
    finish(acc)

    @pl.when(jnp.logical_not(jnp.min(acc[V_HEAD_DIM:V_HEAD_DIM + 1, :]) >= 1e-30))
    def _():
        m = None
        for k_ref, s, _, size in key_blocks():
            bm = jnp.max(_dot_t(k_ref[s:s + size, :], q_both), axis=0, keepdims=True)
            m = bm if m is None else jnp.maximum(m, bm)
        finish(accumulate(m))


def _attn_call(lam_q, lam_k, subln_g, layer, q_arr, q_tile0, kv_list, *, n_batch, q_len,
               lam_init):
    tq = min(1024, q_len)
    q_tiles = q_len // tq
    in_specs = [
        pl.BlockSpec((1, 2, HEAD_DIM), lambda b, h, i: (layer, 0, 0)),
        pl.BlockSpec((1, 2, HEAD_DIM), lambda b, h, i: (layer, 0, 0)),
        pl.BlockSpec((1, 1, V_HEAD_DIM), lambda b, h, i: (layer, 0, 0)),
        pl.BlockSpec((tq, LANES), lambda b, h, i: (b * q_tiles + i, q_tile0 + h)),
    ]
    args = [lam_q, lam_k, subln_g.reshape(subln_g.shape[0], 1, V_HEAD_DIM), q_arr]
    for k_arr, k0, v_arr, v0, rows in kv_list:
        in_specs.append(pl.BlockSpec((rows, LANES), lambda b, h, i, k0=k0: (b, k0 + h)))
        in_specs.append(pl.BlockSpec((rows, LANES), lambda b, h, i, v0=v0: (b, v0 + h)))
        args += [k_arr, v_arr]
    pieces = tuple(kv[4] for kv in kv_list)
    return pl.pallas_call(
        functools.partial(_attn_kernel, lam_init=lam_init, pieces=pieces, key_block=256),
        grid=(n_batch, N_HEADS, q_tiles),
        in_specs=in_specs,
        out_specs=pl.BlockSpec((tq, LANES), lambda b, h, i: (b * q_tiles + i, h)),
        out_shape=jax.ShapeDtypeStruct((n_batch * q_len, N_HEADS * V_HEAD_DIM), BF16),
        scratch_shapes=[pltpu.VMEM((V_HEAD_DIM + ONES_ROWS, sum(pieces)), BF16),
                        pltpu.VMEM((1, LANES), F32)],
        compiler_params=_vmem(52 << 20),
        name="attn",
    )(*args)


def _fourier_dense_kernel(x_ref, cc_ref, sc_ref, pos_ref, o_ref, y_ref, *, n_pos):
    @pl.when(pl.program_id(1) == 0)
    def _():
        cc = cc_ref[...]
        sc = sc_ref[...]
        for g in range(FOURIER_GROUPS):
            xg = x_ref[:, g * GROUP_W:(g + 1) * GROUP_W]
            y_ref[0:n_pos, g * GROUP_W:(g + 1) * GROUP_W] = _dot(xg, cc).astype(BF16)
            y_ref[n_pos:2 * n_pos, g * GROUP_W:(g + 1) * GROUP_W] = _dot(xg, sc).astype(BF16)

    o_ref[...] = _dot(pos_ref[...], y_ref[...]).astype(BF16)


def _fourier_dense_call(p_arr, f_tile, chan_cos, chan_sin, pos_mat, *, n_batch, n_pos):
    tm = min(256, n_pos)
    row_tiles = n_pos // tm
    return pl.pallas_call(
        functools.partial(_fourier_dense_kernel, n_pos=n_pos),
        grid=(n_batch, row_tiles),
        in_specs=[
            pl.BlockSpec((n_pos, BRANCH_W), lambda b, i: (b, f_tile),
                         pipeline_mode=pl.Buffered(1)),
            pl.BlockSpec((GROUP_W, GROUP_W), lambda b, i: (0, 0)),
            pl.BlockSpec((GROUP_W, GROUP_W), lambda b, i: (0, 0)),
            pl.BlockSpec((tm, 2 * n_pos), lambda b, i: (i, 0)),
        ],
        out_specs=pl.BlockSpec((tm, BRANCH_W), lambda b, i: (b * row_tiles + i, 0)),
        out_shape=jax.ShapeDtypeStruct((n_batch * n_pos, BRANCH_W), BF16),
        scratch_shapes=[pltpu.VMEM((2 * n_pos, BRANCH_W), BF16)],
        compiler_params=_vmem(32 << 20),
        name="fourier_dense",
    )(p_arr, chan_cos, chan_sin, pos_mat)


def _fourier_latent_kernel(x_ref, wa_ref, wc_ref, twc_ref, tws_ref, cs_ref, o_ref,
                           xs_ref, ar_ref, ai_ref, zr_ref, zi_ref):
    tiles = GROUP_W // LANES
    lane_tile = lambda v, t: v[:, t * LANES:(t + 1) * LANES]
    gather = lambda ref, rows: jnp.concatenate([ref[t, rows, :] for t in range(tiles)], axis=1)

    for t in range(tiles):
        xs_ref[t] = lane_tile(x_ref, t).astype(F32)
    wa = wa_ref[...]
    wc = wc_ref[...]

    def strided(i):
        return pl.ds(i, FFT_R, stride=FFT_R)

    def block_rows(i):
        return pl.ds(pl.multiple_of(i * FFT_R, FFT_R), FFT_R)

    def stage1(b, carry):
        a = _dot(wa, gather(xs_ref, strided(b)).astype(BF16))
        tc = twc_ref[block_rows(b), :]
        ts = tws_ref[block_rows(b), :]
        for t in range(tiles):
            re = lane_tile(a[0:FFT_R], t)
            im = lane_tile(a[FFT_R:], t)
            ar_ref[t, block_rows(b), :] = re * tc + im * ts
            ai_ref[t, block_rows(b), :] = im * tc - re * ts
        return carry

    lax.fori_loop(0, FFT_R, stage1, 0, unroll=8)

    def stage2(c, carry):
        a = jnp.concatenate([gather(ar_ref, strided(c)), gather(ai_ref, strided(c))], axis=0)
        z = _dot(wc, a.astype(BF16))
        for t in range(tiles):
            zr_ref[t, strided(c), :] = lane_tile(z[0:FFT_R], t)
            zi_ref[t, strided(c), :] = lane_tile(z[FFT_R:], t)
        return carry

    lax.fori_loop(0, FFT_R, stage2, 0, unroll=8)

    cs = cs_ref[...]
    block = 512
    for r in range(0, x_ref.shape[0], block):
        rows = slice(r, r + block)
        z = jnp.concatenate([gather(zr_ref, rows), gather(zi_ref, rows)], axis=1)
        o_ref[rows, :] = _dot(z.astype(BF16), cs).astype(BF16)


def _fourier_latent_call(plain, wa, wc, tw_cos, tw_sin, chan_cs, *, n_batch):
    n_pos = FFT_R * FFT_R
    tiles = GROUP_W // LANES
    const = lambda shape: pl.BlockSpec(shape, lambda n, q: (0,) * len(shape))
    scratch = pltpu.VMEM((tiles, n_pos, LANES), F32)
    return pl.pallas_call(
        _fourier_latent_kernel,
        grid=(n_batch, FOURIER_GROUPS),
        in_specs=[
            pl.BlockSpec((n_pos, GROUP_W), lambda n, q: (n, PLAIN_F * FOURIER_GROUPS + q)),
            const((2 * FFT_R, FFT_R)),
            const((2 * FFT_R, 2 * FFT_R)),
            const((n_pos, LANES)),
            const((n_pos, LANES)),
            const((2 * GROUP_W, GROUP_W)),
        ],
        out_specs=pl.BlockSpec((n_pos, GROUP_W), lambda n, q: (n, q)),
        out_shape=jax.ShapeDtypeStruct((n_batch * n_pos, BRANCH_W), BF16),
        scratch_shapes=[scratch] * 5,
        compiler_params=_vmem(48 << 20),
        name="fourier_latent",
    )(plain, wa, wc, tw_cos, tw_sin, chan_cs)


def _merge_kernel(x_ref, mod_ref, attn_ref, conv_ref, before_ref, after_ref, four_ref, gate_ref,
                  wconv_ref, wbr_ref, wo_ref, ln2_ref, o_ref, h2_ref, *, seq_len):
    tm, d = x_ref.shape
    i = pl.program_id(0)
    pos0 = (i * tm) % seq_len
    cc_cols = slice(PLAIN_CC * BRANCH_W, (PLAIN_CC + 1) * BRANCH_W)
    cx_cols = slice(PLAIN_CX * BRANCH_W, (PLAIN_CX + 1) * BRANCH_W)
    cb_cols = slice(PLAIN_CB * BRANCH_W, (PLAIN_CB + 1) * BRANCH_W)

    u = conv_ref[:, cc_cols].astype(F32) * conv_ref[:, cx_cols].astype(F32)
    last = slice(SUBLANES - 1, SUBLANES)
    u_before = before_ref[last, cc_cols].astype(F32) * before_ref[last, cx_cols].astype(F32)
    u_after = after_ref[0:1, cc_cols].astype(F32) * after_ref[0:1, cx_cols].astype(F32)
    u_before = jnp.where(pos0 == 0, 0.0, u_before)
    u_after = jnp.where(pos0 + tm == seq_len, 0.0, u_after)
    row = lax.broadcasted_iota(jnp.int32, (tm, 1), 0)
    u_prev = jnp.where(row == 0, u_before, pltpu.roll(u, 1, 0))
    u_next = jnp.where(row == tm - 1, u_after, pltpu.roll(u, tm - 1, 0))
    w = wconv_ref[0]
    conv = (conv_ref[:, cb_cols].astype(F32)
            * (u_prev * w[0:1, :] + u * w[1:2, :] + u_next * w[2:3, :]))

    gate = lambda branch: gate_ref[:, branch * d:(branch + 1) * d].astype(F32)
    mix = (gate(0) * _dot(attn_ref[...], wbr_ref[0, 0])
           + gate(1) * _dot(conv.astype(BF16), wbr_ref[0, 1])
           + gate(2) * _dot(four_ref[...], wbr_ref[0, 2]))
    y = _dot(mix.astype(BF16), wo_ref[0])
    x_new = x_ref[...] + mod_ref[0, 0, 2:3, :] * y
    o_ref[...] = x_new
    h2 = _rms_modulate(x_new, ln2_ref[0], mod_ref[0, 0, 3:4, :], mod_ref[0, 0, 4:5, :])
    h2_ref[...] = h2.astype(BF16)


def _merge_call(x2d, mod, layer, mod_row, attn_o, plain, gates, four_o, w_conv, w_br, w_o,
                ln2_g, *, seq_len):
    m, d = x2d.shape
    tm = 256
    halo_per_tile = tm // SUBLANES
    last_halo = m // SUBLANES - 1
    const = dict(pipeline_mode=pl.Buffered(1))
    assert (PLAIN_CC, PLAIN_CX, PLAIN_CB) == (0, 1, 2)

    return pl.pallas_call(
        functools.partial(_merge_kernel, seq_len=seq_len),
        grid=(m // tm,),
        in_specs=[
            pl.BlockSpec((tm, d), lambda i: (i, 0)),
            pl.BlockSpec((1, 1, 6, d), lambda i: (layer, mod_row(i * tm), 0, 0)),
            pl.BlockSpec((tm, BRANCH_W), lambda i: (i, 0)),
            pl.BlockSpec((tm, 3 * BRANCH_W), lambda i: (i, 0)),
            pl.BlockSpec((SUBLANES, 2 * BRANCH_W),
                         lambda i: (jnp.maximum(i * halo_per_tile - 1, 0), 0)),
            pl.BlockSpec((SUBLANES, 2 * BRANCH_W),
                         lambda i: (jnp.minimum((i + 1) * halo_per_tile, last_halo), 0)),
            pl.BlockSpec((tm, BRANCH_W), lambda i: (i, 0)),
            pl.BlockSpec((tm, N_BRANCH * d), lambda i: (i, 0)),
            pl.BlockSpec((1, CONV_K, BRANCH_W), lambda i: (layer, 0, 0)),
            pl.BlockSpec((1, N_BRANCH, BRANCH_W, d), lambda i: (layer, 0, 0, 0), **const),
            pl.BlockSpec((1, d, d), lambda i: (layer, 0, 0), **const),
            pl.BlockSpec((1, 1, d), lambda i: (layer, 0, 0)),
        ],
        out_specs=[pl.BlockSpec((tm, d), lambda i: (i, 0))] * 2,
        out_shape=[jax.ShapeDtypeStruct((m, d), F32), jax.ShapeDtypeStruct((m, d), BF16)],
        compiler_params=_vmem(56 << 20),
        name="merge",
    )(x2d, mod, attn_o, plain, plain, plain, four_o, gates, w_conv, w_br, w_o,
      ln2_g.reshape(ln2_g.shape[0], 1, d))


def _mlp_kernel(x_ref, h_ref, mod_ref, w1_ref, w2_ref, o_ref, *, out_chunk):
    j = pl.program_id(1)

    @pl.when(j == 0)
    def _():
        o_ref[...] = jnp.zeros_like(o_ref)

    h = h_ref[...]
    half = w1_ref.shape[2] // 2
    up = lambda s: jnp.maximum(_dot(h, w1_ref[0, :, s * half:(s + 1) * half]), 0.0)
    hid_next = up(0)
    for s in range(2):
        hid = hid_next
        if s == 0:
            hid_next = up(1)
        hid2 = (hid * hid).astype(BF16)
        for n0 in range(0, o_ref.shape[1], out_chunk):
            cols = slice(n0, n0 + out_chunk)
            o_ref[:, cols] += (mod_ref[0, 0, 5:6, cols]
                               * _dot(hid2, w2_ref[0, s * half:(s + 1) * half, cols]))

    @pl.when(j == pl.num_programs(1) - 1)
    def _():
        o_ref[...] += x_ref[...]


def _mlp_call(x2d, h2, mod, layer, mod_row, w1, w2):
    m, d = x2d.shape
    d_ff = w1.shape[2]
    tm, tf = 1024, 512
    steps = d_ff // tf
    x_rows = lambda i, j: (jnp.where(j >= steps // 2, i, jnp.maximum(i - 1, 0)), 0)
    return pl.pallas_call(
        functools.partial(_mlp_kernel, out_chunk=512),
        grid=(m // tm, steps),
        in_specs=[
            pl.BlockSpec((tm, d), x_rows),
            pl.BlockSpec((tm, d), lambda i, j: (i, 0)),
            pl.BlockSpec((1, 1, 6, d), lambda i, j: (layer, mod_row(i * tm), 0, 0)),
            pl.BlockSpec((1, d, tf), lambda i, j: (layer, 0, j)),
            pl.BlockSpec((1, tf, d), lambda i, j: (layer, j, 0)),
        ],
        out_specs=pl.BlockSpec((tm, d), lambda i, j: (i, 0)),
        out_shape=jax.ShapeDtypeStruct((m, d), F32),
        compiler_params=_vmem(56 << 20),
        name="mlp",
    )(x2d, h2, mod, w1, w2)


def _rope_tables(seq_len):
    rows = seq_len // GRID_W
    t_row = jnp.repeat(jnp.arange(rows), GRID_W).astype(F32)
    t_col = jnp.tile(jnp.arange(GRID_W), rows).astype(F32)
    axis_dim = HEAD_DIM // 2
    inv_freq = 1.0 / (ROPE_THETA ** (jnp.arange(0, axis_dim, 2, dtype=F32) / axis_dim))
    ang_r = t_row[:, None] * inv_freq[None, :]
    ang_c = t_col[:, None] * inv_freq[None, :]
    cos = jnp.concatenate([jnp.cos(ang_r)] * 2 + [jnp.cos(ang_c)] * 2, axis=-1)
    sin = jnp.concatenate([-jnp.sin(ang_r), jnp.sin(ang_r), -jnp.sin(ang_c), jnp.sin(ang_c)],
                          axis=-1)
    return jnp.tile(cos, (1, LANES // HEAD_DIM)), jnp.tile(sin, (1, LANES // HEAD_DIM))


def _dft_cos_sin(rows, cols, period, scale):
    r = jnp.arange(rows, dtype=jnp.int32)
    c = jnp.arange(cols, dtype=jnp.int32)
    ang = ((r[:, None] * c[None, :]) % period).astype(F32) * (2.0 * math.pi / period)
    return jnp.cos(ang) * scale, jnp.sin(ang) * scale


def kernel(x, c, ctx, c_ctx, ln1_g, ln2_g, w_ada, b_ada, w_in, qn_g, kn_g, lam_q, lam_k,
           subln_g, w_conv, w_br, w_o, w1, w2):
    n_batch, seq, d = x.shape
    ctx_len = ctx.shape[1]
    depth = w_ada.shape[0]
    assert seq == FFT_R * FFT_R and ctx_len == 256 and n_batch <= CTX_MOD_ROW and d == D_MODEL

    c_rows = jnp.zeros((MOD_ROWS, d), F32).at[:n_batch].set(c).at[CTX_MOD_ROW].set(c_ctx)
    mod = _mod_call(c_rows, w_ada, b_ada).reshape(depth, MOD_ROWS, 6, d)

    cos_l, sin_l = _rope_tables(seq)
    cos_c = jnp.ones((n_batch * ctx_len, LANES), F32)
    sin_c = jnp.zeros((n_batch * ctx_len, LANES), F32)
    qg = jnp.tile(qn_g, (1, MXU_W // HEAD_DIM)).reshape(depth, 1, MXU_W)
    kg = jnp.tile(kn_g, (1, MXU_W // HEAD_DIM)).reshape(depth, 1, MXU_W)

    ch_c, ch_s = _dft_cos_sin(GROUP_W, GROUP_W, GROUP_W, GROUP_W ** -0.5)
    chan_cos, chan_sin = ch_c.astype(BF16), ch_s.astype(BF16)
    chan_cs = jnp.concatenate([chan_cos, chan_sin], axis=0)
    pc_c, pc_s = _dft_cos_sin(ctx_len, ctx_len, ctx_len, ctx_len ** -0.5)
    pos_c = jnp.concatenate([pc_c, -pc_s], axis=1).astype(BF16)
    r_c, r_s = _dft_cos_sin(FFT_R, FFT_R, FFT_R, FFT_R ** -0.5)
    wa = jnp.concatenate([r_c, -r_s], axis=0).astype(BF16)
    wc = jnp.concatenate([jnp.concatenate([r_c, r_s], axis=1),
                          jnp.concatenate([-r_s, r_c], axis=1)], axis=0).astype(BF16)
    tw_c, tw_s = _dft_cos_sin(FFT_R, FFT_R, seq, 1.0)
    tw_cos = jnp.broadcast_to(tw_c.reshape(seq, 1), (seq, LANES))
    tw_sin = jnp.broadcast_to(tw_s.reshape(seq, 1), (seq, LANES))

    w_br_b = w_br.astype(BF16)
    w_o_b = w_o.astype(BF16)
    w1_b = w1.astype(BF16)
    w2_b = w2.astype(BF16)

    lat_row = lambda row: row // seq
    ctx_row = lambda row: CTX_MOD_ROW
    heads = COL_TILE // LANES

    xl = x.reshape(n_batch * seq, d)
    xc = ctx.reshape(n_batch * ctx_len, d)

    for l in range(depth):
        last = l == depth - 1
        lam_init = 0.8 - 0.6 * math.exp(-0.3 * l)

        hc = _normmod_call(xc, mod, l, ctx_row, ln1_g)
        hl = _normmod_call(xl, mod, l, lat_row, ln1_g)
        qk_c = _proj_call("qk", hc, w_in, l, QK_TILES, (qg, kg, cos_c, sin_c))
        qk_l = _proj_call("qk", hl, w_in, l, QK_TILES, (qg, kg, cos_l, sin_l))
        plain_c = _proj_call("plain", hc, w_in, l, (TILE_V,) if last else PLAIN_TILES)
        plain_l = _proj_call("plain", hl, w_in, l, PLAIN_TILES)
        gates_l = _proj_call("gate", hl, w_in, l, GATE_TILES)

        kv_ctx = (qk_c, heads, plain_c, (0 if last else PLAIN_V) * heads, ctx_len)
        kv_lat = (qk_l, heads, plain_l, PLAIN_V * heads, seq)
        attn_l = _attn_call(lam_q, lam_k, subln_g, l, qk_l, 0, [kv_ctx, kv_lat],
                            n_batch=n_batch, q_len=seq, lam_init=lam_init)
        four_l = _fourier_latent_call(plain_l, wa, wc, tw_cos, tw_sin, chan_cs, n_batch=n_batch)
        xl, h2l = _merge_call(xl, mod, l, lat_row, attn_l, plain_l, gates_l, four_l, w_conv,
                              w_br_b, w_o_b, ln2_g, seq_len=seq)
        xl = _mlp_call(xl, h2l, mod, l, lat_row, w1_b, w2_b)

        if not last:
            gates_c = _proj_call("gate", hc, w_in, l, GATE_TILES)
            attn_c = _attn_call(lam_q, lam_k, subln_g, l, qk_c, 0, [kv_ctx],
                                n_batch=n_batch, q_len=ctx_len, lam_init=lam_init)
            four_c = _fourier_dense_call(plain_c, PLAIN_F, chan_cos, chan_sin, pos_c,
                                         n_batch=n_batch, n_pos=ctx_len)
            xc, h2c = _merge_call(xc, mod, l, ctx_row, attn_c, plain_c, gates_c, four_c, w_conv,
                                  w_br_b, w_o_b, ln2_g, seq_len=ctx_len)
            xc = _mlp_call(xc, h2c, mod, l, ctx_row, w1_b, w2_b)

    return xl.reshape(n_batch, seq, d)
```

```python
import functools
import math

import jax
import jax.numpy as jnp
from jax import lax
from jax.experimental import pallas as pl
from jax.experimental.pallas import tpu as pltpu

D_MODEL = 2048
GRID_W = 64
N_HEADS = 8
HEAD_DIM = 64
V_HEAD_DIM = 128
BRANCH_W = 1024
FOURIER_GROUPS = 4
GROUP_W = BRANCH_W // FOURIER_GROUPS
N_BRANCH = 3
CONV_K = 3
EPS = 1e-6
ROPE_THETA = 10000.0

COL_TILE = 1024
TILE_Q, TILE_K, TILE_V, TILE_CB, TILE_CC, TILE_CX, TILE_F, TILE_G = range(8)
QK_TILES = (TILE_Q, TILE_K)
PLAIN_TILES = (TILE_CC, TILE_CX, TILE_CB, TILE_V, TILE_F)
PLAIN_CC, PLAIN_CX, PLAIN_CB, PLAIN_V, PLAIN_F = range(5)
GATE_TILES = tuple(range(TILE_G, TILE_G + N_BRANCH * D_MODEL // COL_TILE))

LANES = 128
SUBLANES = 8
MXU_W = 256
ONES_ROWS = 16
LOG2_E = math.log2(math.e)
MOD_ROWS = 8
CTX_MOD_ROW = 4
FFT_R = 64

F32 = jnp.float32
BF16 = jnp.bfloat16


def _vmem(nbytes):
    return pltpu.CompilerParams(vmem_limit_bytes=nbytes)


def _sigmoid(x):
    return 1.0 / (1.0 + jnp.exp(-x))


def _rms_modulate(x, g, shift, scale):
    y = x * lax.rsqrt(jnp.mean(x * x, axis=-1, keepdims=True) + EPS) * g
    return y * (1.0 + scale) + shift


def _dot(a, b):
    return jnp.dot(a, b, preferred_element_type=F32)


def _dot_t(a, b):
    return lax.dot_general(a, b, (((1,), (1,)), ((), ())), preferred_element_type=F32)


def _mod_kernel(c_ref, w_ref, b_ref, o_ref):
    c = c_ref[...]
    cs = c * _sigmoid(c)
    o_ref[0] = jnp.dot(cs, w_ref[0], preferred_element_type=F32,
                       precision=lax.Precision.HIGHEST) + b_ref[0]


def _mod_call(c_rows, w_ada, b_ada):
    depth, d, n = w_ada.shape
    tn = 1536
    return pl.pallas_call(
        _mod_kernel,
        grid=(depth, n // tn),
        in_specs=[
            pl.BlockSpec((MOD_ROWS, d), lambda l, j: (0, 0)),
            pl.BlockSpec((1, d, tn), lambda l, j: (l, 0, j)),
            pl.BlockSpec((1, 1, tn), lambda l, j: (l, 0, j)),
        ],
        out_specs=pl.BlockSpec((1, MOD_ROWS, tn), lambda l, j: (l, 0, j)),
        out_shape=jax.ShapeDtypeStruct((depth, MOD_ROWS, n), F32),
        compiler_params=_vmem(40 << 20),
        name="mod",
    )(c_rows, w_ada, b_ada.reshape(depth, 1, n))


def _normmod_kernel(x_ref, mod_ref, g_ref, o_ref):
    h = _rms_modulate(x_ref[...], g_ref[0], mod_ref[0, 0, 0:1, :], mod_ref[0, 0, 1:2, :])
    o_ref[...] = h.astype(BF16)


def _normmod_call(x2d, mod, layer, mod_row, ln_g):
    m, d = x2d.shape
    tm = 512
    return pl.pallas_call(
        _normmod_kernel,
        grid=(m // tm,),
        in_specs=[
            pl.BlockSpec((tm, d), lambda i: (i, 0)),
            pl.BlockSpec((1, 1, 6, d), lambda i: (layer, mod_row(i * tm), 0, 0)),
            pl.BlockSpec((1, 1, d), lambda i: (layer, 0, 0)),
        ],
        out_specs=pl.BlockSpec((tm, d), lambda i: (i, 0)),
        out_shape=jax.ShapeDtypeStruct((m, d), BF16),
        compiler_params=_vmem(32 << 20),
        name="normmod",
    )(x2d, mod, ln_g.reshape(ln_g.shape[0], 1, d))


def _cast_weights(w_ref, wb_ref):
    @pl.when(pl.program_id(1) == 0)
    def _():
        wb_ref[...] = w_ref[0].astype(BF16)


def _proj_plain_kernel(h_ref, w_ref, o_ref, wb_ref):
    _cast_weights(w_ref, wb_ref)
    o_ref[...] = _dot(h_ref[...], wb_ref[...]).astype(BF16)


def _proj_gate_kernel(h_ref, w_ref, o_ref, wb_ref):
    _cast_weights(w_ref, wb_ref)
    acc = _dot(h_ref[...], wb_ref[...])
    o_ref[...] = (0.5 * jnp.tanh(0.5 * acc) + 0.5).astype(BF16)


def _proj_qk_kernel(h_ref, w_ref, qg_ref, kg_ref, cos_ref, sin_ref, o_ref, wb_ref):
    _cast_weights(w_ref, wb_ref)
    h = h_ref[...]
    n_chunks = COL_TILE // MXU_W
    project = lambda t: _dot(h, wb_ref[:, t * MXU_W:(t + 1) * MXU_W])
    is_q = pl.program_id(0) == TILE_Q
    gain = jnp.where(is_q, qg_ref[0] * (LOG2_E * HEAD_DIM ** -0.5), kg_ref[0])
    cos = cos_ref[...]
    sin = sin_ref[...]
    lane = lax.broadcasted_iota(jnp.int32, (1, LANES), 1)
    first_half = (lane % (HEAD_DIM // 2)) < (HEAD_DIM // 4)
    gsum = (lax.broadcasted_iota(jnp.int32, (MXU_W, MXU_W), 0) // HEAD_DIM
            == lax.broadcasted_iota(jnp.int32, (MXU_W, MXU_W), 1) // HEAD_DIM).astype(BF16)
    a_next = project(0)
    for t in range(n_chunks):
        a = a_next
        if t + 1 < n_chunks:
            a_next = project(t + 1)
        ssq = _dot((a * a).astype(BF16), gsum)
        y = a * lax.rsqrt(ssq * (1.0 / HEAD_DIM) + EPS) * gain
        for u in range(MXU_W // LANES):
            yh = y[:, u * LANES:(u + 1) * LANES]
            partner = jnp.where(first_half,
                                pltpu.roll(yh, LANES - HEAD_DIM // 4, 1),
                                pltpu.roll(yh, HEAD_DIM // 4, 1))
            c0 = t * MXU_W + u * LANES
            o_ref[:, c0:c0 + LANES] = (yh * cos + partner * sin).astype(BF16)


def _proj_call(kind, h, w_in, layer, w_tiles, extra=()):
    m, d = h.shape
    tm = 1024
    n_tiles = len(w_tiles)

    def w_tile(j):
        tile = w_tiles[0]
        for idx in range(1, n_tiles):
            tile = jnp.where(j == idx, w_tiles[idx], tile)
        return tile

    in_specs = [
        pl.BlockSpec((tm, d), lambda j, i: (i, 0)),
        pl.BlockSpec((1, d, COL_TILE), lambda j, i: (layer, 0, w_tile(j))),
    ]
    args = [h, w_in]
    if kind == "qk":
        qg, kg, cos_t, sin_t = extra
        seq_tiles = cos_t.shape[0] // tm
        in_specs += [
            pl.BlockSpec((1, 1, MXU_W), lambda j, i: (layer, 0, 0)),
            pl.BlockSpec((1, 1, MXU_W), lambda j, i: (layer, 0, 0)),
            pl.BlockSpec((tm, LANES), lambda j, i: (i % seq_tiles, 0)),
            pl.BlockSpec((tm, LANES), lambda j, i: (i % seq_tiles, 0)),
        ]
        args += [qg, kg, cos_t, sin_t]
    body = {"qk": _proj_qk_kernel, "plain": _proj_plain_kernel, "gate": _proj_gate_kernel}[kind]
    return pl.pallas_call(
        body,
        grid=(n_tiles, m // tm),
        in_specs=in_specs,
        out_specs=pl.BlockSpec((tm, COL_TILE), lambda j, i: (i, j)),
        out_shape=jax.ShapeDtypeStruct((m, n_tiles * COL_TILE), BF16),
        scratch_shapes=[pltpu.VMEM((d, COL_TILE), BF16)],
        compiler_params=_vmem(52 << 20),
        name="proj_" + kind,
    )(*args)


def _attn_kernel(lq_ref, lk_ref, sg_ref, q_ref, *refs, lam_init, pieces, key_block):
    n = len(pieces)
    k_refs, v_refs = refs[0:2 * n:2], refs[1:2 * n:2]
    o_ref, vt_ref, kmax_ref = refs[2 * n:]
    tq = q_ref.shape[0]
    lane = lax.broadcasted_iota(jnp.int32, (1, LANES), 1)

    @pl.when(pl.program_id(2) == 0)
    def _():
        group = lax.broadcasted_iota(jnp.int32, (LANES, LANES), 0) // HEAD_DIM
        gsum = (group == lane // HEAD_DIM).astype(BF16)
        kmax = None
        off = 0
        for t in range(n):
            rows = pieces[t]
            vt_ref[0:V_HEAD_DIM, off:off + rows] = v_refs[t][...].astype(F32).T.astype(BF16)
            kf = k_refs[t][...].astype(F32)
            norms = _dot((kf * kf).astype(BF16), gsum)
            piece_max = jnp.max(norms, axis=0, keepdims=True)
            kmax = piece_max if kmax is None else jnp.maximum(kmax, piece_max)
            off += rows
        vt_ref[V_HEAD_DIM:, :] = jnp.ones((ONES_ROWS, off), BF16)
        kmax_ref[...] = kmax

    e = jnp.exp(jnp.sum(lq_ref[0] * lk_ref[0], axis=-1, keepdims=True))
    lam = e[0:1, :] - e[1:2, :] + lam_init

    q = q_ref[...]
    zero = jnp.zeros_like(q)
    lo = lane < HEAD_DIM
    q_both = jnp.concatenate([jnp.where(lo, q, zero), jnp.where(lo, zero, q)], axis=0)

    qf = q.astype(F32)
    sel = (lax.broadcasted_iota(jnp.int32, (SUBLANES, LANES), 0) == lane // HEAD_DIM).astype(BF16)
    q_norms = _dot_t(sel, (qf * qf).astype(BF16))
    kmax = kmax_ref[...]
    bounds = jnp.concatenate(
        [jnp.sqrt(q_norms[c:c + 1, :] * kmax[:, c * HEAD_DIM:c * HEAD_DIM + 1]) * 1.01 + 1e-6
         for c in range(2)], axis=1)

    def key_blocks():
        off = 0
        for t in range(n):
            for s in range(0, pieces[t], key_block):
                size = min(key_block, pieces[t] - s)
                yield k_refs[t], s, off + s, size
            off += pieces[t]

    def accumulate(shifts):
        blocks = list(key_blocks())
        scores = lambda blk: _dot_t(blk[0][blk[1]:blk[1] + blk[3], :], q_both)
        acc = jnp.zeros((V_HEAD_DIM + ONES_ROWS, 2 * tq), F32)
        st_next = scores(blocks[0])
        for idx, (_, _, pos, size) in enumerate(blocks):
            st = st_next
            if idx + 1 < len(blocks):
                st_next = scores(blocks[idx + 1])
            acc = acc + _dot(vt_ref[:, pos:pos + size], jnp.exp2(st - shifts).astype(BF16))
        return acc

    def finish(acc):
        sums = acc[V_HEAD_DIM:V_HEAD_DIM + 1, :]
        ot = (acc[0:V_HEAD_DIM, 0:tq] * (1.0 / sums[:, 0:tq])
              - acc[0:V_HEAD_DIM, tq:] * (lam / sums[:, tq:]))
        yt = ot * lax.rsqrt(jnp.mean(ot * ot, axis=0, keepdims=True) + EPS)
        o_ref[...] = (yt.T * (sg_ref[0] * (1.0 - lam_init))).astype(BF16)

    acc = accumulate(bounds)
    finish(acc)

    @pl.when(jnp.logical_not(jnp.min(acc[V_HEAD_DIM:V_HEAD_DIM + 1, :]) >= 1e-30))
    def _():
        m = None
        for k_ref, s, _, size in key_blocks():
            bm = jnp.max(_dot_t(k_ref[s:s + size, :], q_both), axis=0, keepdims=True)
            m = bm if m is None else jnp.maximum(m, bm)
        finish(accumulate(m))


def _attn_call(lam_q, lam_k, subln_g, layer, q_arr, q_tile0, kv_list, *, n_batch, q_len,
               lam_init):
    tq = min(1024, q_len)
    q_tiles = q_len // tq
    in_specs = [
        pl.BlockSpec((1, 2, HEAD_DIM), lambda b, h, i: (layer, 0, 0)),
        pl.BlockSpec((1, 2, HEAD_DIM), lambda b, h, i: (layer, 0, 0)),
        pl.BlockSpec((1, 1, V_HEAD_DIM), lambda b, h, i: (layer, 0, 0)),
        pl.BlockSpec((tq, LANES), lambda b, h, i: (b * q_tiles + i, q_tile0 + h)),
    ]
    args = [lam_q, lam_k, subln_g.reshape(subln_g.shape[0], 1, V_HEAD_DIM), q_arr]
    for k_arr, k0, v_arr, v0, rows in kv_list:
        in_specs.append(pl.BlockSpec((rows, LANES), lambda b, h, i, k0=k0: (b, k0 + h)))
        in_specs.append(pl.BlockSpec((rows, LANES), lambda b, h, i, v0=v0: (b, v0 + h)))
        args += [k_arr, v_arr]
    pieces = tuple(kv[4] for kv in kv_list)
    return pl.pallas_call(
        functools.partial(_attn_kernel, lam_init=lam_init, pieces=pieces, key_block=256),
        grid=(n_batch, N_HEADS, q_tiles),
        in_specs=in_specs,
        out_specs=pl.BlockSpec((tq, LANES), lambda b, h, i: (b * q_tiles + i, h)),
        out_shape=jax.ShapeDtypeStruct((n_batch * q_len, N_HEADS * V_HEAD_DIM), BF16),
        scratch_shapes=[pltpu.VMEM((V_HEAD_DIM + ONES_ROWS, sum(pieces)), BF16),
                        pltpu.VMEM((1, LANES), F32)],
        compiler_params=_vmem(52 << 20),
        name="attn",
    )(*args)


def _fourier_dense_kernel(x_ref, cc_ref, sc_ref, pos_ref, o_ref, y_ref, *, n_pos):
    @pl.when(pl.program_id(1) == 0)
    def _():
        cc = cc_ref[...]
        sc = sc_ref[...]
        for g in range(FOURIER_GROUPS):
            xg = x_ref[:, g * GROUP_W:(g + 1) * GROUP_W]
            y_ref[0:n_pos, g * GROUP_W:(g + 1) * GROUP_W] = _dot(xg, cc).astype(BF16)
            y_ref[n_pos:2 * n_pos, g * GROUP_W:(g + 1) * GROUP_W] = _dot(xg, sc).astype(BF16)

    o_ref[...] = _dot(pos_ref[...], y_ref[...]).astype(BF16)


def _fourier_dense_call(p_arr, f_tile, chan_cos, chan_sin, pos_mat, *, n_batch, n_pos):
    tm = min(256, n_pos)
    row_tiles = n_pos // tm
    return pl.pallas_call(
        functools.partial(_fourier_dense_kernel, n_pos=n_pos),
        grid=(n_batch, row_tiles),
        in_specs=[
            pl.BlockSpec((n_pos, BRANCH_W), lambda b, i: (b, f_tile),
                         pipeline_mode=pl.Buffered(1)),
            pl.BlockSpec((GROUP_W, GROUP_W), lambda b, i: (0, 0)),
            pl.BlockSpec((GROUP_W, GROUP_W), lambda b, i: (0, 0)),
            pl.BlockSpec((tm, 2 * n_pos), lambda b, i: (i, 0)),
        ],
        out_specs=pl.BlockSpec((tm, BRANCH_W), lambda b, i: (b * row_tiles + i, 0)),
        out_shape=jax.ShapeDtypeStruct((n_batch * n_pos, BRANCH_W), BF16),
        scratch_shapes=[pltpu.VMEM((2 * n_pos, BRANCH_W), BF16)],
        compiler_params=_vmem(32 << 20),
        name="fourier_dense",
    )(p_arr, chan_cos, chan_sin, pos_mat)


def _fourier_latent_kernel(x_ref, wa_ref, wc_ref, twc_ref, tws_ref, cs_ref, o_ref,
                           xs_ref, ar_ref, ai_ref, zr_ref, zi_ref):
    tiles = GROUP_W // LANES
    half = FFT_R // 2
    lane_tile = lambda v, t: v[:, t * LANES:(t + 1) * LANES]
    pack = lambda v: pltpu.bitcast(v.astype(BF16), jnp.int32)

    def gather(ref, rows):
        words = jnp.concatenate([ref[t, rows, :] for t in range(tiles)], axis=1)
        return pltpu.bitcast(words, BF16)

    def scatter(ref, rows, v):
        words = pack(v)
        for t in range(tiles):
            ref[t, rows, :] = lane_tile(words, t)

    for t in range(tiles):
        xs_ref[t] = pltpu.bitcast(lane_tile(x_ref, t), jnp.int32)
    wa = wa_ref[...]
    wc = wc_ref[...]

    def stage1(beta, carry):
        xb = gather(xs_ref, pl.ds(beta, FFT_R, stride=half))
        a = _dot(wa, xb)
        tw_rows = pl.ds(pl.multiple_of(beta * 2 * FFT_R, 2 * FFT_R), 2 * FFT_R)
        tc = twc_ref[tw_rows, :]
        ts = tws_ref[tw_rows, :]
        re, im = a[0:2 * FFT_R], a[2 * FFT_R:]
        tc = jnp.concatenate([tc] * tiles, axis=1)
        ts = jnp.concatenate([ts] * tiles, axis=1)
        rows = pl.ds(pl.multiple_of(beta * FFT_R, FFT_R), FFT_R)
        scatter(ar_ref, rows, re * tc + im * ts)
        scatter(ai_ref, rows, im * tc - re * ts)
        return carry

    lax.fori_loop(0, half, stage1, 0, unroll=4)

    def stage2(gamma, carry):
        parts = []
        for p in range(2):
            rows = pl.ds(2 * gamma + p, half, stride=FFT_R)
            parts += [gather(ar_ref, rows), gather(ai_ref, rows)]
        z = _dot(wc, jnp.concatenate(parts, axis=0))
        rows = pl.ds(gamma, FFT_R, stride=half)
        scatter(zr_ref, rows, z[0:2 * FFT_R])
        scatter(zi_ref, rows, z[2 * FFT_R:])
        return carry

    lax.fori_loop(0, half, stage2, 0, unroll=4)

    cs = cs_ref[...]
    block = 512
    for r in range(0, x_ref.shape[0], block):
        rows = slice(r // 2, (r + block) // 2)
        z = jnp.concatenate([gather(zr_ref, rows), gather(zi_ref, rows)], axis=1)
        o_ref[r:r + block, :] = _dot(z, cs).astype(BF16)


def _fourier_latent_call(plain, wa, wc, tw_cos, tw_sin, chan_cs, *, n_batch):
    n_pos = FFT_R * FFT_R
    tiles = GROUP_W // LANES
    const = lambda shape: pl.BlockSpec(shape, lambda n, q: (0,) * len(shape))
    scratch = pltpu.VMEM((tiles, n_pos // 2, LANES), jnp.int32)
    return pl.pallas_call(
        _fourier_latent_kernel,
        grid=(n_batch, FOURIER_GROUPS),
        in_specs=[
            pl.BlockSpec((n_pos, GROUP_W), lambda n, q: (n, PLAIN_F * FOURIER_GROUPS + q)),
            const((4 * FFT_R, 2 * FFT_R)),
            const((4 * FFT_R, 4 * FFT_R)),
            const((n_pos, LANES)),
            const((n_pos, LANES)),
            const((2 * GROUP_W, GROUP_W)),
        ],
        out_specs=pl.BlockSpec((n_pos, GROUP_W), lambda n, q: (n, q)),
        out_shape=jax.ShapeDtypeStruct((n_batch * n_pos, BRANCH_W), BF16),
        scratch_shapes=[scratch] * 5,
        compiler_params=_vmem(48 << 20),
        name="fourier_latent",
    )(plain, wa, wc, tw_cos, tw_sin, chan_cs)


def _merge_kernel(x_ref, mod_ref, attn_ref, conv_ref, before_ref, after_ref, four_ref, gate_ref,
                  wconv_ref, wbr_ref, wo_ref, ln2_ref, o_ref, h2_ref, *, seq_len):
    tm, d = x_ref.shape
    i = pl.program_id(0)
    pos0 = (i * tm) % seq_len
    cc_cols = slice(PLAIN_CC * BRANCH_W, (PLAIN_CC + 1) * BRANCH_W)
    cx_cols = slice(PLAIN_CX * BRANCH_W, (PLAIN_CX + 1) * BRANCH_W)
    cb_cols = slice(PLAIN_CB * BRANCH_W, (PLAIN_CB + 1) * BRANCH_W)

    u = conv_ref[:, cc_cols].astype(F32) * conv_ref[:, cx_cols].astype(F32)
    last = slice(SUBLANES - 1, SUBLANES)
    u_before = before_ref[last, cc_cols].astype(F32) * before_ref[last, cx_cols].astype(F32)
    u_after = after_ref[0:1, cc_cols].astype(F32) * after_ref[0:1, cx_cols].astype(F32)
    u_before = jnp.where(pos0 == 0, 0.0, u_before)
    u_after = jnp.where(pos0 + tm == seq_len, 0.0, u_after)
    row = lax.broadcasted_iota(jnp.int32, (tm, 1), 0)
    u_prev = jnp.where(row == 0, u_before, pltpu.roll(u, 1, 0))
    u_next = jnp.where(row == tm - 1, u_after, pltpu.roll(u, tm - 1, 0))
    w = wconv_ref[0]
    conv = (conv_ref[:, cb_cols].astype(F32)
            * (u_prev * w[0:1, :] + u * w[1:2, :] + u_next * w[2:3, :]))

    gate = lambda branch: gate_ref[:, branch * d:(branch + 1) * d].astype(F32)
    mix = (gate(0) * _dot(attn_ref[...], wbr_ref[0, 0])
           + gate(1) * _dot(conv.astype(BF16), wbr_ref[0, 1])
           + gate(2) * _dot(four_ref[...], wbr_ref[0, 2]))
    y = _dot(mix.astype(BF16), wo_ref[0])
    x_new = x_ref[...] + mod_ref[0, 0, 2:3, :] * y
    o_ref[...] = x_new
    h2 = _rms_modulate(x_new, ln2_ref[0], mod_ref[0, 0, 3:4, :], mod_ref[0, 0, 4:5, :])
    h2_ref[...] = h2.astype(BF16)


def _merge_call(x2d, mod, layer, mod_row, attn_o, plain, gates, four_o, w_conv, w_br, w_o,
                ln2_g, *, seq_len):
    m, d = x2d.shape
    tm = 256
    halo_per_tile = tm // SUBLANES
    last_halo = m // SUBLANES - 1
    const = dict(pipeline_mode=pl.Buffered(1))
    assert (PLAIN_CC, PLAIN_CX, PLAIN_CB) == (0, 1, 2)

    return pl.pallas_call(
        functools.partial(_merge_kernel, seq_len=seq_len),
        grid=(m // tm,),
        in_specs=[
            pl.BlockSpec((tm, d), lambda i: (i, 0)),
            pl.BlockSpec((1, 1, 6, d), lambda i: (layer, mod_row(i * tm), 0, 0)),
            pl.BlockSpec((tm, BRANCH_W), lambda i: (i, 0)),
            pl.BlockSpec((tm, 3 * BRANCH_W), lambda i: (i, 0)),
            pl.BlockSpec((SUBLANES, 2 * BRANCH_W),
                         lambda i: (jnp.maximum(i * halo_per_tile - 1, 0), 0)),
            pl.BlockSpec((SUBLANES, 2 * BRANCH_W),
                         lambda i: (jnp.minimum((i + 1) * halo_per_tile, last_halo), 0)),
            pl.BlockSpec((tm, BRANCH_W), lambda i: (i, 0)),
            pl.BlockSpec((tm, N_BRANCH * d), lambda i: (i, 0)),
            pl.BlockSpec((1, CONV_K, BRANCH_W), lambda i: (layer, 0, 0)),
            pl.BlockSpec((1, N_BRANCH, BRANCH_W, d), lambda i: (layer, 0, 0, 0), **const),
            pl.BlockSpec((1, d, d), lambda i: (layer, 0, 0), **const),
            pl.BlockSpec((1, 1, d), lambda i: (layer, 0, 0)),
        ],
        out_specs=[pl.BlockSpec((tm, d), lambda i: (i, 0))] * 2,
        out_shape=[jax.ShapeDtypeStruct((m, d), F32), jax.ShapeDtypeStruct((m, d), BF16)],
        compiler_params=_vmem(56 << 20),
        name="merge",
    )(x2d, mod, attn_o, plain, plain, plain, four_o, gates, w_conv, w_br, w_o,
      ln2_g.reshape(ln2_g.shape[0], 1, d))


def _mlp_kernel(x_ref, h_ref, mod_ref, w1_ref, w2_ref, o_ref, *, out_chunk):
    j = pl.program_id(1)

    @pl.when(j == 0)
    def _():
        o_ref[...] = jnp.zeros_like(o_ref)

    h = h_ref[...]
    half = w1_ref.shape[2] // 2
    up = lambda s: jnp.maximum(_dot(h, w1_ref[0, :, s * half:(s + 1) * half]), 0.0)
    hid = [up(0), up(1)]
    hid2 = [(v * v).astype(BF16) for v in hid]
    for n0 in range(0, o_ref.shape[1], out_chunk):
        cols = slice(n0, n0 + out_chunk)
        down = (_dot(hid2[0], w2_ref[0, 0:half, cols])
                + _dot(hid2[1], w2_ref[0, half:, cols]))
        o_ref[:, cols] += mod_ref[0, 0, 5:6, cols] * down

    @pl.when(j == pl.num_programs(1) - 1)
    def _():
        o_ref[...] += x_ref[...]


def _mlp_call(x2d, h2, mod, layer, mod_row, w1, w2):
    m, d = x2d.shape
    d_ff = w1.shape[2]
    tm, tf = 1024, 512
    steps = d_ff // tf
    x_rows = lambda i, j: (jnp.where(j >= steps // 2, i, jnp.maximum(i - 1, 0)), 0)
    return pl.pallas_call(
        functools.partial(_mlp_kernel, out_chunk=512),
        grid=(m // tm, steps),
        in_specs=[
            pl.BlockSpec((tm, d), x_rows),
            pl.BlockSpec((tm, d), lambda i, j: (i, 0)),
            pl.BlockSpec((1, 1, 6, d), lambda i, j: (layer, mod_row(i * tm), 0, 0)),
            pl.BlockSpec((1, d, tf), lambda i, j: (layer, 0, j)),
            pl.BlockSpec((1, tf, d), lambda i, j: (layer, j, 0)),
        ],
        out_specs=pl.BlockSpec((tm, d), lambda i, j: (i, 0)),
        out_shape=jax.ShapeDtypeStruct((m, d), F32),
        compiler_params=_vmem(56 << 20),
        name="mlp",
    )(x2d, h2, mod, w1, w2)


def _rope_tables(seq_len):
    rows = seq_len // GRID_W
    t_row = jnp.repeat(jnp.arange(rows), GRID_W).astype(F32)
    t_col = jnp.tile(jnp.arange(GRID_W), rows).astype(F32)
    axis_dim = HEAD_DIM // 2
    inv_freq = 1.0 / (ROPE_THETA ** (jnp.arange(0, axis_dim, 2, dtype=F32) / axis_dim))
    ang_r = t_row[:, None] * inv_freq[None, :]
    ang_c = t_col[:, None] * inv_freq[None, :]
    cos = jnp.concatenate([jnp.cos(ang_r)] * 2 + [jnp.cos(ang_c)] * 2, axis=-1)
    sin = jnp.concatenate([-jnp.sin(ang_r), jnp.sin(ang_r), -jnp.sin(ang_c), jnp.sin(ang_c)],
                          axis=-1)
    return jnp.tile(cos, (1, LANES // HEAD_DIM)), jnp.tile(sin, (1, LANES // HEAD_DIM))


def _dft_cos_sin(rows, cols, period, scale):
    r = jnp.arange(rows, dtype=jnp.int32)
    c = jnp.arange(cols, dtype=jnp.int32)
    ang = ((r[:, None] * c[None, :]) % period).astype(F32) * (2.0 * math.pi / period)
    return jnp.cos(ang) * scale, jnp.sin(ang) * scale


def kernel(x, c, ctx, c_ctx, ln1_g, ln2_g, w_ada, b_ada, w_in, qn_g, kn_g, lam_q, lam_k,
           subln_g, w_conv, w_br, w_o, w1, w2):
    n_batch, seq, d = x.shape
    ctx_len = ctx.shape[1]
    depth = w_ada.shape[0]
    assert seq == FFT_R * FFT_R and ctx_len == 256 and n_batch <= CTX_MOD_ROW and d == D_MODEL

    c_rows = jnp.zeros((MOD_ROWS, d), F32).at[:n_batch].set(c).at[CTX_MOD_ROW].set(c_ctx)
    mod = _mod_call(c_rows, w_ada, b_ada).reshape(depth, MOD_ROWS, 6, d)

    cos_l, sin_l = _rope_tables(seq)
    cos_c = jnp.ones((n_batch * ctx_len, LANES), F32)
    sin_c = jnp.zeros((n_batch * ctx_len, LANES), F32)
    qg = jnp.tile(qn_g, (1, MXU_W // HEAD_DIM)).reshape(depth, 1, MXU_W)
    kg = jnp.tile(kn_g, (1, MXU_W // HEAD_DIM)).reshape(depth, 1, MXU_W)

    ch_c, ch_s = _dft_cos_sin(GROUP_W, GROUP_W, GROUP_W, GROUP_W ** -0.5)
    chan_cos, chan_sin = ch_c.astype(BF16), ch_s.astype(BF16)
    chan_cs = jnp.concatenate([chan_cos, chan_sin], axis=0)
    pc_c, pc_s = _dft_cos_sin(ctx_len, ctx_len, ctx_len, ctx_len ** -0.5)
    pos_c = jnp.concatenate([pc_c, -pc_s], axis=1).astype(BF16)
    r_c, r_s = _dft_cos_sin(FFT_R, FFT_R, FFT_R, FFT_R ** -0.5)
    eye2 = jnp.eye(2, dtype=F32)
    wa = jnp.kron(jnp.concatenate([r_c, -r_s], axis=0), eye2).astype(BF16)
    wc = jnp.concatenate([jnp.concatenate([r_c, r_s], axis=1),
                          jnp.concatenate([-r_s, r_c], axis=1)], axis=0)
    wc = jnp.einsum("odib,pq->odpqib", wc.reshape(2, FFT_R, 2, FFT_R), eye2)
    wc = wc.reshape(4 * FFT_R, 4 * FFT_R).astype(BF16)
    tw_c, tw_s = _dft_cos_sin(FFT_R, FFT_R, seq, 1.0)
    pair_order = lambda t: jnp.broadcast_to(
        t.reshape(FFT_R // 2, 2, FFT_R).transpose(0, 2, 1).reshape(seq, 1), (seq, LANES))
    tw_cos, tw_sin = pair_order(tw_c), pair_order(tw_s)

    w_br_b = w_br.astype(BF16)
    w_o_b = w_o.astype(BF16)
    w1_b = w1.astype(BF16)
    w2_b = w2.astype(BF16)

    lat_row = lambda row: row // seq
    ctx_row = lambda row: CTX_MOD_ROW
    heads = COL_TILE // LANES

    xl = x.reshape(n_batch * seq, d)
    xc = ctx.reshape(n_batch * ctx_len, d)

    for l in range(depth):
        last = l == depth - 1
        lam_init = 0.8 - 0.6 * math.exp(-0.3 * l)

        hc = _normmod_call(xc, mod, l, ctx_row, ln1_g)
        hl = _normmod_call(xl, mod, l, lat_row, ln1_g)
        qk_c = _proj_call("qk", hc, w_in, l, QK_TILES, (qg, kg, cos_c, sin_c))
        qk_l = _proj_call("qk", hl, w_in, l, QK_TILES, (qg, kg, cos_l, sin_l))
        plain_c = _proj_call("plain", hc, w_in, l, (TILE_V,) if last else PLAIN_TILES)
        plain_l = _proj_call("plain", hl, w_in, l, PLAIN_TILES)
        gates_l = _proj_call("gate", hl, w_in, l, GATE_TILES)

        kv_ctx = (qk_c, heads, plain_c, (0 if last else PLAIN_V) * heads, ctx_len)
        kv_lat = (qk_l, heads, plain_l, PLAIN_V * heads, seq)
        attn_l = _attn_call(lam_q, lam_k, subln_g, l, qk_l, 0, [kv_ctx, kv_lat],
                            n_batch=n_batch, q_len=seq, lam_init=lam_init)
        four_l = _fourier_latent_call(plain_l, wa, wc, tw_cos, tw_sin, chan_cs, n_batch=n_batch)
        xl, h2l = _merge_call(xl, mod, l, lat_row, attn_l, plain_l, gates_l, four_l, w_conv,
                              w_br_b, w_o_b, ln2_g, seq_len=seq)
        xl = _mlp_call(xl, h2l, mod, l, lat_row, w1_b, w2_b)

        if not last:
            gates_c = _proj_call("gate", hc, w_in, l, GATE_TILES)
            attn_c = _attn_call(lam_q, lam_k, subln_g, l, qk_c, 0, [kv_ctx],
                                n_batch=n_batch, q_len=ctx_len, lam_init=lam_init)
            four_c = _fourier_dense_call(plain_c, PLAIN_F, chan_cos, chan_sin, pos_c,
                                         n_batch=n_batch, n_pos=ctx_len)
            xc, h2c = _merge_call(xc, mod, l, ctx_row, attn_c, plain_c, gates_c, four_c, w_conv,
                                  w_br_b, w_o_b, ln2_g, seq_len=ctx_len)
            xc = _mlp_call(xc, h2c, mod, l, ctx_row, w1_b, w2_b)

    return xl.reshape(n_batch, seq, d)
```

```python
import functools
import math

import jax
import jax.numpy as jnp
from jax import lax
from jax.experimental import pallas as pl
from jax.experimental.pallas import tpu as pltpu

D_MODEL = 2048
GRID_W = 64
N_HEADS = 8
HEAD_DIM = 64
V_HEAD_DIM = 128
BRANCH_W = 1024
FOURIER_GROUPS = 4
GROUP_W = BRANCH_W // FOURIER_GROUPS
N_BRANCH = 3
CONV_K = 3
EPS = 1e-6
ROPE_THETA = 10000.0

COL_TILE = 1024
TILE_Q, TILE_K, TILE_V, TILE_CB, TILE_CC, TILE_CX, TILE_F, TILE_G = range(8)
QK_TILES = (TILE_Q, TILE_K)
PLAIN_TILES = (TILE_CC, TILE_CX, TILE_CB, TILE_V, TILE_F)
PLAIN_CC, PLAIN_CX, PLAIN_CB, PLAIN_V, PLAIN_F = range(5)
GATE_TILES = tuple(range(TILE_G, TILE_G + N_BRANCH * D_MODEL // COL_TILE))

LANES = 128
SUBLANES = 8
MXU_W = 256
ONES_ROWS = 16
LOG2_E = math.log2(math.e)
MOD_ROWS = 8
CTX_MOD_ROW = 4
FFT_R = 64

F32 = jnp.float32
BF16 = jnp.bfloat16


def _vmem(nbytes):
    return pltpu.CompilerParams(vmem_limit_bytes=nbytes)


def _sigmoid(x):
    return 1.0 / (1.0 + jnp.exp(-x))


def _rms_modulate(x, g, shift, scale):
    y = x * lax.rsqrt(jnp.mean(x * x, axis=-1, keepdims=True) + EPS) * g
    return y * (1.0 + scale) + shift


def _dot(a, b):
    return jnp.dot(a, b, preferred_element_type=F32)


def _dot_t(a, b):
    return lax.dot_general(a, b, (((1,), (1,)), ((), ())), preferred_element_type=F32)


def _mod_kernel(c_ref, w_ref, b_ref, o_ref):
    c = c_ref[...]
    cs = c * _sigmoid(c)
    o_ref[0] = jnp.dot(cs, w_ref[0], preferred_element_type=F32,
                       precision=lax.Precision.HIGHEST) + b_ref[0]


def _mod_call(c_rows, w_ada, b_ada):
    depth, d, n = w_ada.shape
    tn = 1536
    return pl.pallas_call(
        _mod_kernel,
        grid=(depth, n // tn),
        in_specs=[
            pl.BlockSpec((MOD_ROWS, d), lambda l, j: (0, 0)),
            pl.BlockSpec((1, d, tn), lambda l, j: (l, 0, j)),
            pl.BlockSpec((1, 1, tn), lambda l, j: (l, 0, j)),
        ],
        out_specs=pl.BlockSpec((1, MOD_ROWS, tn), lambda l, j: (l, 0, j)),
        out_shape=jax.ShapeDtypeStruct((depth, MOD_ROWS, n), F32),
        compiler_params=_vmem(40 << 20),
        name="mod",
    )(c_rows, w_ada, b_ada.reshape(depth, 1, n))


def _normmod_kernel(x_ref, mod_ref, g_ref, o_ref):
    h = _rms_modulate(x_ref[...], g_ref[0], mod_ref[0, 0, 0:1, :], mod_ref[0, 0, 1:2, :])
    o_ref[...] = h.astype(BF16)


def _normmod_call(x2d, mod, layer, mod_row, ln_g):
    m, d = x2d.shape
    tm = 512
    return pl.pallas_call(
        _normmod_kernel,
        grid=(m // tm,),
        in_specs=[
            pl.BlockSpec((tm, d), lambda i: (i, 0)),
            pl.BlockSpec((1, 1, 6, d), lambda i: (layer, mod_row(i * tm), 0, 0)),
            pl.BlockSpec((1, 1, d), lambda i: (layer, 0, 0)),
        ],
        out_specs=pl.BlockSpec((tm, d), lambda i: (i, 0)),
        out_shape=jax.ShapeDtypeStruct((m, d), BF16),
        compiler_params=_vmem(32 << 20),
        name="normmod",
    )(x2d, mod, ln_g.reshape(ln_g.shape[0], 1, d))


def _cast_weights(w_ref, wb_ref):
    @pl.when(pl.program_id(1) == 0)
    def _():
        wb_ref[...] = w_ref[0].astype(BF16)


def _proj_plain_kernel(h_ref, w_ref, o_ref, wb_ref):
    _cast_weights(w_ref, wb_ref)
    o_ref[...] = _dot(h_ref[...], wb_ref[...]).astype(BF16)


def _proj_gate_kernel(h_ref, w_ref, o_ref, wb_ref):
    _cast_weights(w_ref, wb_ref)
    acc = _dot(h_ref[...], wb_ref[...])
    o_ref[...] = (0.5 * jnp.tanh(0.5 * acc) + 0.5).astype(BF16)


def _proj_qk_kernel(h_ref, w_ref, qg_ref, kg_ref, cos_ref, sin_ref, o_ref, wb_ref):
    _cast_weights(w_ref, wb_ref)
    h = h_ref[...]
    n_chunks = COL_TILE // MXU_W
    project = lambda t: _dot(h, wb_ref[:, t * MXU_W:(t + 1) * MXU_W])
    is_q = pl.program_id(0) == TILE_Q
    gain = jnp.where(is_q, qg_ref[0] * (LOG2_E * HEAD_DIM ** -0.5), kg_ref[0])
    cos = cos_ref[...]
    sin = sin_ref[...]
    lane = lax.broadcasted_iota(jnp.int32, (1, LANES), 1)
    first_half = (lane % (HEAD_DIM // 2)) < (HEAD_DIM // 4)
    gsum = (lax.broadcasted_iota(jnp.int32, (MXU_W, MXU_W), 0) // HEAD_DIM
            == lax.broadcasted_iota(jnp.int32, (MXU_W, MXU_W), 1) // HEAD_DIM).astype(BF16)
    a_next = project(0)
    for t in range(n_chunks):
        a = a_next
        if t + 1 < n_chunks:
            a_next = project(t + 1)
        ssq = _dot((a * a).astype(BF16), gsum)
        y = a * lax.rsqrt(ssq * (1.0 / HEAD_DIM) + EPS) * gain
        for u in range(MXU_W // LANES):
            yh = y[:, u * LANES:(u + 1) * LANES]
            partner = jnp.where(first_half,
                                pltpu.roll(yh, LANES - HEAD_DIM // 4, 1),
                                pltpu.roll(yh, HEAD_DIM // 4, 1))
            c0 = t * MXU_W + u * LANES
            o_ref[:, c0:c0 + LANES] = (yh * cos + partner * sin).astype(BF16)


def _proj_call(kind, h, w_in, layer, w_tiles, extra=()):
    m, d = h.shape
    tm = 1024
    n_tiles = len(w_tiles)

    def w_tile(j):
        tile = w_tiles[0]
        for idx in range(1, n_tiles):
            tile = jnp.where(j == idx, w_tiles[idx], tile)
        return tile

    in_specs = [
        pl.BlockSpec((tm, d), lambda j, i: (i, 0)),
        pl.BlockSpec((1, d, COL_TILE), lambda j, i: (layer, 0, w_tile(j))),
    ]
    args = [h, w_in]
    if kind == "qk":
        qg, kg, cos_t, sin_t = extra
        seq_tiles = cos_t.shape[0] // tm
        in_specs += [
            pl.BlockSpec((1, 1, MXU_W), lambda j, i: (layer, 0, 0)),
            pl.BlockSpec((1, 1, MXU_W), lambda j, i: (layer, 0, 0)),
            pl.BlockSpec((tm, LANES), lambda j, i: (i % seq_tiles, 0)),
            pl.BlockSpec((tm, LANES), lambda j, i: (i % seq_tiles, 0)),
        ]
        args += [qg, kg, cos_t, sin_t]
    body = {"qk": _proj_qk_kernel, "plain": _proj_plain_kernel, "gate": _proj_gate_kernel}[kind]
    return pl.pallas_call(
        body,
        grid=(n_tiles, m // tm),
        in_specs=in_specs,
        out_specs=pl.BlockSpec((tm, COL_TILE), lambda j, i: (i, j)),
        out_shape=jax.ShapeDtypeStruct((m, n_tiles * COL_TILE), BF16),
        scratch_shapes=[pltpu.VMEM((d, COL_TILE), BF16)],
        compiler_params=_vmem(52 << 20),
        name="proj_" + kind,
    )(*args)


def _attn_kernel(lq_ref, lk_ref, sg_ref, q_ref, *refs, lam_init, pieces, key_block):
    n = len(pieces)
    k_refs, v_refs = refs[0:2 * n:2], refs[1:2 * n:2]
    o_ref, vt_ref, kmax_ref = refs[2 * n:]
    tq = q_ref.shape[0]
    lane = lax.broadcasted_iota(jnp.int32, (1, LANES), 1)

    @pl.when(pl.program_id(2) == 0)
    def _():
        group = lax.broadcasted_iota(jnp.int32, (LANES, LANES), 0) // HEAD_DIM
        gsum = (group == lane // HEAD_DIM).astype(BF16)
        kmax = None
        off = 0
        for t in range(n):
            rows = pieces[t]
            vt_ref[0:V_HEAD_DIM, off:off + rows] = v_refs[t][...].astype(F32).T.astype(BF16)
            kf = k_refs[t][...].astype(F32)
            norms = _dot((kf * kf).astype(BF16), gsum)
            piece_max = jnp.max(norms, axis=0, keepdims=True)
            kmax = piece_max if kmax is None else jnp.maximum(kmax, piece_max)
            off += rows
        vt_ref[V_HEAD_DIM:, :] = jnp.ones((ONES_ROWS, off), BF16)
        kmax_ref[...] = kmax

    e = jnp.exp(jnp.sum(lq_ref[0] * lk_ref[0], axis=-1, keepdims=True))
    lam = e[0:1, :] - e[1:2, :] + lam_init

    q = q_ref[...]
    zero = jnp.zeros_like(q)
    lo = lane < HEAD_DIM
    q_both = jnp.concatenate([jnp.where(lo, q, zero), jnp.where(lo, zero, q)], axis=0)

    qf = q.astype(F32)
    sel = (lax.broadcasted_iota(jnp.int32, (SUBLANES, LANES), 0) == lane // HEAD_DIM).astype(BF16)
    q_norms = _dot_t(sel, (qf * qf).astype(BF16))
    kmax = kmax_ref[...]
    bounds = jnp.concatenate(
        [jnp.sqrt(q_norms[c:c + 1, :] * kmax[:, c * HEAD_DIM:c * HEAD_DIM + 1]) * 1.01 + 1e-6
         for c in range(2)], axis=1)

    def key_blocks():
        off = 0
        for t in range(n):
            for s in range(0, pieces[t], key_block):
                size = min(key_block, pieces[t] - s)
                yield k_refs[t], s, off + s, size
            off += pieces[t]

    def accumulate(shifts):
        blocks = list(key_blocks())
        scores = lambda blk: _dot_t(blk[0][blk[1]:blk[1] + blk[3], :], q_both)
        acc = jnp.zeros((V_HEAD_DIM + ONES_ROWS, 2 * tq), F32)
        st_next = scores(blocks[0])
        for idx, (_, _, pos, size) in enumerate(blocks):
            st = st_next
            if idx + 1 < len(blocks):
                st_next = scores(blocks[idx + 1])
            acc = acc + _dot(vt_ref[:, pos:pos + size], jnp.exp2(st - shifts).astype(BF16))
        return acc

    def finish(acc):
        sums = acc[V_HEAD_DIM:V_HEAD_DIM + 1, :]
        ot = (acc[0:V_HEAD_DIM, 0:tq] * (1.0 / sums[:, 0:tq])
              - acc[0:V_HEAD_DIM, tq:] * (lam / sums[:, tq:]))
        yt = ot * lax.rsqrt(jnp.mean(ot * ot, axis=0, keepdims=True) + EPS)
        o_ref[...] = (yt.T * (sg_ref[0] * (1.0 - lam_init))).astype(BF16)

    acc = accumulate(bounds)
    finish(acc)

    @pl.when(jnp.logical_not(jnp.min(acc[V_HEAD_DIM:V_HEAD_DIM + 1, :]) >= 1e-30))
    def _():
        m = None
        for k_ref, s, _, size in key_blocks():
            bm = jnp.max(_dot_t(k_ref[s:s + size, :], q_both), axis=0, keepdims=True)
            m = bm if m is None else jnp.maximum(m, bm)
        finish(accumulate(m))


def _attn_call(lam_q, lam_k, subln_g, layer, q_arr, q_tile0, kv_list, *, n_batch, q_len,
               lam_init):
    tq = min(1024, q_len)
    q_tiles = q_len // tq
    in_specs = [
        pl.BlockSpec((1, 2, HEAD_DIM), lambda b, h, i: (layer, 0, 0)),
        pl.BlockSpec((1, 2, HEAD_DIM), lambda b, h, i: (layer, 0, 0)),
        pl.BlockSpec((1, 1, V_HEAD_DIM), lambda b, h, i: (layer, 0, 0)),
        pl.BlockSpec((tq, LANES), lambda b, h, i: (b * q_tiles + i, q_tile0 + h)),
    ]
    args = [lam_q, lam_k, subln_g.reshape(subln_g.shape[0], 1, V_HEAD_DIM), q_arr]
    for k_arr, k0, v_arr, v0, rows in kv_list:
        in_specs.append(pl.BlockSpec((rows, LANES), lambda b, h, i, k0=k0: (b, k0 + h)))
        in_specs.append(pl.BlockSpec((rows, LANES), lambda b, h, i, v0=v0: (b, v0 + h)))
        args += [k_arr, v_arr]
    pieces = tuple(kv[4] for kv in kv_list)
    return pl.pallas_call(
        functools.partial(_attn_kernel, lam_init=lam_init, pieces=pieces, key_block=256),
        grid=(n_batch, N_HEADS, q_tiles),
        in_specs=in_specs,
        out_specs=pl.BlockSpec((tq, LANES), lambda b, h, i: (b * q_tiles + i, h)),
        out_shape=jax.ShapeDtypeStruct((n_batch * q_len, N_HEADS * V_HEAD_DIM), BF16),
        scratch_shapes=[pltpu.VMEM((V_HEAD_DIM + ONES_ROWS, sum(pieces)), BF16),
                        pltpu.VMEM((1, LANES), F32)],
        compiler_params=_vmem(52 << 20),
        name="attn",
    )(*args)


def _fourier_dense_kernel(x_ref, cc_ref, sc_ref, pos_ref, o_ref, y_ref, *, n_pos):
    @pl.when(pl.program_id(1) == 0)
    def _():
        cc = cc_ref[...]
        sc = sc_ref[...]
        for g in range(FOURIER_GROUPS):
            xg = x_ref[:, g * GROUP_W:(g + 1) * GROUP_W]
            y_ref[0:n_pos, g * GROUP_W:(g + 1) * GROUP_W] = _dot(xg, cc).astype(BF16)
            y_ref[n_pos:2 * n_pos, g * GROUP_W:(g + 1) * GROUP_W] = _dot(xg, sc).astype(BF16)

    o_ref[...] = _dot(pos_ref[...], y_ref[...]).astype(BF16)


def _fourier_dense_call(p_arr, f_tile, chan_cos, chan_sin, pos_mat, *, n_batch, n_pos):
    tm = min(256, n_pos)
    row_tiles = n_pos // tm
    return pl.pallas_call(
        functools.partial(_fourier_dense_kernel, n_pos=n_pos),
        grid=(n_batch, row_tiles),
        in_specs=[
            pl.BlockSpec((n_pos, BRANCH_W), lambda b, i: (b, f_tile),
                         pipeline_mode=pl.Buffered(1)),
            pl.BlockSpec((GROUP_W, GROUP_W), lambda b, i: (0, 0)),
            pl.BlockSpec((GROUP_W, GROUP_W), lambda b, i: (0, 0)),
            pl.BlockSpec((tm, 2 * n_pos), lambda b, i: (i, 0)),
        ],
        out_specs=pl.BlockSpec((tm, BRANCH_W), lambda b, i: (b * row_tiles + i, 0)),
        out_shape=jax.ShapeDtypeStruct((n_batch * n_pos, BRANCH_W), BF16),
        scratch_shapes=[pltpu.VMEM((2 * n_pos, BRANCH_W), BF16)],
        compiler_params=_vmem(32 << 20),
        name="fourier_dense",
    )(p_arr, chan_cos, chan_sin, pos_mat)


def _fourier_latent_kernel(x_ref, wa_ref, wc_ref, twc_ref, tws_ref, cs_ref, o_ref,
                           xs_ref, ar_ref, ai_ref, zr_ref, zi_ref):
    tiles = GROUP_W // LANES
    half = FFT_R // 2
    lane_tile = lambda v, t: v[:, t * LANES:(t + 1) * LANES]
    pack = lambda v: pltpu.bitcast(v.astype(BF16), jnp.int32)

    def gather(ref, rows):
        words = jnp.concatenate([ref[t, rows, :] for t in range(tiles)], axis=1)
        return pltpu.bitcast(words, BF16)

    def scatter(ref, rows, v):
        words = pack(v)
        for t in range(tiles):
            ref[t, rows, :] = lane_tile(words, t)

    for t in range(tiles):
        xs_ref[t] = pltpu.bitcast(lane_tile(x_ref, t), jnp.int32)
    wa = wa_ref[...]
    wc = wc_ref[...]

    def stage1(beta, carry):
        xb = gather(xs_ref, pl.ds(beta, FFT_R, stride=half))
        a = _dot(wa, xb)
        tw_rows = pl.ds(pl.multiple_of(beta * 2 * FFT_R, 2 * FFT_R), 2 * FFT_R)
        tc = twc_ref[tw_rows, :]
        ts = tws_ref[tw_rows, :]
        re, im = a[0:2 * FFT_R], a[2 * FFT_R:]
        tc = jnp.concatenate([tc] * tiles, axis=1)
        ts = jnp.concatenate([ts] * tiles, axis=1)
        rows = pl.ds(pl.multiple_of(beta * FFT_R, FFT_R), FFT_R)
        scatter(ar_ref, rows, re * tc + im * ts)
        scatter(ai_ref, rows, im * tc - re * ts)
        return carry

    lax.fori_loop(0, half, stage1, 0, unroll=4)

    def stage2(gamma, carry):
        parts = []
        for p in range(2):
            rows = pl.ds(2 * gamma + p, half, stride=FFT_R)
            parts += [gather(ar_ref, rows), gather(ai_ref, rows)]
        z = _dot(wc, jnp.concatenate(parts, axis=0))
        rows = pl.ds(gamma, FFT_R, stride=half)
        scatter(zr_ref, rows, z[0:2 * FFT_R])
        scatter(zi_ref, rows, z[2 * FFT_R:])
        return carry

    lax.fori_loop(0, half, stage2, 0, unroll=4)

    cs = cs_ref[...]
    block = 512
    for r in range(0, x_ref.shape[0], block):
        rows = slice(r // 2, (r + block) // 2)
        z = jnp.concatenate([gather(zr_ref, rows), gather(zi_ref, rows)], axis=1)
        o_ref[r:r + block, :] = _dot(z, cs).astype(BF16)


def _fourier_latent_call(plain, wa, wc, tw_cos, tw_sin, chan_cs, *, n_batch):
    n_pos = FFT_R * FFT_R
    tiles = GROUP_W // LANES
    const = lambda shape: pl.BlockSpec(shape, lambda n, q: (0,) * len(shape))
    scratch = pltpu.VMEM((tiles, n_pos // 2, LANES), jnp.int32)
    return pl.pallas_call(
        _fourier_latent_kernel,
        grid=(n_batch, FOURIER_GROUPS),
        in_specs=[
            pl.BlockSpec((n_pos, GROUP_W), lambda n, q: (n, PLAIN_F * FOURIER_GROUPS + q)),
            const((4 * FFT_R, 2 * FFT_R)),
            const((4 * FFT_R, 4 * FFT_R)),
            const((n_pos, LANES)),
            const((n_pos, LANES)),
            const((2 * GROUP_W, GROUP_W)),
        ],
        out_specs=pl.BlockSpec((n_pos, GROUP_W), lambda n, q: (n, q)),
        out_shape=jax.ShapeDtypeStruct((n_batch * n_pos, BRANCH_W), BF16),
        scratch_shapes=[scratch] * 5,
        compiler_params=_vmem(48 << 20),
        name="fourier_latent",
    )(plain, wa, wc, tw_cos, tw_sin, chan_cs)


def _merge_kernel(x_ref, mod_ref, attn_ref, conv_ref, before_ref, after_ref, four_ref, gate_ref,
                  wconv_ref, wbr_ref, wo_ref, ln2_ref, o_ref, h2_ref, *, seq_len):
    tm, d = x_ref.shape
    i = pl.program_id(0)
    pos0 = (i * tm) % seq_len
    cc_cols = slice(PLAIN_CC * BRANCH_W, (PLAIN_CC + 1) * BRANCH_W)
    cx_cols = slice(PLAIN_CX * BRANCH_W, (PLAIN_CX + 1) * BRANCH_W)
    cb_cols = slice(PLAIN_CB * BRANCH_W, (PLAIN_CB + 1) * BRANCH_W)

    u = conv_ref[:, cc_cols].astype(F32) * conv_ref[:, cx_cols].astype(F32)
    last = slice(SUBLANES - 1, SUBLANES)
    u_before = before_ref[last, cc_cols].astype(F32) * before_ref[last, cx_cols].astype(F32)
    u_after = after_ref[0:1, cc_cols].astype(F32) * after_ref[0:1, cx_cols].astype(F32)
    u_before = jnp.where(pos0 == 0, 0.0, u_before)
    u_after = jnp.where(pos0 + tm == seq_len, 0.0, u_after)
    row = lax.broadcasted_iota(jnp.int32, (tm, 1), 0)
    u_prev = jnp.where(row == 0, u_before, pltpu.roll(u, 1, 0))
    u_next = jnp.where(row == tm - 1, u_after, pltpu.roll(u, tm - 1, 0))
    w = wconv_ref[0]
    conv = (conv_ref[:, cb_cols].astype(F32)
            * (u_prev * w[0:1, :] + u * w[1:2, :] + u_next * w[2:3, :]))

    gate = lambda branch: gate_ref[:, branch * d:(branch + 1) * d].astype(F32)
    mix = (gate(0) * _dot(attn_ref[...], wbr_ref[0, 0])
           + gate(1) * _dot(conv.astype(BF16), wbr_ref[0, 1])
           + gate(2) * _dot(four_ref[...], wbr_ref[0, 2]))
    y = _dot(mix.astype(BF16), wo_ref[0])
    x_new = x_ref[...] + mod_ref[0, 0, 2:3, :] * y
    o_ref[...] = x_new
    h2 = _rms_modulate(x_new, ln2_ref[0], mod_ref[0, 0, 3:4, :], mod_ref[0, 0, 4:5, :])
    h2_ref[...] = h2.astype(BF16)


def _merge_call(x2d, mod, layer, mod_row, attn_o, plain, gates, four_o, w_conv, w_br, w_o,
                ln2_g, *, seq_len):
    m, d = x2d.shape
    tm = 256
    halo_per_tile = tm // SUBLANES
    last_halo = m // SUBLANES - 1
    const = dict(pipeline_mode=pl.Buffered(1))
    assert (PLAIN_CC, PLAIN_CX, PLAIN_CB) == (0, 1, 2)

    return pl.pallas_call(
        functools.partial(_merge_kernel, seq_len=seq_len),
        grid=(m // tm,),
        in_specs=[
            pl.BlockSpec((tm, d), lambda i: (i, 0)),
            pl.BlockSpec((1, 1, 6, d), lambda i: (layer, mod_row(i * tm), 0, 0)),
            pl.BlockSpec((tm, BRANCH_W), lambda i: (i, 0)),
            pl.BlockSpec((tm, 3 * BRANCH_W), lambda i: (i, 0)),
            pl.BlockSpec((SUBLANES, 2 * BRANCH_W),
                         lambda i: (jnp.maximum(i * halo_per_tile - 1, 0), 0)),
            pl.BlockSpec((SUBLANES, 2 * BRANCH_W),
                         lambda i: (jnp.minimum((i + 1) * halo_per_tile, last_halo), 0)),
            pl.BlockSpec((tm, BRANCH_W), lambda i: (i, 0)),
            pl.BlockSpec((tm, N_BRANCH * d), lambda i: (i, 0)),
            pl.BlockSpec((1, CONV_K, BRANCH_W), lambda i: (layer, 0, 0)),
            pl.BlockSpec((1, N_BRANCH, BRANCH_W, d), lambda i: (layer, 0, 0, 0), **const),
            pl.BlockSpec((1, d, d), lambda i: (layer, 0, 0), **const),
            pl.BlockSpec((1, 1, d), lambda i: (layer, 0, 0)),
        ],
        out_specs=[pl.BlockSpec((tm, d), lambda i: (i, 0))] * 2,
        out_shape=[jax.ShapeDtypeStruct((m, d), F32), jax.ShapeDtypeStruct((m, d), BF16)],
        compiler_params=_vmem(56 << 20),
        name="merge",
    )(x2d, mod, attn_o, plain, plain, plain, four_o, gates, w_conv, w_br, w_o,
      ln2_g.reshape(ln2_g.shape[0], 1, d))


def _mlp_kernel(x_ref, h_ref, mod_ref, w1_ref, w2_ref, o_ref, *, out_chunk, x_parts):
    j = pl.program_id(1)

    @pl.when(j == 0)
    def _():
        o_ref[...] = jnp.zeros_like(o_ref)

    h = h_ref[...]
    half = w1_ref.shape[2] // 2
    up = lambda s: jnp.maximum(_dot(h, w1_ref[0, :, s * half:(s + 1) * half]), 0.0)
    hid = [up(0), up(1)]
    hid2 = [(v * v).astype(BF16) for v in hid]
    for n0 in range(0, o_ref.shape[1], out_chunk):
        cols = slice(n0, n0 + out_chunk)
        down = (_dot(hid2[0], w2_ref[0, 0:half, cols])
                + _dot(hid2[1], w2_ref[0, half:, cols]))
        o_ref[:, cols] += mod_ref[0, 0, 5:6, cols] * down

    part = j - (pl.num_programs(1) - x_parts)

    @pl.when(part >= 0)
    def _():
        part_rows = x_ref.shape[0]
        rows = pl.ds(pl.multiple_of(part * part_rows, part_rows), part_rows)
        o_ref[rows, :] += x_ref[...]


def _mlp_call(x2d, h2, mod, layer, mod_row, w1, w2):
    m, d = x2d.shape
    d_ff = w1.shape[2]
    tm, tf = 1024, 1024
    steps = d_ff // tf
    x_parts = 4

    def x_rows(i, j):
        part = j - (steps - x_parts)
        return jnp.where(part >= 0, i * x_parts + part, jnp.maximum(i * x_parts - 1, 0)), 0

    return pl.pallas_call(
        functools.partial(_mlp_kernel, out_chunk=512, x_parts=x_parts),
        grid=(m // tm, steps),
        in_specs=[
            pl.BlockSpec((tm // x_parts, d), x_rows),
            pl.BlockSpec((tm, d), lambda i, j: (i, 0)),
            pl.BlockSpec((1, 1, 6, d), lambda i, j: (layer, mod_row(i * tm), 0, 0)),
            pl.BlockSpec((1, d, tf), lambda i, j: (layer, 0, j)),
            pl.BlockSpec((1, tf, d), lambda i, j: (layer, j, 0)),
        ],
        out_specs=pl.BlockSpec((tm, d), lambda i, j: (i, 0)),
        out_shape=jax.ShapeDtypeStruct((m, d), F32),
        compiler_params=_vmem(56 << 20),
        name="mlp",
    )(x2d, h2, mod, w1, w2)


def _rope_tables(seq_len):
    rows = seq_len // GRID_W
    t_row = jnp.repeat(jnp.arange(rows), GRID_W).astype(F32)
    t_col = jnp.tile(jnp.arange(GRID_W), rows).astype(F32)
    axis_dim = HEAD_DIM // 2
    inv_freq = 1.0 / (ROPE_THETA ** (jnp.arange(0, axis_dim, 2, dtype=F32) / axis_dim))
    ang_r = t_row[:, None] * inv_freq[None, :]
    ang_c = t_col[:, None] * inv_freq[None, :]
    cos = jnp.concatenate([jnp.cos(ang_r)] * 2 + [jnp.cos(ang_c)] * 2, axis=-1)
    sin = jnp.concatenate([-jnp.sin(ang_r), jnp.sin(ang_r), -jnp.sin(ang_c), jnp.sin(ang_c)],
                          axis=-1)
    return jnp.tile(cos, (1, LANES // HEAD_DIM)), jnp.tile(sin, (1, LANES // HEAD_DIM))


def _dft_cos_sin(rows, cols, period, scale):
    r = jnp.arange(rows, dtype=jnp.int32)
    c = jnp.arange(cols, dtype=jnp.int32)
    ang = ((r[:, None] * c[None, :]) % period).astype(F32) * (2.0 * math.pi / period)
    return jnp.cos(ang) * scale, jnp.sin(ang) * scale


def kernel(x, c, ctx, c_ctx, ln1_g, ln2_g, w_ada, b_ada, w_in, qn_g, kn_g, lam_q, lam_k,
           subln_g, w_conv, w_br, w_o, w1, w2):
    n_batch, seq, d = x.shape
    ctx_len = ctx.shape[1]
    depth = w_ada.shape[0]
    assert seq == FFT_R * FFT_R and ctx_len == 256 and n_batch <= CTX_MOD_ROW and d == D_MODEL

    c_rows = jnp.zeros((MOD_ROWS, d), F32).at[:n_batch].set(c).at[CTX_MOD_ROW].set(c_ctx)
    mod = _mod_call(c_rows, w_ada, b_ada).reshape(depth, MOD_ROWS, 6, d)

    cos_l, sin_l = _rope_tables(seq)
    cos_c = jnp.ones((n_batch * ctx_len, LANES), F32)
    sin_c = jnp.zeros((n_batch * ctx_len, LANES), F32)
    qg = jnp.tile(qn_g, (1, MXU_W // HEAD_DIM)).reshape(depth, 1, MXU_W)
    kg = jnp.tile(kn_g, (1, MXU_W // HEAD_DIM)).reshape(depth, 1, MXU_W)

    ch_c, ch_s = _dft_cos_sin(GROUP_W, GROUP_W, GROUP_W, GROUP_W ** -0.5)
    chan_cos, chan_sin = ch_c.astype(BF16), ch_s.astype(BF16)
    chan_cs = jnp.concatenate([chan_cos, chan_sin], axis=0)
    pc_c, pc_s = _dft_cos_sin(ctx_len, ctx_len, ctx_len, ctx_len ** -0.5)
    pos_c = jnp.concatenate([pc_c, -pc_s], axis=1).astype(BF16)
    r_c, r_s = _dft_cos_sin(FFT_R, FFT_R, FFT_R, FFT_R ** -0.5)
    eye2 = jnp.eye(2, dtype=F32)
    wa = jnp.kron(jnp.concatenate([r_c, -r_s], axis=0), eye2).astype(BF16)
    wc = jnp.concatenate([jnp.concatenate([r_c, r_s], axis=1),
                          jnp.concatenate([-r_s, r_c], axis=1)], axis=0)
    wc = jnp.einsum("odib,pq->odpqib", wc.reshape(2, FFT_R, 2, FFT_R), eye2)
    wc = wc.reshape(4 * FFT_R, 4 * FFT_R).astype(BF16)
    tw_c, tw_s = _dft_cos_sin(FFT_R, FFT_R, seq, 1.0)
    pair_order = lambda t: jnp.broadcast_to(
        t.reshape(FFT_R // 2, 2, FFT_R).transpose(0, 2, 1).reshape(seq, 1), (seq, LANES))
    tw_cos, tw_sin = pair_order(tw_c), pair_order(tw_s)

    w_br_b = w_br.astype(BF16)
    w_o_b = w_o.astype(BF16)
    w1_b = w1.astype(BF16)
    w2_b = w2.astype(BF16)

    lat_row = lambda row: row // seq
    ctx_row = lambda row: CTX_MOD_ROW
    heads = COL_TILE // LANES

    xl = x.reshape(n_batch * seq, d)
    xc = ctx.reshape(n_batch * ctx_len, d)

    for l in range(depth):
        last = l == depth - 1
        lam_init = 0.8 - 0.6 * math.exp(-0.3 * l)

        hc = _normmod_call(xc, mod, l, ctx_row, ln1_g)
        hl = _normmod_call(xl, mod, l, lat_row, ln1_g)
        qk_c = _proj_call("qk", hc, w_in, l, QK_TILES, (qg, kg, cos_c, sin_c))
        qk_l = _proj_call("qk", hl, w_in, l, QK_TILES, (qg, kg, cos_l, sin_l))
        plain_c = _proj_call("plain", hc, w_in, l, (TILE_V,) if last else PLAIN_TILES)
        plain_l = _proj_call("plain", hl, w_in, l, PLAIN_TILES)
        gates_l = _proj_call("gate", hl, w_in, l, GATE_TILES)

        kv_ctx = (qk_c, heads, plain_c, (0 if last else PLAIN_V) * heads, ctx_len)
        kv_lat = (qk_l, heads, plain_l, PLAIN_V * heads, seq)
        attn_l = _attn_call(lam_q, lam_k, subln_g, l, qk_l, 0, [kv_ctx, kv_lat],
                            n_batch=n_batch, q_len=seq, lam_init=lam_init)
        four_l = _fourier_latent_call(plain_l, wa, wc, tw_cos, tw_sin, chan_cs, n_batch=n_batch)
        xl, h2l = _merge_call(xl, mod, l, lat_row, attn_l, plain_l, gates_l, four_l, w_conv,
                              w_br_b, w_o_b, ln2_g, seq_len=seq)
        xl = _mlp_call(xl, h2l, mod, l, lat_row, w1_b, w2_b)

        if not last:
            gates_c = _proj_call("gate", hc, w_in, l, GATE_TILES)
            attn_c = _attn_call(lam_q, lam_k, subln_g, l, qk_c, 0, [kv_ctx],
                                n_batch=n_batch, q_len=ctx_len, lam_init=lam_init)
            four_c = _fourier_dense_call(plain_c, PLAIN_F, chan_cos, chan_sin, pos_c,
                                         n_batch=n_batch, n_pos=ctx_len)
            xc, h2c = _merge_call(xc, mod, l, ctx_row, attn_c, plain_c, gates_c, four_c, w_conv,
                                  w_br_b, w_o_b, ln2_g, seq_len=ctx_len)
            xc = _mlp_call(xc, h2c, mod, l, ctx_row, w1_b, w2_b)

    return xl.reshape(n_batch, seq, d)
```

```python
import functools
import math

import jax
import jax.numpy as jnp
from jax import lax
from jax.experimental import pallas as pl
from jax.experimental.pallas import tpu as pltpu

D_MODEL = 2048
GRID_W = 64
N_HEADS = 8
HEAD_DIM = 64
V_HEAD_DIM = 128
BRANCH_W = 1024
FOURIER_GROUPS = 4
GROUP_W = BRANCH_W // FOURIER_GROUPS
N_BRANCH = 3
CONV_K = 3
EPS = 1e-6
ROPE_THETA = 10000.0

COL_TILE = 1024
TILE_Q, TILE_K, TILE_V, TILE_CB, TILE_CC, TILE_CX, TILE_F, TILE_G = range(8)
QK_TILES = (TILE_Q, TILE_K)
PLAIN_TILES = (TILE_CC, TILE_CX, TILE_CB, TILE_V, TILE_F)
PLAIN_CC, PLAIN_CX, PLAIN_CB, PLAIN_V, PLAIN_F = range(5)
GATE_TILES = tuple(range(TILE_G, TILE_G + N_BRANCH * D_MODEL // COL_TILE))

LANES = 128
SUBLANES = 8
MXU_W = 256
ONES_ROWS = 16
LOG2_E = math.log2(math.e)
MOD_ROWS = 8
CTX_MOD_ROW = 4
FFT_R = 64
STRIDE_PAD = 4

F32 = jnp.float32
BF16 = jnp.bfloat16


def _vmem(nbytes):
    return pltpu.CompilerParams(vmem_limit_bytes=nbytes)


def _sigmoid(x):
    return 1.0 / (1.0 + jnp.exp(-x))


def _rms_modulate(x, g, shift, scale):
    y = x * lax.rsqrt(jnp.mean(x * x, axis=-1, keepdims=True) + EPS) * g
    return y * (1.0 + scale) + shift


def _dot(a, b):
    return jnp.dot(a, b, preferred_element_type=F32)


def _dot_t(a, b):
    return lax.dot_general(a, b, (((1,), (1,)), ((), ())), preferred_element_type=F32)


def _mod_kernel(c_ref, w_ref, b_ref, o_ref):
    c = c_ref[...]
    cs = c * _sigmoid(c)
    o_ref[0] = jnp.dot(cs, w_ref[0], preferred_element_type=F32,
                       precision=lax.Precision.HIGHEST) + b_ref[0]


def _mod_call(c_rows, w_ada, b_ada):
    depth, d, n = w_ada.shape
    tn = 1536
    return pl.pallas_call(
        _mod_kernel,
        grid=(depth, n // tn),
        in_specs=[
            pl.BlockSpec((MOD_ROWS, d), lambda l, j: (0, 0)),
            pl.BlockSpec((1, d, tn), lambda l, j: (l, 0, j)),
            pl.BlockSpec((1, 1, tn), lambda l, j: (l, 0, j)),
        ],
        out_specs=pl.BlockSpec((1, MOD_ROWS, tn), lambda l, j: (l, 0, j)),
        out_shape=jax.ShapeDtypeStruct((depth, MOD_ROWS, n), F32),
        compiler_params=_vmem(40 << 20),
        name="mod",
    )(c_rows, w_ada, b_ada.reshape(depth, 1, n))


def _normmod_kernel(x_ref, mod_ref, g_ref, o_ref):
    h = _rms_modulate(x_ref[...], g_ref[0], mod_ref[0, 0, 0:1, :], mod_ref[0, 0, 1:2, :])
    o_ref[...] = h.astype(BF16)


def _normmod_call(x2d, mod, layer, mod_row, ln_g):
    m, d = x2d.shape
    tm = 1024
    return pl.pallas_call(
        _normmod_kernel,
        grid=(m // tm,),
        in_specs=[
            pl.BlockSpec((tm, d), lambda i: (i, 0)),
            pl.BlockSpec((1, 1, 6, d), lambda i: (layer, mod_row(i * tm), 0, 0)),
            pl.BlockSpec((1, 1, d), lambda i: (layer, 0, 0)),
        ],
        out_specs=pl.BlockSpec((tm, d), lambda i: (i, 0)),
        out_shape=jax.ShapeDtypeStruct((m, d), BF16),
        compiler_params=_vmem(32 << 20),
        name="normmod",
    )(x2d, mod, ln_g.reshape(ln_g.shape[0], 1, d))


def _cast_weights(w_ref, wb_ref):
    @pl.when(pl.program_id(1) == 0)
    def _():
        wb_ref[...] = w_ref[0].astype(BF16)


def _proj_plain_kernel(h_ref, w_ref, o_ref, wb_ref):
    _cast_weights(w_ref, wb_ref)
    o_ref[...] = _dot(h_ref[...], wb_ref[...]).astype(BF16)


def _proj_gate_kernel(h_ref, w_ref, o_ref, wb_ref):
    _cast_weights(w_ref, wb_ref)
    acc = _dot(h_ref[...], wb_ref[...])
    o_ref[...] = (0.5 * jnp.tanh(0.5 * acc) + 0.5).astype(BF16)


def _proj_qk_kernel(h_ref, w_ref, qg_ref, kg_ref, cos_ref, sin_ref, o_ref, wb_ref):
    _cast_weights(w_ref, wb_ref)
    h = h_ref[...]
    n_chunks = COL_TILE // MXU_W
    project = lambda t: _dot(h, wb_ref[:, t * MXU_W:(t + 1) * MXU_W])
    is_q = pl.program_id(0) == TILE_Q
    gain = jnp.where(is_q, qg_ref[0] * (LOG2_E * HEAD_DIM ** -0.5), kg_ref[0])
    cos = cos_ref[...]
    sin = sin_ref[...]
    lane = lax.broadcasted_iota(jnp.int32, (1, LANES), 1)
    first_half = (lane % (HEAD_DIM // 2)) < (HEAD_DIM // 4)
    gsum = (lax.broadcasted_iota(jnp.int32, (MXU_W, MXU_W), 0) // HEAD_DIM
            == lax.broadcasted_iota(jnp.int32, (MXU_W, MXU_W), 1) // HEAD_DIM).astype(BF16)
    a_next = project(0)
    for t in range(n_chunks):
        a = a_next
        if t + 1 < n_chunks:
            a_next = project(t + 1)
        ssq = _dot((a * a).astype(BF16), gsum)
        y = a * lax.rsqrt(ssq * (1.0 / HEAD_DIM) + EPS) * gain
        for u in range(MXU_W // LANES):
            yh = y[:, u * LANES:(u + 1) * LANES]
            partner = jnp.where(first_half,
                                pltpu.roll(yh, LANES - HEAD_DIM // 4, 1),
                                pltpu.roll(yh, HEAD_DIM // 4, 1))
            c0 = t * MXU_W + u * LANES
            o_ref[:, c0:c0 + LANES] = (yh * cos + partner * sin).astype(BF16)


def _proj_call(kind, h, w_in, layer, w_tiles, extra=()):
    m, d = h.shape
    tm = 1024
    n_tiles = len(w_tiles)

    def w_tile(j):
        tile = w_tiles[0]
        for idx in range(1, n_tiles):
            tile = jnp.where(j == idx, w_tiles[idx], tile)
        return tile

    in_specs = [
        pl.BlockSpec((tm, d), lambda j, i: (i, 0)),
        pl.BlockSpec((1, d, COL_TILE), lambda j, i: (layer, 0, w_tile(j))),
    ]
    args = [h, w_in]
    if kind == "qk":
        qg, kg, cos_t, sin_t = extra
        seq_tiles = cos_t.shape[0] // tm
        in_specs += [
            pl.BlockSpec((1, 1, MXU_W), lambda j, i: (layer, 0, 0)),
            pl.BlockSpec((1, 1, MXU_W), lambda j, i: (layer, 0, 0)),
            pl.BlockSpec((tm, LANES), lambda j, i: (i % seq_tiles, 0)),
            pl.BlockSpec((tm, LANES), lambda j, i: (i % seq_tiles, 0)),
        ]
        args += [qg, kg, cos_t, sin_t]
    body = {"qk": _proj_qk_kernel, "plain": _proj_plain_kernel, "gate": _proj_gate_kernel}[kind]
    return pl.pallas_call(
        body,
        grid=(n_tiles, m // tm),
        in_specs=in_specs,
        out_specs=pl.BlockSpec((tm, COL_TILE), lambda j, i: (i, j)),
        out_shape=jax.ShapeDtypeStruct((m, n_tiles * COL_TILE), BF16),
        scratch_shapes=[pltpu.VMEM((d, COL_TILE), BF16)],
        compiler_params=_vmem(52 << 20),
        name="proj_" + kind,
    )(*args)


def _attn_kernel(lq_ref, lk_ref, sg_ref, q_ref, *refs, lam_init, pieces, key_block):
    n = len(pieces)
    k_refs, v_refs = refs[0:2 * n:2], refs[1:2 * n:2]
    o_ref, vt_ref, kmax_ref = refs[2 * n:]
    tq = q_ref.shape[0]
    lane = lax.broadcasted_iota(jnp.int32, (1, LANES), 1)

    @pl.when(pl.program_id(2) == 0)
    def _():
        group = lax.broadcasted_iota(jnp.int32, (LANES, LANES), 0) // HEAD_DIM
        gsum = (group == lane // HEAD_DIM).astype(BF16)
        kmax = None
        off = 0
        for t in range(n):
            rows = pieces[t]
            vt_ref[0:V_HEAD_DIM, off:off + rows] = v_refs[t][...].astype(F32).T.astype(BF16)
            kf = k_refs[t][...].astype(F32)
            norms = _dot((kf * kf).astype(BF16), gsum)
            piece_max = jnp.max(norms, axis=0, keepdims=True)
            kmax = piece_max if kmax is None else jnp.maximum(kmax, piece_max)
            off += rows
        vt_ref[V_HEAD_DIM:, :] = jnp.ones((ONES_ROWS, off), BF16)
        kmax_ref[...] = kmax

    e = jnp.exp(jnp.sum(lq_ref[0] * lk_ref[0], axis=-1, keepdims=True))
    lam = e[0:1, :] - e[1:2, :] + lam_init

    q = q_ref[...]
    zero = jnp.zeros_like(q)
    lo = lane < HEAD_DIM
    q_both = jnp.concatenate([jnp.where(lo, q, zero), jnp.where(lo, zero, q)], axis=0)

    qf = q.astype(F32)
    sel = (lax.broadcasted_iota(jnp.int32, (SUBLANES, LANES), 0) == lane // HEAD_DIM).astype(BF16)
    q_norms = _dot_t(sel, (qf * qf).astype(BF16))
    kmax = kmax_ref[...]
    bounds = jnp.concatenate(
        [jnp.sqrt(q_norms[c:c + 1, :] * kmax[:, c * HEAD_DIM:c * HEAD_DIM + 1]) * 1.01 + 1e-6
         for c in range(2)], axis=1)

    def key_blocks():
        off = 0
        for t in range(n):
            for s in range(0, pieces[t], key_block):
                size = min(key_block, pieces[t] - s)
                yield k_refs[t], s, off + s, size
            off += pieces[t]

    def accumulate(shifts):
        blocks = list(key_blocks())
        scores = lambda blk: _dot_t(blk[0][blk[1]:blk[1] + blk[3], :], q_both)
        acc = jnp.zeros((V_HEAD_DIM + ONES_ROWS, 2 * tq), F32)
        st_next = scores(blocks[0])
        for idx, (_, _, pos, size) in enumerate(blocks):
            st = st_next
            if idx + 1 < len(blocks):
                st_next = scores(blocks[idx + 1])
            acc = acc + _dot(vt_ref[:, pos:pos + size], jnp.exp2(st - shifts).astype(BF16))
        return acc

    def finish(acc):
        sums = acc[V_HEAD_DIM:V_HEAD_DIM + 1, :]
        ot = (acc[0:V_HEAD_DIM, 0:tq] * (1.0 / sums[:, 0:tq])
              - acc[0:V_HEAD_DIM, tq:] * (lam / sums[:, tq:]))
        yt = ot * lax.rsqrt(jnp.mean(ot * ot, axis=0, keepdims=True) + EPS)
        o_ref[...] = (yt.T * (sg_ref[0] * (1.0 - lam_init))).astype(BF16)

    acc = accumulate(bounds)
    finish(acc)

    @pl.when(jnp.logical_not(jnp.min(acc[V_HEAD_DIM:V_HEAD_DIM + 1, :]) >= 1e-30))
    def _():
        m = None
        for k_ref, s, _, size in key_blocks():
            bm = jnp.max(_dot_t(k_ref[s:s + size, :], q_both), axis=0, keepdims=True)
            m = bm if m is None else jnp.maximum(m, bm)
        finish(accumulate(m))


def _attn_call(lam_q, lam_k, subln_g, layer, q_arr, q_tile0, kv_list, *, n_batch, q_len,
               lam_init):
    tq = min(1024, q_len)
    q_tiles = q_len // tq
    in_specs = [
        pl.BlockSpec((1, 2, HEAD_DIM), lambda b, h, i: (layer, 0, 0)),
        pl.BlockSpec((1, 2, HEAD_DIM), lambda b, h, i: (layer, 0, 0)),
        pl.BlockSpec((1, 1, V_HEAD_DIM), lambda b, h, i: (layer, 0, 0)),
        pl.BlockSpec((tq, LANES), lambda b, h, i: (b * q_tiles + i, q_tile0 + h)),
    ]
    args = [lam_q, lam_k, subln_g.reshape(subln_g.shape[0], 1, V_HEAD_DIM), q_arr]
    for k_arr, k0, v_arr, v0, rows in kv_list:
        in_specs.append(pl.BlockSpec((rows, LANES), lambda b, h, i, k0=k0: (b, k0 + h)))
        in_specs.append(pl.BlockSpec((rows, LANES), lambda b, h, i, v0=v0: (b, v0 + h)))
        args += [k_arr, v_arr]
    pieces = tuple(kv[4] for kv in kv_list)
    return pl.pallas_call(
        functools.partial(_attn_kernel, lam_init=lam_init, pieces=pieces, key_block=256),
        grid=(n_batch, N_HEADS, q_tiles),
        in_specs=in_specs,
        out_specs=pl.BlockSpec((tq, LANES), lambda b, h, i: (b * q_tiles + i, h)),
        out_shape=jax.ShapeDtypeStruct((n_batch * q_len, N_HEADS * V_HEAD_DIM), BF16),
        scratch_shapes=[pltpu.VMEM((V_HEAD_DIM + ONES_ROWS, sum(pieces)), BF16),
                        pltpu.VMEM((1, LANES), F32)],
        compiler_params=_vmem(52 << 20),
        name="attn",
    )(*args)


def _fourier_dense_kernel(x_ref, cc_ref, sc_ref, pos_ref, o_ref, y_ref, *, n_pos):
    @pl.when(pl.program_id(1) == 0)
    def _():
        cc = cc_ref[...]
        sc = sc_ref[...]
        for g in range(FOURIER_GROUPS):
            xg = x_ref[:, g * GROUP_W:(g + 1) * GROUP_W]
            y_ref[0:n_pos, g * GROUP_W:(g + 1) * GROUP_W] = _dot(xg, cc).astype(BF16)
            y_ref[n_pos:2 * n_pos, g * GROUP_W:(g + 1) * GROUP_W] = _dot(xg, sc).astype(BF16)

    o_ref[...] = _dot(pos_ref[...], y_ref[...]).astype(BF16)


def _fourier_dense_call(p_arr, f_tile, chan_cos, chan_sin, pos_mat, *, n_batch, n_pos):
    tm = min(256, n_pos)
    row_tiles = n_pos // tm
    return pl.pallas_call(
        functools.partial(_fourier_dense_kernel, n_pos=n_pos),
        grid=(n_batch, row_tiles),
        in_specs=[
            pl.BlockSpec((n_pos, BRANCH_W), lambda b, i: (b, f_tile),
                         pipeline_mode=pl.Buffered(1)),
            pl.BlockSpec((GROUP_W, GROUP_W), lambda b, i: (0, 0)),
            pl.BlockSpec((GROUP_W, GROUP_W), lambda b, i: (0, 0)),
            pl.BlockSpec((tm, 2 * n_pos), lambda b, i: (i, 0)),
        ],
        out_specs=pl.BlockSpec((tm, BRANCH_W), lambda b, i: (b * row_tiles + i, 0)),
        out_shape=jax.ShapeDtypeStruct((n_batch * n_pos, BRANCH_W), BF16),
        scratch_shapes=[pltpu.VMEM((2 * n_pos, BRANCH_W), BF16)],
        compiler_params=_vmem(32 << 20),
        name="fourier_dense",
    )(p_arr, chan_cos, chan_sin, pos_mat)


def _fourier_latent_kernel(x_ref, wa_ref, wc_ref, twc_ref, tws_ref, cs_ref, o_ref,
                           xs_ref, ar_ref, ai_ref, zr_ref, zi_ref):
    tiles = GROUP_W // LANES
    half = FFT_R // 2
    lane_tile = lambda v, t: v[:, t * LANES:(t + 1) * LANES]
    pack = lambda v: pltpu.bitcast(v.astype(BF16), jnp.int32)

    def gather(ref, rows):
        words = jnp.concatenate([ref[t, rows, :] for t in range(tiles)], axis=1)
        return pltpu.bitcast(words, BF16)

    def scatter(ref, rows, v):
        words = pack(v)
        for t in range(tiles):
            ref[t, rows, :] = lane_tile(words, t)

    pitch_h = half + STRIDE_PAD
    pitch_r = FFT_R + STRIDE_PAD

    for a in range(FFT_R):
        words = pltpu.bitcast(x_ref[a * FFT_R:(a + 1) * FFT_R, :], jnp.int32)
        for t in range(tiles):
            xs_ref[t, a * pitch_h:a * pitch_h + half, :] = lane_tile(words, t)
    wa = wa_ref[...]
    wc = wc_ref[...]

    def stage1(beta, carry):
        xb = gather(xs_ref, pl.ds(beta, FFT_R, stride=pitch_h))
        a = _dot(wa, xb)
        tw_rows = pl.ds(pl.multiple_of(beta * 2 * FFT_R, 2 * FFT_R), 2 * FFT_R)
        tc = twc_ref[tw_rows, :]
        ts = tws_ref[tw_rows, :]
        re, im = a[0:2 * FFT_R], a[2 * FFT_R:]
        tc = jnp.concatenate([tc] * tiles, axis=1)
        ts = jnp.concatenate([ts] * tiles, axis=1)
        rows = pl.ds(beta * pitch_r, FFT_R)
        scatter(ar_ref, rows, re * tc + im * ts)
        scatter(ai_ref, rows, im * tc - re * ts)
        return carry

    lax.fori_loop(0, half, stage1, 0, unroll=4)

    def stage2(gamma, carry):
        parts = []
        for p in range(2):
            rows = pl.ds(2 * gamma + p, half, stride=pitch_r)
            parts += [gather(ar_ref, rows), gather(ai_ref, rows)]
        z = _dot(wc, jnp.concatenate(parts, axis=0))
        rows = pl.ds(gamma, FFT_R, stride=pitch_h)
        scatter(zr_ref, rows, z[0:2 * FFT_R])
        scatter(zi_ref, rows, z[2 * FFT_R:])
        return carry

    lax.fori_loop(0, half, stage2, 0, unroll=4)

    cs = cs_ref[...]
    d_per_block = 8
    for d0 in range(0, FFT_R, d_per_block):
        z = jnp.concatenate(
            [jnp.concatenate([gather(ref, slice(d * pitch_h, d * pitch_h + half))
                              for ref in (zr_ref, zi_ref)], axis=1)
             for d in range(d0, d0 + d_per_block)], axis=0)
        o_ref[d0 * FFT_R:(d0 + d_per_block) * FFT_R, :] = _dot(z, cs).astype(BF16)


def _fourier_latent_call(plain, wa, wc, tw_cos, tw_sin, chan_cs, *, n_batch):
    n_pos = FFT_R * FFT_R
    tiles = GROUP_W // LANES
    const = lambda shape: pl.BlockSpec(shape, lambda n, q: (0,) * len(shape))
    half = FFT_R // 2
    by_half = pltpu.VMEM((tiles, FFT_R * (half + STRIDE_PAD), LANES), jnp.int32)
    by_r = pltpu.VMEM((tiles, half * (FFT_R + STRIDE_PAD), LANES), jnp.int32)
    return pl.pallas_call(
        _fourier_latent_kernel,
        grid=(n_batch, FOURIER_GROUPS),
        in_specs=[
            pl.BlockSpec((n_pos, GROUP_W), lambda n, q: (n, PLAIN_F * FOURIER_GROUPS + q)),
            const((4 * FFT_R, 2 * FFT_R)),
            const((4 * FFT_R, 4 * FFT_R)),
            const((n_pos, LANES)),
            const((n_pos, LANES)),
            const((2 * GROUP_W, GROUP_W)),
        ],
        out_specs=pl.BlockSpec((n_pos, GROUP_W), lambda n, q: (n, q)),
        out_shape=jax.ShapeDtypeStruct((n_batch * n_pos, BRANCH_W), BF16),
        scratch_shapes=[by_half, by_r, by_r, by_half, by_half],
        compiler_params=_vmem(48 << 20),
        name="fourier_latent",
    )(plain, wa, wc, tw_cos, tw_sin, chan_cs)


def _merge_kernel(x_ref, mod_ref, attn_ref, conv_ref, before_ref, after_ref, four_ref, gate_ref,
                  wconv_ref, wbr_ref, wo_ref, ln2_ref, o_ref, h2_ref, *, seq_len):
    tm, d = x_ref.shape
    i = pl.program_id(0)
    pos0 = (i * tm) % seq_len
    cc_cols = slice(PLAIN_CC * BRANCH_W, (PLAIN_CC + 1) * BRANCH_W)
    cx_cols = slice(PLAIN_CX * BRANCH_W, (PLAIN_CX + 1) * BRANCH_W)
    cb_cols = slice(PLAIN_CB * BRANCH_W, (PLAIN_CB + 1) * BRANCH_W)

    u = conv_ref[:, cc_cols].astype(F32) * conv_ref[:, cx_cols].astype(F32)
    last = slice(SUBLANES - 1, SUBLANES)
    u_before = before_ref[last, cc_cols].astype(F32) * before_ref[last, cx_cols].astype(F32)
    u_after = after_ref[0:1, cc_cols].astype(F32) * after_ref[0:1, cx_cols].astype(F32)
    u_before = jnp.where(pos0 == 0, 0.0, u_before)
    u_after = jnp.where(pos0 + tm == seq_len, 0.0, u_after)
    row = lax.broadcasted_iota(jnp.int32, (tm, 1), 0)
    u_prev = jnp.where(row == 0, u_before, pltpu.roll(u, 1, 0))
    u_next = jnp.where(row == tm - 1, u_after, pltpu.roll(u, tm - 1, 0))
    w = wconv_ref[0]
    conv = (conv_ref[:, cb_cols].astype(F32)
            * (u_prev * w[0:1, :] + u * w[1:2, :] + u_next * w[2:3, :]))

    gate = lambda branch: gate_ref[:, branch * d:(branch + 1) * d].astype(F32)
    mix = (gate(0) * _dot(attn_ref[...], wbr_ref[0, 0])
           + gate(1) * _dot(conv.astype(BF16), wbr_ref[0, 1])
           + gate(2) * _dot(four_ref[...], wbr_ref[0, 2]))
    y = _dot(mix.astype(BF16), wo_ref[0])
    x_new = x_ref[...] + mod_ref[0, 0, 2:3, :] * y
    o_ref[...] = x_new
    h2 = _rms_modulate(x_new, ln2_ref[0], mod_ref[0, 0, 3:4, :], mod_ref[0, 0, 4:5, :])
    h2_ref[...] = h2.astype(BF16)


def _merge_call(x2d, mod, layer, mod_row, attn_o, plain, gates, four_o, w_conv, w_br, w_o,
                ln2_g, *, seq_len):
    m, d = x2d.shape
    tm = 256
    halo_per_tile = tm // SUBLANES
    last_halo = m // SUBLANES - 1
    const = dict(pipeline_mode=pl.Buffered(1))
    assert (PLAIN_CC, PLAIN_CX, PLAIN_CB) == (0, 1, 2)

    return pl.pallas_call(
        functools.partial(_merge_kernel, seq_len=seq_len),
        grid=(m // tm,),
        in_specs=[
            pl.BlockSpec((tm, d), lambda i: (i, 0)),
            pl.BlockSpec((1, 1, 6, d), lambda i: (layer, mod_row(i * tm), 0, 0)),
            pl.BlockSpec((tm, BRANCH_W), lambda i: (i, 0)),
            pl.BlockSpec((tm, 3 * BRANCH_W), lambda i: (i, 0)),
            pl.BlockSpec((SUBLANES, 2 * BRANCH_W),
                         lambda i: (jnp.maximum(i * halo_per_tile - 1, 0), 0)),
            pl.BlockSpec((SUBLANES, 2 * BRANCH_W),
                         lambda i: (jnp.minimum((i + 1) * halo_per_tile, last_halo), 0)),
            pl.BlockSpec((tm, BRANCH_W), lambda i: (i, 0)),
            pl.BlockSpec((tm, N_BRANCH * d), lambda i: (i, 0)),
            pl.BlockSpec((1, CONV_K, BRANCH_W), lambda i: (layer, 0, 0)),
            pl.BlockSpec((1, N_BRANCH, BRANCH_W, d), lambda i: (layer, 0, 0, 0), **const),
            pl.BlockSpec((1, d, d), lambda i: (layer, 0, 0), **const),
            pl.BlockSpec((1, 1, d), lambda i: (layer, 0, 0)),
        ],
        out_specs=[pl.BlockSpec((tm, d), lambda i: (i, 0))] * 2,
        out_shape=[jax.ShapeDtypeStruct((m, d), F32), jax.ShapeDtypeStruct((m, d), BF16)],
        compiler_params=_vmem(56 << 20),
        name="merge",
    )(x2d, mod, attn_o, plain, plain, plain, four_o, gates, w_conv, w_br, w_o,
      ln2_g.reshape(ln2_g.shape[0], 1, d))


def _mlp_kernel(x_ref, h_ref, mod_ref, w1_ref, w2_ref, o_ref, *, out_chunk, x_parts):
    j = pl.program_id(1)

    @pl.when(j == 0)
    def _():
        o_ref[...] = jnp.zeros_like(o_ref)

    h = h_ref[...]
    half = w1_ref.shape[2] // 2
    up = lambda s: jnp.maximum(_dot(h, w1_ref[0, :, s * half:(s + 1) * half]), 0.0)
    hid = [up(0), up(1)]
    hid2 = [(v * v).astype(BF16) for v in hid]
    for n0 in range(0, o_ref.shape[1], out_chunk):
        cols = slice(n0, n0 + out_chunk)
        down = (_dot(hid2[0], w2_ref[0, 0:half, cols])
                + _dot(hid2[1], w2_ref[0, half:, cols]))
        o_ref[:, cols] += mod_ref[0, 0, 5:6, cols] * down

    part = j - (pl.num_programs(1) - x_parts)

    @pl.when(part >= 0)
    def _():
        part_rows = x_ref.shape[0]
        rows = pl.ds(pl.multiple_of(part * part_rows, part_rows), part_rows)
        o_ref[rows, :] += x_ref[...]


def _mlp_call(x2d, h2, mod, layer, mod_row, w1, w2):
    m, d = x2d.shape
    d_ff = w1.shape[2]
    tm, tf = 1024, 1024
    steps = d_ff // tf
    x_parts = 4

    def x_rows(i, j):
        part = j - (steps - x_parts)
        return jnp.where(part >= 0, i * x_parts + part, jnp.maximum(i * x_parts - 1, 0)), 0

    return pl.pallas_call(
        functools.partial(_mlp_kernel, out_chunk=512, x_parts=x_parts),
        grid=(m // tm, steps),
        in_specs=[
            pl.BlockSpec((tm // x_parts, d), x_rows),
            pl.BlockSpec((tm, d), lambda i, j: (i, 0)),
            pl.BlockSpec((1, 1, 6, d), lambda i, j: (layer, mod_row(i * tm), 0, 0)),
            pl.BlockSpec((1, d, tf), lambda i, j: (layer, 0, j)),
            pl.BlockSpec((1, tf, d), lambda i, j: (layer, j, 0)),
        ],
        out_specs=pl.BlockSpec((tm, d), lambda i, j: (i, 0)),
        out_shape=jax.ShapeDtypeStruct((m, d), F32),
        compiler_params=_vmem(56 << 20),
        name="mlp",
    )(x2d, h2, mod, w1, w2)


def _rope_tables(seq_len):
    rows = seq_len // GRID_W
    t_row = jnp.repeat(jnp.arange(rows), GRID_W).astype(F32)
    t_col = jnp.tile(jnp.arange(GRID_W), rows).astype(F32)
    axis_dim = HEAD_DIM // 2
    inv_freq = 1.0 / (ROPE_THETA ** (jnp.arange(0, axis_dim, 2, dtype=F32) / axis_dim))
    ang_r = t_row[:, None] * inv_freq[None, :]
    ang_c = t_col[:, None] * inv_freq[None, :]
    cos = jnp.concatenate([jnp.cos(ang_r)] * 2 + [jnp.cos(ang_c)] * 2, axis=-1)
    sin = jnp.concatenate([-jnp.sin(ang_r), jnp.sin(ang_r), -jnp.sin(ang_c), jnp.sin(ang_c)],
                          axis=-1)
    return jnp.tile(cos, (1, LANES // HEAD_DIM)), jnp.tile(sin, (1, LANES // HEAD_DIM))


def _dft_cos_sin(rows, cols, period, scale):
    r = jnp.arange(rows, dtype=jnp.int32)
    c = jnp.arange(cols, dtype=jnp.int32)
    ang = ((r[:, None] * c[None, :]) % period).astype(F32) * (2.0 * math.pi / period)
    return jnp.cos(ang) * scale, jnp.sin(ang) * scale


def kernel(x, c, ctx, c_ctx, ln1_g, ln2_g, w_ada, b_ada, w_in, qn_g, kn_g, lam_q, lam_k,
           subln_g, w_conv, w_br, w_o, w1, w2):
    n_batch, seq, d = x.shape
    ctx_len = ctx.shape[1]
    depth = w_ada.shape[0]
    assert seq == FFT_R * FFT_R and ctx_len == 256 and n_batch <= CTX_MOD_ROW and d == D_MODEL

    c_rows = jnp.zeros((MOD_ROWS, d), F32).at[:n_batch].set(c).at[CTX_MOD_ROW].set(c_ctx)
    mod = _mod_call(c_rows, w_ada, b_ada).reshape(depth, MOD_ROWS, 6, d)

    cos_l, sin_l = _rope_tables(seq)
    cos_c = jnp.ones((n_batch * ctx_len, LANES), F32)
    sin_c = jnp.zeros((n_batch * ctx_len, LANES), F32)
    qg = jnp.tile(qn_g, (1, MXU_W // HEAD_DIM)).reshape(depth, 1, MXU_W)
    kg = jnp.tile(kn_g, (1, MXU_W // HEAD_DIM)).reshape(depth, 1, MXU_W)

    ch_c, ch_s = _dft_cos_sin(GROUP_W, GROUP_W, GROUP_W, GROUP_W ** -0.5)
    chan_cos, chan_sin = ch_c.astype(BF16), ch_s.astype(BF16)
    chan_cs = jnp.concatenate([chan_cos, chan_sin], axis=0)
    pc_c, pc_s = _dft_cos_sin(ctx_len, ctx_len, ctx_len, ctx_len ** -0.5)
    pos_c = jnp.concatenate([pc_c, -pc_s], axis=1).astype(BF16)
    r_c, r_s = _dft_cos_sin(FFT_R, FFT_R, FFT_R, FFT_R ** -0.5)
    eye2 = jnp.eye(2, dtype=F32)
    wa = jnp.kron(jnp.concatenate([r_c, -r_s], axis=0), eye2).astype(BF16)
    wc = jnp.concatenate([jnp.concatenate([r_c, r_s], axis=1),
                          jnp.concatenate([-r_s, r_c], axis=1)], axis=0)
    wc = jnp.einsum("odib,pq->odpqib", wc.reshape(2, FFT_R, 2, FFT_R), eye2)
    wc = wc.reshape(4 * FFT_R, 4 * FFT_R).astype(BF16)
    tw_c, tw_s = _dft_cos_sin(FFT_R, FFT_R, seq, 1.0)
    pair_order = lambda t: jnp.broadcast_to(
        t.reshape(FFT_R // 2, 2, FFT_R).transpose(0, 2, 1).reshape(seq, 1), (seq, LANES))
    tw_cos, tw_sin = pair_order(tw_c), pair_order(tw_s)

    w_br_b = w_br.astype(BF16)
    w_o_b = w_o.astype(BF16)
    w1_b = w1.astype(BF16)
    w2_b = w2.astype(BF16)

    lat_row = lambda row: row // seq
    ctx_row = lambda row: CTX_MOD_ROW
    heads = COL_TILE // LANES

    xl = x.reshape(n_batch * seq, d)
    xc = ctx.reshape(n_batch * ctx_len, d)

    for l in range(depth):
        last = l == depth - 1
        lam_init = 0.8 - 0.6 * math.exp(-0.3 * l)

        hc = _normmod_call(xc, mod, l, ctx_row, ln1_g)
        hl = _normmod_call(xl, mod, l, lat_row, ln1_g)
        qk_c = _proj_call("qk", hc, w_in, l, QK_TILES, (qg, kg, cos_c, sin_c))
        qk_l = _proj_call("qk", hl, w_in, l, QK_TILES, (qg, kg, cos_l, sin_l))
        plain_c = _proj_call("plain", hc, w_in, l, (TILE_V,) if last else PLAIN_TILES)
        plain_l = _proj_call("plain", hl, w_in, l, PLAIN_TILES)
        gates_l = _proj_call("gate", hl, w_in, l, GATE_TILES)

        kv_ctx = (qk_c, heads, plain_c, (0 if last else PLAIN_V) * heads, ctx_len)
        kv_lat = (qk_l, heads, plain_l, PLAIN_V * heads, seq)
        attn_l = _attn_call(lam_q, lam_k, subln_g, l, qk_l, 0, [kv_ctx, kv_lat],
                            n_batch=n_batch, q_len=seq, lam_init=lam_init)
        four_l = _fourier_latent_call(plain_l, wa, wc, tw_cos, tw_sin, chan_cs, n_batch=n_batch)
        xl, h2l = _merge_call(xl, mod, l, lat_row, attn_l, plain_l, gates_l, four_l, w_conv,
                              w_br_b, w_o_b, ln2_g, seq_len=seq)
        xl = _mlp_call(xl, h2l, mod, l, lat_row, w1_b, w2_b)

        if not last:
            gates_c = _proj_call("gate", hc, w_in, l, GATE_TILES)
            attn_c = _attn_call(lam_q, lam_k, subln_g, l, qk_c, 0, [kv_ctx],
                                n_batch=n_batch, q_len=ctx_len, lam_init=lam_init)
            four_c = _fourier_dense_call(plain_c, PLAIN_F, chan_cos, chan_sin, pos_c,
                                         n_batch=n_batch, n_pos=ctx_len)
            xc, h2c = _merge_call(xc, mod, l, ctx_row, attn_c, plain_c, gates_c, four_c, w_conv,
                                  w_br_b, w_o_b, ln2_g, seq_len=ctx_len)
            xc = _mlp_call(xc, h2c, mod, l, ctx_row, w1_b, w2_b)

    return xl.reshape(n_batch, seq, d)
```

```python
import functools
import math

import jax
import jax.numpy as jnp
from jax import lax
from jax.experimental import pallas as pl
from jax.experimental.pallas import tpu as pltpu

D_MODEL = 2048
GRID_W = 64
N_HEADS = 8
HEAD_DIM = 64
V_HEAD_DIM = 128
BRANCH_W = 1024
FOURIER_GROUPS = 4
GROUP_W = BRANCH_W // FOURIER_GROUPS
N_BRANCH = 3
CONV_K = 3
EPS = 1e-6
ROPE_THETA = 10000.0

COL_TILE = 1024
PROJ_ROW_CHUNK = 1024
TILE_Q, TILE_K, TILE_V, TILE_CB, TILE_CC, TILE_CX, TILE_F, TILE_G = range(8)
QK_TILES = (TILE_Q, TILE_K)
PLAIN_TILES = (TILE_CC, TILE_CX, TILE_CB, TILE_V, TILE_F)
PLAIN_CC, PLAIN_CX, PLAIN_CB, PLAIN_V, PLAIN_F = range(5)
GATE_TILES = tuple(range(TILE_G, TILE_G + N_BRANCH * D_MODEL // COL_TILE))

LANES = 128
SUBLANES = 8
MXU_W = 256
ONES_ROWS = 16
LOG2_E = math.log2(math.e)
MOD_ROWS = 8
CTX_MOD_ROW = 4
FFT_R = 64
STRIDE_PAD = 4

F32 = jnp.float32
BF16 = jnp.bfloat16


def _vmem(nbytes):
    return pltpu.CompilerParams(vmem_limit_bytes=nbytes)


def _sigmoid(x):
    return 1.0 / (1.0 + jnp.exp(-x))


def _rms_modulate(x, g, shift, scale):
    y = x * lax.rsqrt(jnp.mean(x * x, axis=-1, keepdims=True) + EPS) * g
    return y * (1.0 + scale) + shift


def _dot(a, b):
    return jnp.dot(a, b, preferred_element_type=F32)


def _dot_t(a, b):
    return lax.dot_general(a, b, (((1,), (1,)), ((), ())), preferred_element_type=F32)


def _mod_kernel(c_ref, w_ref, b_ref, o_ref):
    c = c_ref[...]
    cs = c * _sigmoid(c)
    o_ref[0] = jnp.dot(cs, w_ref[0], preferred_element_type=F32,
                       precision=lax.Precision.HIGHEST) + b_ref[0]


def _mod_call(c_rows, w_ada, b_ada):
    depth, d, n = w_ada.shape
    tn = 1536
    return pl.pallas_call(
        _mod_kernel,
        grid=(depth, n // tn),
        in_specs=[
            pl.BlockSpec((MOD_ROWS, d), lambda l, j: (0, 0)),
            pl.BlockSpec((1, d, tn), lambda l, j: (l, 0, j)),
            pl.BlockSpec((1, 1, tn), lambda l, j: (l, 0, j)),
        ],
        out_specs=pl.BlockSpec((1, MOD_ROWS, tn), lambda l, j: (l, 0, j)),
        out_shape=jax.ShapeDtypeStruct((depth, MOD_ROWS, n), F32),
        compiler_params=_vmem(40 << 20),
        name="mod",
    )(c_rows, w_ada, b_ada.reshape(depth, 1, n))


def _normmod_kernel(x_ref, mod_ref, g_ref, o_ref):
    h = _rms_modulate(x_ref[...], g_ref[0], mod_ref[0, 0, 0:1, :], mod_ref[0, 0, 1:2, :])
    o_ref[...] = h.astype(BF16)


def _normmod_call(x2d, mod, layer, mod_row, ln_g):
    m, d = x2d.shape
    tm = 1024
    return pl.pallas_call(
        _normmod_kernel,
        grid=(m // tm,),
        in_specs=[
            pl.BlockSpec((tm, d), lambda i: (i, 0)),
            pl.BlockSpec((1, 1, 6, d), lambda i: (layer, mod_row(i * tm), 0, 0)),
            pl.BlockSpec((1, 1, d), lambda i: (layer, 0, 0)),
        ],
        out_specs=pl.BlockSpec((tm, d), lambda i: (i, 0)),
        out_shape=jax.ShapeDtypeStruct((m, d), BF16),
        compiler_params=_vmem(32 << 20),
        name="normmod",
    )(x2d, mod, ln_g.reshape(ln_g.shape[0], 1, d))


def _cast_weights(w_ref, wb_ref):
    @pl.when(pl.program_id(1) == 0)
    def _():
        wb_ref[...] = w_ref[0].astype(BF16)


def _project_rows(h_ref, wb_ref, o_ref, epilogue):
    for r in range(0, h_ref.shape[0], PROJ_ROW_CHUNK):
        rows = slice(r, r + PROJ_ROW_CHUNK)
        o_ref[rows, :] = epilogue(_dot(h_ref[rows, :], wb_ref[...])).astype(BF16)


def _proj_plain_kernel(h_ref, w_ref, o_ref, wb_ref):
    _cast_weights(w_ref, wb_ref)
    _project_rows(h_ref, wb_ref, o_ref, lambda acc: acc)


def _proj_gate_kernel(h_ref, w_ref, o_ref, wb_ref):
    _cast_weights(w_ref, wb_ref)
    _project_rows(h_ref, wb_ref, o_ref, lambda acc: 0.5 * jnp.tanh(0.5 * acc) + 0.5)


def _proj_qk_kernel(h_ref, w_ref, qg_ref, kg_ref, cos_ref, sin_ref, o_ref, wb_ref):
    _cast_weights(w_ref, wb_ref)
    h = h_ref[...]
    n_chunks = COL_TILE // MXU_W
    project = lambda t: _dot(h, wb_ref[:, t * MXU_W:(t + 1) * MXU_W])
    is_q = pl.program_id(0) == TILE_Q
    gain = jnp.where(is_q, qg_ref[0] * (LOG2_E * HEAD_DIM ** -0.5), kg_ref[0])
    cos = cos_ref[...]
    sin = sin_ref[...]
    lane = lax.broadcasted_iota(jnp.int32, (1, LANES), 1)
    first_half = (lane % (HEAD_DIM // 2)) < (HEAD_DIM // 4)
    gsum = (lax.broadcasted_iota(jnp.int32, (MXU_W, MXU_W), 0) // HEAD_DIM
            == lax.broadcasted_iota(jnp.int32, (MXU_W, MXU_W), 1) // HEAD_DIM).astype(BF16)
    a_next = project(0)
    for t in range(n_chunks):
        a = a_next
        if t + 1 < n_chunks:
            a_next = project(t + 1)
        ssq = _dot((a * a).astype(BF16), gsum)
        y = a * lax.rsqrt(ssq * (1.0 / HEAD_DIM) + EPS) * gain
        for u in range(MXU_W // LANES):
            yh = y[:, u * LANES:(u + 1) * LANES]
            partner = jnp.where(first_half,
                                pltpu.roll(yh, LANES - HEAD_DIM // 4, 1),
                                pltpu.roll(yh, HEAD_DIM // 4, 1))
            c0 = t * MXU_W + u * LANES
            o_ref[:, c0:c0 + LANES] = (yh * cos + partner * sin).astype(BF16)


def _proj_call(kind, h, w_in, layer, w_tiles, extra=()):
    m, d = h.shape
    tm = 2 * PROJ_ROW_CHUNK if kind != "qk" and m % (2 * PROJ_ROW_CHUNK) == 0 else PROJ_ROW_CHUNK
    n_tiles = len(w_tiles)

    def w_tile(j):
        tile = w_tiles[0]
        for idx in range(1, n_tiles):
            tile = jnp.where(j == idx, w_tiles[idx], tile)
        return tile

    in_specs = [
        pl.BlockSpec((tm, d), lambda j, i: (i, 0)),
        pl.BlockSpec((1, d, COL_TILE), lambda j, i: (layer, 0, w_tile(j))),
    ]
    args = [h, w_in]
    if kind == "qk":
        qg, kg, cos_t, sin_t = extra
        seq_tiles = cos_t.shape[0] // tm
        in_specs += [
            pl.BlockSpec((1, 1, MXU_W), lambda j, i: (layer, 0, 0)),
            pl.BlockSpec((1, 1, MXU_W), lambda j, i: (layer, 0, 0)),
            pl.BlockSpec((tm, LANES), lambda j, i: (i % seq_tiles, 0)),
            pl.BlockSpec((tm, LANES), lambda j, i: (i % seq_tiles, 0)),
        ]
        args += [qg, kg, cos_t, sin_t]
    body = {"qk": _proj_qk_kernel, "plain": _proj_plain_kernel, "gate": _proj_gate_kernel}[kind]
    return pl.pallas_call(
        body,
        grid=(n_tiles, m // tm),
        in_specs=in_specs,
        out_specs=pl.BlockSpec((tm, COL_TILE), lambda j, i: (i, j)),
        out_shape=jax.ShapeDtypeStruct((m, n_tiles * COL_TILE), BF16),
        scratch_shapes=[pltpu.VMEM((d, COL_TILE), BF16)],
        compiler_params=_vmem(52 << 20),
        name="proj_" + kind,
    )(*args)


def _attn_kernel(lq_ref, lk_ref, sg_ref, q_ref, *refs, lam_init, pieces, key_block):
    n = len(pieces)
    k_refs, v_refs = refs[0:2 * n:2], refs[1:2 * n:2]
    o_ref, vt_ref, kmax_ref = refs[2 * n:]
    tq = q_ref.shape[0]
    lane = lax.broadcasted_iota(jnp.int32, (1, LANES), 1)

    @pl.when(pl.program_id(2) == 0)
    def _():
        group = lax.broadcasted_iota(jnp.int32, (LANES, LANES), 0) // HEAD_DIM
        gsum = (group == lane // HEAD_DIM).astype(BF16)
        kmax = None
        off = 0
        for t in range(n):
            rows = pieces[t]
            vt_ref[0:V_HEAD_DIM, off:off + rows] = v_refs[t][...].astype(F32).T.astype(BF16)
            kf = k_refs[t][...].astype(F32)
            norms = _dot((kf * kf).astype(BF16), gsum)
            piece_max = jnp.max(norms, axis=0, keepdims=True)
            kmax = piece_max if kmax is None else jnp.maximum(kmax, piece_max)
            off += rows
        vt_ref[V_HEAD_DIM:, :] = jnp.ones((ONES_ROWS, off), BF16)
        kmax_ref[...] = kmax

    e = jnp.exp(jnp.sum(lq_ref[0] * lk_ref[0], axis=-1, keepdims=True))
    lam = e[0:1, :] - e[1:2, :] + lam_init

    q = q_ref[...]
    zero = jnp.zeros_like(q)
    lo = lane < HEAD_DIM
    q_both = jnp.concatenate([jnp.where(lo, q, zero), jnp.where(lo, zero, q)], axis=0)

    qf = q.astype(F32)
    sel = (lax.broadcasted_iota(jnp.int32, (SUBLANES, LANES), 0) == lane // HEAD_DIM).astype(BF16)
    q_norms = _dot_t(sel, (qf * qf).astype(BF16))
    kmax = kmax_ref[...]
    bounds = jnp.concatenate(
        [jnp.sqrt(q_norms[c:c + 1, :] * kmax[:, c * HEAD_DIM:c * HEAD_DIM + 1]) * 1.01 + 1e-6
         for c in range(2)], axis=1)

    def key_blocks():
        off = 0
        for t in range(n):
            for s in range(0, pieces[t], key_block):
                size = min(key_block, pieces[t] - s)
                yield k_refs[t], s, off + s, size
            off += pieces[t]

    def accumulate(shifts):
        blocks = list(key_blocks())
        scores = lambda blk: _dot_t(blk[0][blk[1]:blk[1] + blk[3], :], q_both)
        acc = jnp.zeros((V_HEAD_DIM + ONES_ROWS, 2 * tq), F32)
        st_next = scores(blocks[0])
        for idx, (_, _, pos, size) in enumerate(blocks):
            st = st_next
            if idx + 1 < len(blocks):
                st_next = scores(blocks[idx + 1])
            acc = acc + _dot(vt_ref[:, pos:pos + size], jnp.exp2(st - shifts).astype(BF16))
        return acc

    def finish(acc):
        sums = acc[V_HEAD_DIM:V_HEAD_DIM + 1, :]
        ot = (acc[0:V_HEAD_DIM, 0:tq] * (1.0 / sums[:, 0:tq])
              - acc[0:V_HEAD_DIM, tq:] * (lam / sums[:, tq:]))
        yt = ot * lax.rsqrt(jnp.mean(ot * ot, axis=0, keepdims=True) + EPS)
        o_ref[...] = (yt.T * (sg_ref[0] * (1.0 - lam_init))).astype(BF16)

    acc = accumulate(bounds)
    finish(acc)

    @pl.when(jnp.logical_not(jnp.min(acc[V_HEAD_DIM:V_HEAD_DIM + 1, :]) >= 1e-30))
    def _():
        m = None
        for k_ref, s, _, size in key_blocks():
            bm = jnp.max(_dot_t(k_ref[s:s + size, :], q_both), axis=0, keepdims=True)
            m = bm if m is None else jnp.maximum(m, bm)
        finish(accumulate(m))


def _attn_call(lam_q, lam_k, subln_g, layer, q_arr, q_tile0, kv_list, *, n_batch, q_len,
               lam_init):
    tq = min(1024, q_len)
    q_tiles = q_len // tq
    in_specs = [
        pl.BlockSpec((1, 2, HEAD_DIM), lambda b, h, i: (layer, 0, 0)),
        pl.BlockSpec((1, 2, HEAD_DIM), lambda b, h, i: (layer, 0, 0)),
        pl.BlockSpec((1, 1, V_HEAD_DIM), lambda b, h, i: (layer, 0, 0)),
        pl.BlockSpec((tq, LANES), lambda b, h, i: (b * q_tiles + i, q_tile0 + h)),
    ]
    args = [lam_q, lam_k, subln_g.reshape(subln_g.shape[0], 1, V_HEAD_DIM), q_arr]
    for k_arr, k0, v_arr, v0, rows in kv_list:
        in_specs.append(pl.BlockSpec((rows, LANES), lambda b, h, i, k0=k0: (b, k0 + h)))
        in_specs.append(pl.BlockSpec((rows, LANES), lambda b, h, i, v0=v0: (b, v0 + h)))
        args += [k_arr, v_arr]
    pieces = tuple(kv[4] for kv in kv_list)
    return pl.pallas_call(
        functools.partial(_attn_kernel, lam_init=lam_init, pieces=pieces, key_block=512),
        grid=(n_batch, N_HEADS, q_tiles),
        in_specs=in_specs,
        out_specs=pl.BlockSpec((tq, LANES), lambda b, h, i: (b * q_tiles + i, h)),
        out_shape=jax.ShapeDtypeStruct((n_batch * q_len, N_HEADS * V_HEAD_DIM), BF16),
        scratch_shapes=[pltpu.VMEM((V_HEAD_DIM + ONES_ROWS, sum(pieces)), BF16),
                        pltpu.VMEM((1, LANES), F32)],
        compiler_params=_vmem(52 << 20),
        name="attn",
    )(*args)


def _fourier_dense_kernel(x_ref, cc_ref, sc_ref, pos_ref, o_ref, y_ref, *, n_pos):
    @pl.when(pl.program_id(1) == 0)
    def _():
        cc = cc_ref[...]
        sc = sc_ref[...]
        for g in range(FOURIER_GROUPS):
            xg = x_ref[:, g * GROUP_W:(g + 1) * GROUP_W]
            y_ref[0:n_pos, g * GROUP_W:(g + 1) * GROUP_W] = _dot(xg, cc).astype(BF16)
            y_ref[n_pos:2 * n_pos, g * GROUP_W:(g + 1) * GROUP_W] = _dot(xg, sc).astype(BF16)

    o_ref[...] = _dot(pos_ref[...], y_ref[...]).astype(BF16)


def _fourier_dense_call(p_arr, f_tile, chan_cos, chan_sin, pos_mat, *, n_batch, n_pos):
    tm = min(256, n_pos)
    row_tiles = n_pos // tm
    return pl.pallas_call(
        functools.partial(_fourier_dense_kernel, n_pos=n_pos),
        grid=(n_batch, row_tiles),
        in_specs=[
            pl.BlockSpec((n_pos, BRANCH_W), lambda b, i: (b, f_tile),
                         pipeline_mode=pl.Buffered(1)),
            pl.BlockSpec((GROUP_W, GROUP_W), lambda b, i: (0, 0)),
            pl.BlockSpec((GROUP_W, GROUP_W), lambda b, i: (0, 0)),
            pl.BlockSpec((tm, 2 * n_pos), lambda b, i: (i, 0)),
        ],
        out_specs=pl.BlockSpec((tm, BRANCH_W), lambda b, i: (b * row_tiles + i, 0)),
        out_shape=jax.ShapeDtypeStruct((n_batch * n_pos, BRANCH_W), BF16),
        scratch_shapes=[pltpu.VMEM((2 * n_pos, BRANCH_W), BF16)],
        compiler_params=_vmem(32 << 20),
        name="fourier_dense",
    )(p_arr, chan_cos, chan_sin, pos_mat)


def _fourier_latent_kernel(x_ref, wa_ref, wc_ref, twc_ref, tws_ref, cs_ref, o_ref,
                           xs_ref, ar_ref, ai_ref, zr_ref, zi_ref):
    tiles = GROUP_W // LANES
    half = FFT_R // 2
    lane_tile = lambda v, t: v[:, t * LANES:(t + 1) * LANES]
    pack = lambda v: pltpu.bitcast(v.astype(BF16), jnp.int32)

    def gather(ref, rows):
        words = jnp.concatenate([ref[t, rows, :] for t in range(tiles)], axis=1)
        return pltpu.bitcast(words, BF16)

    def scatter(ref, rows, v):
        words = pack(v)
        for t in range(tiles):
            ref[t, rows, :] = lane_tile(words, t)

    pitch_h = half + STRIDE_PAD
    pitch_r = FFT_R + STRIDE_PAD

    for a in range(FFT_R):
        words = pltpu.bitcast(x_ref[a * FFT_R:(a + 1) * FFT_R, :], jnp.int32)
        for t in range(tiles):
            xs_ref[t, a * pitch_h:a * pitch_h + half, :] = lane_tile(words, t)
    wa = wa_ref[...]
    wc = wc_ref[...]

    def stage1(beta, carry):
        xb = gather(xs_ref, pl.ds(beta, FFT_R, stride=pitch_h))
        a = _dot(wa, xb)
        tw_rows = pl.ds(pl.multiple_of(beta * 2 * FFT_R, 2 * FFT_R), 2 * FFT_R)
        tc = twc_ref[tw_rows, :]
        ts = tws_ref[tw_rows, :]
        re, im = a[0:2 * FFT_R], a[2 * FFT_R:]
        tc = jnp.concatenate([tc] * tiles, axis=1)
        ts = jnp.concatenate([ts] * tiles, axis=1)
        rows = pl.ds(beta * pitch_r, FFT_R)
        scatter(ar_ref, rows, re * tc + im * ts)
        scatter(ai_ref, rows, im * tc - re * ts)
        return carry

    lax.fori_loop(0, half, stage1, 0, unroll=4)

    def stage2(gamma, carry):
        parts = []
        for p in range(2):
            rows = pl.ds(2 * gamma + p, half, stride=pitch_r)
            parts += [gather(ar_ref, rows), gather(ai_ref, rows)]
        z = _dot(wc, jnp.concatenate(parts, axis=0))
        rows = pl.ds(gamma, FFT_R, stride=pitch_h)
        scatter(zr_ref, rows, z[0:2 * FFT_R])
        scatter(zi_ref, rows, z[2 * FFT_R:])
        return carry

    lax.fori_loop(0, half, stage2, 0, unroll=4)

    cs = cs_ref[...]
    d_per_block = 8
    for d0 in range(0, FFT_R, d_per_block):
        z = jnp.concatenate(
            [jnp.concatenate([gather(ref, slice(d * pitch_h, d * pitch_h + half))
                              for ref in (zr_ref, zi_ref)], axis=1)
             for d in range(d0, d0 + d_per_block)], axis=0)
        o_ref[d0 * FFT_R:(d0 + d_per_block) * FFT_R, :] = _dot(z, cs).astype(BF16)


def _fourier_latent_call(plain, wa, wc, tw_cos, tw_sin, chan_cs, *, n_batch):
    n_pos = FFT_R * FFT_R
    tiles = GROUP_W // LANES
    const = lambda shape: pl.BlockSpec(shape, lambda n, q: (0,) * len(shape))
    half = FFT_R // 2
    by_half = pltpu.VMEM((tiles, FFT_R * (half + STRIDE_PAD), LANES), jnp.int32)
    by_r = pltpu.VMEM((tiles, half * (FFT_R + STRIDE_PAD), LANES), jnp.int32)
    return pl.pallas_call(
        _fourier_latent_kernel,
        grid=(n_batch, FOURIER_GROUPS),
        in_specs=[
            pl.BlockSpec((n_pos, GROUP_W), lambda n, q: (n, PLAIN_F * FOURIER_GROUPS + q)),
            const((4 * FFT_R, 2 * FFT_R)),
            const((4 * FFT_R, 4 * FFT_R)),
            const((n_pos, LANES)),
            const((n_pos, LANES)),
            const((2 * GROUP_W, GROUP_W)),
        ],
        out_specs=pl.BlockSpec((n_pos, GROUP_W), lambda n, q: (n, q)),
        out_shape=jax.ShapeDtypeStruct((n_batch * n_pos, BRANCH_W), BF16),
        scratch_shapes=[by_half, by_r, by_r, by_half, by_half],
        compiler_params=_vmem(48 << 20),
        name="fourier_latent",
    )(plain, wa, wc, tw_cos, tw_sin, chan_cs)


def _merge_kernel(x_ref, mod_ref, attn_ref, conv_ref, before_ref, after_ref, four_ref, gate_ref,
                  wconv_ref, wbr_ref, wo_ref, ln2_ref, o_ref, h2_ref, *, seq_len):
    tm, d = x_ref.shape
    i = pl.program_id(0)
    pos0 = (i * tm) % seq_len
    cc_cols = slice(PLAIN_CC * BRANCH_W, (PLAIN_CC + 1) * BRANCH_W)
    cx_cols = slice(PLAIN_CX * BRANCH_W, (PLAIN_CX + 1) * BRANCH_W)
    cb_cols = slice(PLAIN_CB * BRANCH_W, (PLAIN_CB + 1) * BRANCH_W)

    u = conv_ref[:, cc_cols].astype(F32) * conv_ref[:, cx_cols].astype(F32)
    last = slice(SUBLANES - 1, SUBLANES)
    u_before = before_ref[last, cc_cols].astype(F32) * before_ref[last, cx_cols].astype(F32)
    u_after = after_ref[0:1, cc_cols].astype(F32) * after_ref[0:1, cx_cols].astype(F32)
    u_before = jnp.where(pos0 == 0, 0.0, u_before)
    u_after = jnp.where(pos0 + tm == seq_len, 0.0, u_after)
    row = lax.broadcasted_iota(jnp.int32, (tm, 1), 0)
    u_prev = jnp.where(row == 0, u_before, pltpu.roll(u, 1, 0))
    u_next = jnp.where(row == tm - 1, u_after, pltpu.roll(u, tm - 1, 0))
    w = wconv_ref[0]
    conv = (conv_ref[:, cb_cols].astype(F32)
            * (u_prev * w[0:1, :] + u * w[1:2, :] + u_next * w[2:3, :]))

    gate = lambda branch: gate_ref[:, branch * d:(branch + 1) * d].astype(F32)
    mix = (gate(0) * _dot(attn_ref[...], wbr_ref[0, 0])
           + gate(1) * _dot(conv.astype(BF16), wbr_ref[0, 1])
           + gate(2) * _dot(four_ref[...], wbr_ref[0, 2]))
    y = _dot(mix.astype(BF16), wo_ref[0])
    x_new = x_ref[...] + mod_ref[0, 0, 2:3, :] * y
    o_ref[...] = x_new
    h2 = _rms_modulate(x_new, ln2_ref[0], mod_ref[0, 0, 3:4, :], mod_ref[0, 0, 4:5, :])
    h2_ref[...] = h2.astype(BF16)


def _merge_call(x2d, mod, layer, mod_row, attn_o, plain, gates, four_o, w_conv, w_br, w_o,
                ln2_g, *, seq_len):
    m, d = x2d.shape
    tm = 256
    halo_per_tile = tm // SUBLANES
    last_halo = m // SUBLANES - 1
    const = dict(pipeline_mode=pl.Buffered(1))
    assert (PLAIN_CC, PLAIN_CX, PLAIN_CB) == (0, 1, 2)

    return pl.pallas_call(
        functools.partial(_merge_kernel, seq_len=seq_len),
        grid=(m // tm,),
        in_specs=[
            pl.BlockSpec((tm, d), lambda i: (i, 0)),
            pl.BlockSpec((1, 1, 6, d), lambda i: (layer, mod_row(i * tm), 0, 0)),
            pl.BlockSpec((tm, BRANCH_W), lambda i: (i, 0)),
            pl.BlockSpec((tm, 3 * BRANCH_W), lambda i: (i, 0)),
            pl.BlockSpec((SUBLANES, 2 * BRANCH_W),
                         lambda i: (jnp.maximum(i * halo_per_tile - 1, 0), 0)),
            pl.BlockSpec((SUBLANES, 2 * BRANCH_W),
                         lambda i: (jnp.minimum((i + 1) * halo_per_tile, last_halo), 0)),
            pl.BlockSpec((tm, BRANCH_W), lambda i: (i, 0)),
            pl.BlockSpec((tm, N_BRANCH * d), lambda i: (i, 0)),
            pl.BlockSpec((1, CONV_K, BRANCH_W), lambda i: (layer, 0, 0)),
            pl.BlockSpec((1, N_BRANCH, BRANCH_W, d), lambda i: (layer, 0, 0, 0), **const),
            pl.BlockSpec((1, d, d), lambda i: (layer, 0, 0), **const),
            pl.BlockSpec((1, 1, d), lambda i: (layer, 0, 0)),
        ],
        out_specs=[pl.BlockSpec((tm, d), lambda i: (i, 0))] * 2,
        out_shape=[jax.ShapeDtypeStruct((m, d), F32), jax.ShapeDtypeStruct((m, d), BF16)],
        compiler_params=_vmem(56 << 20),
        name="merge",
    )(x2d, mod, attn_o, plain, plain, plain, four_o, gates, w_conv, w_br, w_o,
      ln2_g.reshape(ln2_g.shape[0], 1, d))


def _mlp_kernel(x_ref, h_ref, mod_ref, w1_ref, w2_ref, o_ref, *, out_chunk, x_parts):
    j = pl.program_id(1)

    @pl.when(j == 0)
    def _():
        o_ref[...] = jnp.zeros_like(o_ref)

    h = h_ref[...]
    half = w1_ref.shape[2] // 2
    up = lambda s: jnp.maximum(_dot(h, w1_ref[0, :, s * half:(s + 1) * half]), 0.0)
    hid = [up(0), up(1)]
    hid2 = [(v * v).astype(BF16) for v in hid]
    for n0 in range(0, o_ref.shape[1], out_chunk):
        cols = slice(n0, n0 + out_chunk)
        down = (_dot(hid2[0], w2_ref[0, 0:half, cols])
                + _dot(hid2[1], w2_ref[0, half:, cols]))
        o_ref[:, cols] += mod_ref[0, 0, 5:6, cols] * down

    part = j - (pl.num_programs(1) - x_parts)

    @pl.when(part >= 0)
    def _():
        part_rows = x_ref.shape[0]
        rows = pl.ds(pl.multiple_of(part * part_rows, part_rows), part_rows)
        o_ref[rows, :] += x_ref[...]


def _mlp_call(x2d, h2, mod, layer, mod_row, w1, w2):
    m, d = x2d.shape
    d_ff = w1.shape[2]
    tm, tf = 1024, 1024
    steps = d_ff // tf
    x_parts = 4

    def x_rows(i, j):
        part = j - (steps - x_parts)
        return jnp.where(part >= 0, i * x_parts + part, jnp.maximum(i * x_parts - 1, 0)), 0

    return pl.pallas_call(
        functools.partial(_mlp_kernel, out_chunk=512, x_parts=x_parts),
        grid=(m // tm, steps),
        in_specs=[
            pl.BlockSpec((tm // x_parts, d), x_rows),
            pl.BlockSpec((tm, d), lambda i, j: (i, 0)),
            pl.BlockSpec((1, 1, 6, d), lambda i, j: (layer, mod_row(i * tm), 0, 0)),
            pl.BlockSpec((1, d, tf), lambda i, j: (layer, 0, j)),
            pl.BlockSpec((1, tf, d), lambda i, j: (layer, j, 0)),
        ],
        out_specs=pl.BlockSpec((tm, d), lambda i, j: (i, 0)),
        out_shape=jax.ShapeDtypeStruct((m, d), F32),
        compiler_params=_vmem(56 << 20),
        name="mlp",
    )(x2d, h2, mod, w1, w2)


def _rope_tables(seq_len):
    rows = seq_len // GRID_W
    t_row = jnp.repeat(jnp.arange(rows), GRID_W).astype(F32)
    t_col = jnp.tile(jnp.arange(GRID_W), rows).astype(F32)
    axis_dim = HEAD_DIM // 2
    inv_freq = 1.0 / (ROPE_THETA ** (jnp.arange(0, axis_dim, 2, dtype=F32) / axis_dim))
    ang_r = t_row[:, None] * inv_freq[None, :]
    ang_c = t_col[:, None] * inv_freq[None, :]
    cos = jnp.concatenate([jnp.cos(ang_r)] * 2 + [jnp.cos(ang_c)] * 2, axis=-1)
    sin = jnp.concatenate([-jnp.sin(ang_r), jnp.sin(ang_r), -jnp.sin(ang_c), jnp.sin(ang_c)],
                          axis=-1)
    return jnp.tile(cos, (1, LANES // HEAD_DIM)), jnp.tile(sin, (1, LANES // HEAD_DIM))


def _dft_cos_sin(rows, cols, period, scale):
    r = jnp.arange(rows, dtype=jnp.int32)
    c = jnp.arange(cols, dtype=jnp.int32)
    ang = ((r[:, None] * c[None, :]) % period).astype(F32) * (2.0 * math.pi / period)
    return jnp.cos(ang) * scale, jnp.sin(ang) * scale


def kernel(x, c, ctx, c_ctx, ln1_g, ln2_g, w_ada, b_ada, w_in, qn_g, kn_g, lam_q, lam_k,
           subln_g, w_conv, w_br, w_o, w1, w2):
    n_batch, seq, d = x.shape
    ctx_len = ctx.shape[1]
    depth = w_ada.shape[0]
    assert seq == FFT_R * FFT_R and ctx_len == 256 and n_batch <= CTX_MOD_ROW and d == D_MODEL

    c_rows = jnp.zeros((MOD_ROWS, d), F32).at[:n_batch].set(c).at[CTX_MOD_ROW].set(c_ctx)
    mod = _mod_call(c_rows, w_ada, b_ada).reshape(depth, MOD_ROWS, 6, d)

    cos_l, sin_l = _rope_tables(seq)
    cos_c = jnp.ones((n_batch * ctx_len, LANES), F32)
    sin_c = jnp.zeros((n_batch * ctx_len, LANES), F32)
    qg = jnp.tile(qn_g, (1, MXU_W // HEAD_DIM)).reshape(depth, 1, MXU_W)
    kg = jnp.tile(kn_g, (1, MXU_W // HEAD_DIM)).reshape(depth, 1, MXU_W)

    ch_c, ch_s = _dft_cos_sin(GROUP_W, GROUP_W, GROUP_W, GROUP_W ** -0.5)
    chan_cos, chan_sin = ch_c.astype(BF16), ch_s.astype(BF16)
    chan_cs = jnp.concatenate([chan_cos, chan_sin], axis=0)
    pc_c, pc_s = _dft_cos_sin(ctx_len, ctx_len, ctx_len, ctx_len ** -0.5)
    pos_c = jnp.concatenate([pc_c, -pc_s], axis=1).astype(BF16)
    r_c, r_s = _dft_cos_sin(FFT_R, FFT_R, FFT_R, FFT_R ** -0.5)
    eye2 = jnp.eye(2, dtype=F32)
    wa = jnp.kron(jnp.concatenate([r_c, -r_s], axis=0), eye2).astype(BF16)
    wc = jnp.concatenate([jnp.concatenate([r_c, r_s], axis=1),
                          jnp.concatenate([-r_s, r_c], axis=1)], axis=0)
    wc = jnp.einsum("odib,pq->odpqib", wc.reshape(2, FFT_R, 2, FFT_R), eye2)
    wc = wc.reshape(4 * FFT_R, 4 * FFT_R).astype(BF16)
    tw_c, tw_s = _dft_cos_sin(FFT_R, FFT_R, seq, 1.0)
    pair_order = lambda t: jnp.broadcast_to(
        t.reshape(FFT_R // 2, 2, FFT_R).transpose(0, 2, 1).reshape(seq, 1), (seq, LANES))
    tw_cos, tw_sin = pair_order(tw_c), pair_order(tw_s)

    w_br_b = w_br.astype(BF16)
    w_o_b = w_o.astype(BF16)
    w1_b = w1.astype(BF16)
    w2_b = w2.astype(BF16)

    lat_row = lambda row: row // seq
    ctx_row = lambda row: CTX_MOD_ROW
    heads = COL_TILE // LANES

    xl = x.reshape(n_batch * seq, d)
    xc = ctx.reshape(n_batch * ctx_len, d)

    for l in range(depth):
        last = l == depth - 1
        lam_init = 0.8 - 0.6 * math.exp(-0.3 * l)

        hc = _normmod_call(xc, mod, l, ctx_row, ln1_g)
        hl = _normmod_call(xl, mod, l, lat_row, ln1_g)
        qk_c = _proj_call("qk", hc, w_in, l, QK_TILES, (qg, kg, cos_c, sin_c))
        qk_l = _proj_call("qk", hl, w_in, l, QK_TILES, (qg, kg, cos_l, sin_l))
        plain_c = _proj_call("plain", hc, w_in, l, (TILE_V,) if last else PLAIN_TILES)
        plain_l = _proj_call("plain", hl, w_in, l, PLAIN_TILES)
        gates_l = _proj_call("gate", hl, w_in, l, GATE_TILES)

        kv_ctx = (qk_c, heads, plain_c, (0 if last else PLAIN_V) * heads, ctx_len)
        kv_lat = (qk_l, heads, plain_l, PLAIN_V * heads, seq)
        attn_l = _attn_call(lam_q, lam_k, subln_g, l, qk_l, 0, [kv_ctx, kv_lat],
                            n_batch=n_batch, q_len=seq, lam_init=lam_init)
        four_l = _fourier_latent_call(plain_l, wa, wc, tw_cos, tw_sin, chan_cs, n_batch=n_batch)
        xl, h2l = _merge_call(xl, mod, l, lat_row, attn_l, plain_l, gates_l, four_l, w_conv,
                              w_br_b, w_o_b, ln2_g, seq_len=seq)
        xl = _mlp_call(xl, h2l, mod, l, lat_row, w1_b, w2_b)

        if not last:
            gates_c = _proj_call("gate", hc, w_in, l, GATE_TILES)
            attn_c = _attn_call(lam_q, lam_k, subln_g, l, qk_c, 0, [kv_ctx],
                                n_batch=n_batch, q_len=ctx_len, lam_init=lam_init)
            four_c = _fourier_dense_call(plain_c, PLAIN_F, chan_cos, chan_sin, pos_c,
                                         n_batch=n_batch, n_pos=ctx_len)
            xc, h2c = _merge_call(xc, mod, l, ctx_row, attn_c, plain_c, gates_c, four_c, w_conv,
                                  w_br_b, w_o_b, ln2_g, seq_len=ctx_len)
            xc = _mlp_call(xc, h2c, mod, l, ctx_row, w1_b, w2_b)

    return xl.reshape(n_batch, seq, d)
```

```python
import functools
import math

import jax
import jax.numpy as jnp
from jax import lax
from jax.experimental import pallas as pl
from jax.experimental.pallas import tpu as pltpu

D_MODEL = 2048
GRID_W = 64
N_HEADS = 8
HEAD_DIM = 64
V_HEAD_DIM = 128
BRANCH_W = 1024
FOURIER_GROUPS = 4
GROUP_W = BRANCH_W // FOURIER_GROUPS
N_BRANCH = 3
CONV_K = 3
EPS = 1e-6
ROPE_THETA = 10000.0

COL_TILE = 1024
PROJ_ROW_CHUNK = 1024
TILE_Q, TILE_K, TILE_V, TILE_CB, TILE_CC, TILE_CX, TILE_F, TILE_G = range(8)
QK_TILES = (TILE_Q, TILE_K)
PLAIN_TILES = (TILE_CC, TILE_CX, TILE_CB, TILE_V, TILE_F)
PLAIN_CC, PLAIN_CX, PLAIN_CB, PLAIN_V, PLAIN_F = range(5)
GATE_TILES = tuple(range(TILE_G, TILE_G + N_BRANCH * D_MODEL // COL_TILE))

LANES = 128
SUBLANES = 8
MXU_W = 256
ONES_ROWS = 16
LOG2_E = math.log2(math.e)
MOD_ROWS = 8
CTX_MOD_ROW = 4
FFT_R = 64
STRIDE_PAD = 4

F32 = jnp.float32
BF16 = jnp.bfloat16


def _vmem(nbytes):
    return pltpu.CompilerParams(vmem_limit_bytes=nbytes)


def _sigmoid(x):
    return 1.0 / (1.0 + jnp.exp(-x))


def _rms_modulate(x, g, shift, scale):
    y = x * lax.rsqrt(jnp.mean(x * x, axis=-1, keepdims=True) + EPS) * g
    return y * (1.0 + scale) + shift


def _dot(a, b):
    return jnp.dot(a, b, preferred_element_type=F32)


def _dot_t(a, b):
    return lax.dot_general(a, b, (((1,), (1,)), ((), ())), preferred_element_type=F32)


def _mod_kernel(c_ref, w_ref, b_ref, o_ref):
    c = c_ref[...]
    cs = c * _sigmoid(c)
    o_ref[0] = jnp.dot(cs, w_ref[0], preferred_element_type=F32,
                       precision=lax.Precision.HIGHEST) + b_ref[0]


def _mod_call(c_rows, w_ada, b_ada):
    depth, d, n = w_ada.shape
    tn = 1536
    return pl.pallas_call(
        _mod_kernel,
        grid=(depth, n // tn),
        in_specs=[
            pl.BlockSpec((MOD_ROWS, d), lambda l, j: (0, 0)),
            pl.BlockSpec((1, d, tn), lambda l, j: (l, 0, j)),
            pl.BlockSpec((1, 1, tn), lambda l, j: (l, 0, j)),
        ],
        out_specs=pl.BlockSpec((1, MOD_ROWS, tn), lambda l, j: (l, 0, j)),
        out_shape=jax.ShapeDtypeStruct((depth, MOD_ROWS, n), F32),
        compiler_params=_vmem(40 << 20),
        name="mod",
    )(c_rows, w_ada, b_ada.reshape(depth, 1, n))


def _normmod_kernel(x_ref, mod_ref, g_ref, o_ref):
    h = _rms_modulate(x_ref[...], g_ref[0], mod_ref[0, 0, 0:1, :], mod_ref[0, 0, 1:2, :])
    o_ref[...] = h.astype(BF16)


def _normmod_call(x2d, mod, layer, mod_row, ln_g):
    m, d = x2d.shape
    tm = 1024
    return pl.pallas_call(
        _normmod_kernel,
        grid=(m // tm,),
        in_specs=[
            pl.BlockSpec((tm, d), lambda i: (i, 0)),
            pl.BlockSpec((1, 1, 6, d), lambda i: (layer, mod_row(i * tm), 0, 0)),
            pl.BlockSpec((1, 1, d), lambda i: (layer, 0, 0)),
        ],
        out_specs=pl.BlockSpec((tm, d), lambda i: (i, 0)),
        out_shape=jax.ShapeDtypeStruct((m, d), BF16),
        compiler_params=_vmem(32 << 20),
        name="normmod",
    )(x2d, mod, ln_g.reshape(ln_g.shape[0], 1, d))


def _cast_weights(w_ref, wb_ref):
    @pl.when(pl.program_id(1) == 0)
    def _():
        wb_ref[...] = w_ref[0].astype(BF16)


def _project_rows(h_ref, wb_ref, o_ref, epilogue):
    for r in range(0, h_ref.shape[0], PROJ_ROW_CHUNK):
        rows = slice(r, r + PROJ_ROW_CHUNK)
        o_ref[rows, :] = epilogue(_dot(h_ref[rows, :], wb_ref[...])).astype(BF16)


def _proj_plain_kernel(h_ref, w_ref, o_ref, wb_ref):
    _cast_weights(w_ref, wb_ref)
    _project_rows(h_ref, wb_ref, o_ref, lambda acc: acc)


def _proj_gate_kernel(h_ref, w_ref, o_ref, wb_ref):
    _cast_weights(w_ref, wb_ref)
    _project_rows(h_ref, wb_ref, o_ref, lambda acc: 0.5 * jnp.tanh(0.5 * acc) + 0.5)


def _proj_qk_kernel(h_ref, w_ref, qg_ref, kg_ref, cos_ref, sin_ref, o_ref, wb_ref):
    _cast_weights(w_ref, wb_ref)
    items = [(r, t) for r in range(0, h_ref.shape[0], PROJ_ROW_CHUNK)
             for t in range(COL_TILE // MXU_W)]
    project = lambda r, t: _dot(h_ref[r:r + PROJ_ROW_CHUNK, :],
                                wb_ref[:, t * MXU_W:(t + 1) * MXU_W])
    is_q = pl.program_id(0) == TILE_Q
    gain = jnp.where(is_q, qg_ref[0] * (LOG2_E * HEAD_DIM ** -0.5), kg_ref[0])
    lane = lax.broadcasted_iota(jnp.int32, (1, LANES), 1)
    first_half = (lane % (HEAD_DIM // 2)) < (HEAD_DIM // 4)
    gsum = (lax.broadcasted_iota(jnp.int32, (MXU_W, MXU_W), 0) // HEAD_DIM
            == lax.broadcasted_iota(jnp.int32, (MXU_W, MXU_W), 1) // HEAD_DIM).astype(BF16)
    a_next = project(*items[0])
    for idx, (r, t) in enumerate(items):
        a = a_next
        if idx + 1 < len(items):
            a_next = project(*items[idx + 1])
        rows = slice(r, r + PROJ_ROW_CHUNK)
        cos = cos_ref[rows, :]
        sin = sin_ref[rows, :]
        ssq = _dot((a * a).astype(BF16), gsum)
        y = a * lax.rsqrt(ssq * (1.0 / HEAD_DIM) + EPS) * gain
        for u in range(MXU_W // LANES):
            yh = y[:, u * LANES:(u + 1) * LANES]
            partner = jnp.where(first_half,
                                pltpu.roll(yh, LANES - HEAD_DIM // 4, 1),
                                pltpu.roll(yh, HEAD_DIM // 4, 1))
            c0 = t * MXU_W + u * LANES
            o_ref[rows, c0:c0 + LANES] = (yh * cos + partner * sin).astype(BF16)


def _proj_call(kind, h, w_in, layer, w_tiles, extra=()):
    m, d = h.shape
    tm = 2 * PROJ_ROW_CHUNK if m % (2 * PROJ_ROW_CHUNK) == 0 else PROJ_ROW_CHUNK
    n_tiles = len(w_tiles)

    def w_tile(j):
        tile = w_tiles[0]
        for idx in range(1, n_tiles):
            tile = jnp.where(j == idx, w_tiles[idx], tile)
        return tile

    in_specs = [
        pl.BlockSpec((tm, d), lambda j, i: (i, 0)),
        pl.BlockSpec((1, d, COL_TILE), lambda j, i: (layer, 0, w_tile(j))),
    ]
    args = [h, w_in]
    if kind == "qk":
        qg, kg, cos_t, sin_t = extra
        seq_tiles = cos_t.shape[0] // tm
        in_specs += [
            pl.BlockSpec((1, 1, MXU_W), lambda j, i: (layer, 0, 0)),
            pl.BlockSpec((1, 1, MXU_W), lambda j, i: (layer, 0, 0)),
            pl.BlockSpec((tm, LANES), lambda j, i: (i % seq_tiles, 0)),
            pl.BlockSpec((tm, LANES), lambda j, i: (i % seq_tiles, 0)),
        ]
        args += [qg, kg, cos_t, sin_t]
    body = {"qk": _proj_qk_kernel, "plain": _proj_plain_kernel, "gate": _proj_gate_kernel}[kind]
    return pl.pallas_call(
        body,
        grid=(n_tiles, m // tm),
        in_specs=in_specs,
        out_specs=pl.BlockSpec((tm, COL_TILE), lambda j, i: (i, j)),
        out_shape=jax.ShapeDtypeStruct((m, n_tiles * COL_TILE), BF16),
        scratch_shapes=[pltpu.VMEM((d, COL_TILE), BF16)],
        compiler_params=_vmem(56 << 20),
        name="proj_" + kind,
    )(*args)


def _attn_kernel(lq_ref, lk_ref, sg_ref, q_ref, *refs, lam_init, pieces, key_block):
    n = len(pieces)
    k_refs, v_refs = refs[0:2 * n:2], refs[1:2 * n:2]
    o_ref, vt_ref, kmax_ref = refs[2 * n:]
    tq = q_ref.shape[0]
    lane = lax.broadcasted_iota(jnp.int32, (1, LANES), 1)

    @pl.when(pl.program_id(2) == 0)
    def _():
        group = lax.broadcasted_iota(jnp.int32, (LANES, LANES), 0) // HEAD_DIM
        gsum = (group == lane // HEAD_DIM).astype(BF16)
        kmax = None
        off = 0
        for t in range(n):
            rows = pieces[t]
            vt_ref[0:V_HEAD_DIM, off:off + rows] = v_refs[t][...].astype(F32).T.astype(BF16)
            kf = k_refs[t][...].astype(F32)
            norms = _dot((kf * kf).astype(BF16), gsum)
            piece_max = jnp.max(norms, axis=0, keepdims=True)
            kmax = piece_max if kmax is None else jnp.maximum(kmax, piece_max)
            off += rows
        vt_ref[V_HEAD_DIM:, :] = jnp.ones((ONES_ROWS, off), BF16)
        kmax_ref[...] = kmax

    e = jnp.exp(jnp.sum(lq_ref[0] * lk_ref[0], axis=-1, keepdims=True))
    lam = e[0:1, :] - e[1:2, :] + lam_init

    q = q_ref[...]
    zero = jnp.zeros_like(q)
    lo = lane < HEAD_DIM
    q_both = jnp.concatenate([jnp.where(lo, q, zero), jnp.where(lo, zero, q)], axis=0)

    qf = q.astype(F32)
    sel = (lax.broadcasted_iota(jnp.int32, (SUBLANES, LANES), 0) == lane // HEAD_DIM).astype(BF16)
    q_norms = _dot_t(sel, (qf * qf).astype(BF16))
    kmax = kmax_ref[...]
    bounds = jnp.concatenate(
        [jnp.sqrt(q_norms[c:c + 1, :] * kmax[:, c * HEAD_DIM:c * HEAD_DIM + 1]) * 1.01 + 1e-6
         for c in range(2)], axis=1)

    def key_blocks():
        off = 0
        for t in range(n):
            for s in range(0, pieces[t], key_block):
                size = min(key_block, pieces[t] - s)
                yield k_refs[t], s, off + s, size
            off += pieces[t]

    def accumulate(shifts):
        blocks = list(key_blocks())
        scores = lambda blk: _dot_t(blk[0][blk[1]:blk[1] + blk[3], :], q_both)
        acc = jnp.zeros((V_HEAD_DIM + ONES_ROWS, 2 * tq), F32)
        st_next = scores(blocks[0])
        for idx, (_, _, pos, size) in enumerate(blocks):
            st = st_next
            if idx + 1 < len(blocks):
                st_next = scores(blocks[idx + 1])
            acc = acc + _dot(vt_ref[:, pos:pos + size], jnp.exp2(st - shifts).astype(BF16))
        return acc

    def finish(acc):
        sums = acc[V_HEAD_DIM:V_HEAD_DIM + 1, :]
        ot = (acc[0:V_HEAD_DIM, 0:tq] * (1.0 / sums[:, 0:tq])
              - acc[0:V_HEAD_DIM, tq:] * (lam / sums[:, tq:]))
        yt = ot * lax.rsqrt(jnp.mean(ot * ot, axis=0, keepdims=True) + EPS)
        o_ref[...] = (yt.T * (sg_ref[0] * (1.0 - lam_init))).astype(BF16)

    acc = accumulate(bounds)
    finish(acc)

    @pl.when(jnp.logical_not(jnp.min(acc[V_HEAD_DIM:V_HEAD_DIM + 1, :]) >= 1e-30))
    def _():
        m = None
        for k_ref, s, _, size in key_blocks():
            bm = jnp.max(_dot_t(k_ref[s:s + size, :], q_both), axis=0, keepdims=True)
            m = bm if m is None else jnp.maximum(m, bm)
        finish(accumulate(m))


def _attn_call(lam_q, lam_k, subln_g, layer, q_arr, q_tile0, kv_list, *, n_batch, q_len,
               lam_init):
    tq = min(1024, q_len)
    q_tiles = q_len // tq
    in_specs = [
        pl.BlockSpec((1, 2, HEAD_DIM), lambda b, h, i: (layer, 0, 0)),
        pl.BlockSpec((1, 2, HEAD_DIM), lambda b, h, i: (layer, 0, 0)),
        pl.BlockSpec((1, 1, V_HEAD_DIM), lambda b, h, i: (layer, 0, 0)),
        pl.BlockSpec((tq, LANES), lambda b, h, i: (b * q_tiles + i, q_tile0 + h)),
    ]
    args = [lam_q, lam_k, subln_g.reshape(subln_g.shape[0], 1, V_HEAD_DIM), q_arr]
    for k_arr, k0, v_arr, v0, rows in kv_list:
        in_specs.append(pl.BlockSpec((rows, LANES), lambda b, h, i, k0=k0: (b, k0 + h)))
        in_specs.append(pl.BlockSpec((rows, LANES), lambda b, h, i, v0=v0: (b, v0 + h)))
        args += [k_arr, v_arr]
    pieces = tuple(kv[4] for kv in kv_list)
    return pl.pallas_call(
        functools.partial(_attn_kernel, lam_init=lam_init, pieces=pieces, key_block=512),
        grid=(n_batch, N_HEADS, q_tiles),
        in_specs=in_specs,
        out_specs=pl.BlockSpec((tq, LANES), lambda b, h, i: (b * q_tiles + i, h)),
        out_shape=jax.ShapeDtypeStruct((n_batch * q_len, N_HEADS * V_HEAD_DIM), BF16),
        scratch_shapes=[pltpu.VMEM((V_HEAD_DIM + ONES_ROWS, sum(pieces)), BF16),
                        pltpu.VMEM((1, LANES), F32)],
        compiler_params=_vmem(52 << 20),
        name="attn",
    )(*args)


def _fourier_dense_kernel(x_ref, cc_ref, sc_ref, pos_ref, o_ref, y_ref, *, n_pos):
    @pl.when(pl.program_id(1) == 0)
    def _():
        cc = cc_ref[...]
        sc = sc_ref[...]
        for g in range(FOURIER_GROUPS):
            xg = x_ref[:, g * GROUP_W:(g + 1) * GROUP_W]
            y_ref[0:n_pos, g * GROUP_W:(g + 1) * GROUP_W] = _dot(xg, cc).astype(BF16)
            y_ref[n_pos:2 * n_pos, g * GROUP_W:(g + 1) * GROUP_W] = _dot(xg, sc).astype(BF16)

    o_ref[...] = _dot(pos_ref[...], y_ref[...]).astype(BF16)


def _fourier_dense_call(p_arr, f_tile, chan_cos, chan_sin, pos_mat, *, n_batch, n_pos):
    tm = min(256, n_pos)
    row_tiles = n_pos // tm
    return pl.pallas_call(
        functools.partial(_fourier_dense_kernel, n_pos=n_pos),
        grid=(n_batch, row_tiles),
        in_specs=[
            pl.BlockSpec((n_pos, BRANCH_W), lambda b, i: (b, f_tile),
                         pipeline_mode=pl.Buffered(1)),
            pl.BlockSpec((GROUP_W, GROUP_W), lambda b, i: (0, 0)),
            pl.BlockSpec((GROUP_W, GROUP_W), lambda b, i: (0, 0)),
            pl.BlockSpec((tm, 2 * n_pos), lambda b, i: (i, 0)),
        ],
        out_specs=pl.BlockSpec((tm, BRANCH_W), lambda b, i: (b * row_tiles + i, 0)),
        out_shape=jax.ShapeDtypeStruct((n_batch * n_pos, BRANCH_W), BF16),
        scratch_shapes=[pltpu.VMEM((2 * n_pos, BRANCH_W), BF16)],
        compiler_params=_vmem(32 << 20),
        name="fourier_dense",
    )(p_arr, chan_cos, chan_sin, pos_mat)


def _fourier_latent_kernel(x_ref, wa_ref, wc_ref, twc_ref, tws_ref, cs_ref, o_ref,
                           xs_ref, ar_ref, ai_ref, zr_ref, zi_ref):
    tiles = GROUP_W // LANES
    half = FFT_R // 2
    lane_tile = lambda v, t: v[:, t * LANES:(t + 1) * LANES]
    pack = lambda v: pltpu.bitcast(v.astype(BF16), jnp.int32)

    def gather(ref, rows):
        words = jnp.concatenate([ref[t, rows, :] for t in range(tiles)], axis=1)
        return pltpu.bitcast(words, BF16)

    def scatter(ref, rows, v):
        words = pack(v)
        for t in range(tiles):
            ref[t, rows, :] = lane_tile(words, t)

    pitch_h = half + STRIDE_PAD
    pitch_r = FFT_R + STRIDE_PAD

    for a in range(FFT_R):
        words = pltpu.bitcast(x_ref[a * FFT_R:(a + 1) * FFT_R, :], jnp.int32)
        for t in range(tiles):
            xs_ref[t, a * pitch_h:a * pitch_h + half, :] = lane_tile(words, t)
    wa = wa_ref[...]
    wc = wc_ref[...]

    def stage1(beta, carry):
        xb = gather(xs_ref, pl.ds(beta, FFT_R, stride=pitch_h))
        a = _dot(wa, xb)
        tw_rows = pl.ds(pl.multiple_of(beta * 2 * FFT_R, 2 * FFT_R), 2 * FFT_R)
        tc = twc_ref[tw_rows, :]
        ts = tws_ref[tw_rows, :]
        re, im = a[0:2 * FFT_R], a[2 * FFT_R:]
        tc = jnp.concatenate([tc] * tiles, axis=1)
        ts = jnp.concatenate([ts] * tiles, axis=1)
        rows = pl.ds(beta * pitch_r, FFT_R)
        scatter(ar_ref, rows, re * tc + im * ts)
        scatter(ai_ref, rows, im * tc - re * ts)
        return carry

    lax.fori_loop(0, half, stage1, 0, unroll=4)

    def stage2(gamma, carry):
        parts = []
        for p in range(2):
            rows = pl.ds(2 * gamma + p, half, stride=pitch_r)
            parts += [gather(ar_ref, rows), gather(ai_ref, rows)]
        z = _dot(wc, jnp.concatenate(parts, axis=0))
        rows = pl.ds(gamma, FFT_R, stride=pitch_h)
        scatter(zr_ref, rows, z[0:2 * FFT_R])
        scatter(zi_ref, rows, z[2 * FFT_R:])
        return carry

    lax.fori_loop(0, half, stage2, 0, unroll=4)

    cs = cs_ref[...]
    d_per_block = 8
    for d0 in range(0, FFT_R, d_per_block):
        z = jnp.concatenate(
            [jnp.concatenate([gather(ref, slice(d * pitch_h, d * pitch_h + half))
                              for ref in (zr_ref, zi_ref)], axis=1)
             for d in range(d0, d0 + d_per_block)], axis=0)
        o_ref[d0 * FFT_R:(d0 + d_per_block) * FFT_R, :] = _dot(z, cs).astype(BF16)


def _fourier_latent_call(plain, wa, wc, tw_cos, tw_sin, chan_cs, *, n_batch):
    n_pos = FFT_R * FFT_R
    tiles = GROUP_W // LANES
    const = lambda shape: pl.BlockSpec(shape, lambda n, q: (0,) * len(shape))
    half = FFT_R // 2
    by_half = pltpu.VMEM((tiles, FFT_R * (half + STRIDE_PAD), LANES), jnp.int32)
    by_r = pltpu.VMEM((tiles, half * (FFT_R + STRIDE_PAD), LANES), jnp.int32)
    return pl.pallas_call(
        _fourier_latent_kernel,
        grid=(n_batch, FOURIER_GROUPS),
        in_specs=[
            pl.BlockSpec((n_pos, GROUP_W), lambda n, q: (n, PLAIN_F * FOURIER_GROUPS + q)),
            const((4 * FFT_R, 2 * FFT_R)),
            const((4 * FFT_R, 4 * FFT_R)),
            const((n_pos, LANES)),
            const((n_pos, LANES)),
            const((2 * GROUP_W, GROUP_W)),
        ],
        out_specs=pl.BlockSpec((n_pos, GROUP_W), lambda n, q: (n, q)),
        out_shape=jax.ShapeDtypeStruct((n_batch * n_pos, BRANCH_W), BF16),
        scratch_shapes=[by_half, by_r, by_r, by_half, by_half],
        compiler_params=_vmem(48 << 20),
        name="fourier_latent",
    )(plain, wa, wc, tw_cos, tw_sin, chan_cs)


def _merge_kernel(x_ref, mod_ref, attn_ref, conv_ref, before_ref, after_ref, four_ref, gate_ref,
                  wconv_ref, wbr_ref, wo_ref, ln2_ref, o_ref, h2_ref, *, seq_len):
    tm, d = x_ref.shape
    i = pl.program_id(0)
    pos0 = (i * tm) % seq_len
    cc_cols = slice(PLAIN_CC * BRANCH_W, (PLAIN_CC + 1) * BRANCH_W)
    cx_cols = slice(PLAIN_CX * BRANCH_W, (PLAIN_CX + 1) * BRANCH_W)
    cb_cols = slice(PLAIN_CB * BRANCH_W, (PLAIN_CB + 1) * BRANCH_W)

    u = conv_ref[:, cc_cols].astype(F32) * conv_ref[:, cx_cols].astype(F32)
    last = slice(SUBLANES - 1, SUBLANES)
    u_before = before_ref[last, cc_cols].astype(F32) * before_ref[last, cx_cols].astype(F32)
    u_after = after_ref[0:1, cc_cols].astype(F32) * after_ref[0:1, cx_cols].astype(F32)
    u_before = jnp.where(pos0 == 0, 0.0, u_before)
    u_after = jnp.where(pos0 + tm == seq_len, 0.0, u_after)
    row = lax.broadcasted_iota(jnp.int32, (tm, 1), 0)
    u_prev = jnp.where(row == 0, u_before, pltpu.roll(u, 1, 0))
    u_next = jnp.where(row == tm - 1, u_after, pltpu.roll(u, tm - 1, 0))
    w = wconv_ref[0]
    conv = (conv_ref[:, cb_cols].astype(F32)
            * (u_prev * w[0:1, :] + u * w[1:2, :] + u_next * w[2:3, :]))

    gate = lambda branch: gate_ref[:, branch * d:(branch + 1) * d].astype(F32)
    mix = (gate(0) * _dot(attn_ref[...], wbr_ref[0, 0])
           + gate(1) * _dot(conv.astype(BF16), wbr_ref[0, 1])
           + gate(2) * _dot(four_ref[...], wbr_ref[0, 2]))
    y = _dot(mix.astype(BF16), wo_ref[0])
    x_new = x_ref[...] + mod_ref[0, 0, 2:3, :] * y
    o_ref[...] = x_new
    h2 = _rms_modulate(x_new, ln2_ref[0], mod_ref[0, 0, 3:4, :], mod_ref[0, 0, 4:5, :])
    h2_ref[...] = h2.astype(BF16)


def _merge_call(x2d, mod, layer, mod_row, attn_o, plain, gates, four_o, w_conv, w_br, w_o,
                ln2_g, *, seq_len):
    m, d = x2d.shape
    tm = 256
    halo_per_tile = tm // SUBLANES
    last_halo = m // SUBLANES - 1
    const = dict(pipeline_mode=pl.Buffered(1))
    assert (PLAIN_CC, PLAIN_CX, PLAIN_CB) == (0, 1, 2)

    return pl.pallas_call(
        functools.partial(_merge_kernel, seq_len=seq_len),
        grid=(m // tm,),
        in_specs=[
            pl.BlockSpec((tm, d), lambda i: (i, 0)),
            pl.BlockSpec((1, 1, 6, d), lambda i: (layer, mod_row(i * tm), 0, 0)),
            pl.BlockSpec((tm, BRANCH_W), lambda i: (i, 0)),
            pl.BlockSpec((tm, 3 * BRANCH_W), lambda i: (i, 0)),
            pl.BlockSpec((SUBLANES, 2 * BRANCH_W),
                         lambda i: (jnp.maximum(i * halo_per_tile - 1, 0), 0)),
            pl.BlockSpec((SUBLANES, 2 * BRANCH_W),
                         lambda i: (jnp.minimum((i + 1) * halo_per_tile, last_halo), 0)),
            pl.BlockSpec((tm, BRANCH_W), lambda i: (i, 0)),
            pl.BlockSpec((tm, N_BRANCH * d), lambda i: (i, 0)),
            pl.BlockSpec((1, CONV_K, BRANCH_W), lambda i: (layer, 0, 0)),
            pl.BlockSpec((1, N_BRANCH, BRANCH_W, d), lambda i: (layer, 0, 0, 0), **const),
            pl.BlockSpec((1, d, d), lambda i: (layer, 0, 0), **const),
            pl.BlockSpec((1, 1, d), lambda i: (layer, 0, 0)),
        ],
        out_specs=[pl.BlockSpec((tm, d), lambda i: (i, 0))] * 2,
        out_shape=[jax.ShapeDtypeStruct((m, d), F32), jax.ShapeDtypeStruct((m, d), BF16)],
        compiler_params=_vmem(56 << 20),
        name="merge",
    )(x2d, mod, attn_o, plain, plain, plain, four_o, gates, w_conv, w_br, w_o,
      ln2_g.reshape(ln2_g.shape[0], 1, d))


def _mlp_kernel(x_ref, h_ref, mod_ref, w1_ref, w2_ref, o_ref, *, out_chunk, x_parts):
    j = pl.program_id(1)

    @pl.when(j == 0)
    def _():
        o_ref[...] = jnp.zeros_like(o_ref)

    h = h_ref[...]
    half = w1_ref.shape[2] // 2
    up = lambda s: jnp.maximum(_dot(h, w1_ref[0, :, s * half:(s + 1) * half]), 0.0)
    hid = [up(0), up(1)]
    hid2 = [(v * v).astype(BF16) for v in hid]
    for n0 in range(0, o_ref.shape[1], out_chunk):
        cols = slice(n0, n0 + out_chunk)
        down = (_dot(hid2[0], w2_ref[0, 0:half, cols])
                + _dot(hid2[1], w2_ref[0, half:, cols]))
        o_ref[:, cols] += mod_ref[0, 0, 5:6, cols] * down

    part = j - (pl.num_programs(1) - x_parts)

    @pl.when(part >= 0)
    def _():
        part_rows = x_ref.shape[0]
        rows = pl.ds(pl.multiple_of(part * part_rows, part_rows), part_rows)
        o_ref[rows, :] += x_ref[...]


def _mlp_call(x2d, h2, mod, layer, mod_row, w1, w2):
    m, d = x2d.shape
    d_ff = w1.shape[2]
    tm, tf = 1024, 1024
    steps = d_ff // tf
    x_parts = 4

    def x_rows(i, j):
        part = j - (steps - x_parts)
        return jnp.where(part >= 0, i * x_parts + part, jnp.maximum(i * x_parts - 1, 0)), 0

    return pl.pallas_call(
        functools.partial(_mlp_kernel, out_chunk=512, x_parts=x_parts),
        grid=(m // tm, steps),
        in_specs=[
            pl.BlockSpec((tm // x_parts, d), x_rows),
            pl.BlockSpec((tm, d), lambda i, j: (i, 0)),
            pl.BlockSpec((1, 1, 6, d), lambda i, j: (layer, mod_row(i * tm), 0, 0)),
            pl.BlockSpec((1, d, tf), lambda i, j: (layer, 0, j)),
            pl.BlockSpec((1, tf, d), lambda i, j: (layer, j, 0)),
        ],
        out_specs=pl.BlockSpec((tm, d), lambda i, j: (i, 0)),
        out_shape=jax.ShapeDtypeStruct((m, d), F32),
        compiler_params=_vmem(56 << 20),
        name="mlp",
    )(x2d, h2, mod, w1, w2)


def _rope_tables(seq_len):
    rows = seq_len // GRID_W
    t_row = jnp.repeat(jnp.arange(rows), GRID_W).astype(F32)
    t_col = jnp.tile(jnp.arange(GRID_W), rows).astype(F32)
    axis_dim = HEAD_DIM // 2
    inv_freq = 1.0 / (ROPE_THETA ** (jnp.arange(0, axis_dim, 2, dtype=F32) / axis_dim))
    ang_r = t_row[:, None] * inv_freq[None, :]
    ang_c = t_col[:, None] * inv_freq[None, :]
    cos = jnp.concatenate([jnp.cos(ang_r)] * 2 + [jnp.cos(ang_c)] * 2, axis=-1)
    sin = jnp.concatenate([-jnp.sin(ang_r), jnp.sin(ang_r), -jnp.sin(ang_c), jnp.sin(ang_c)],
                          axis=-1)
    return jnp.tile(cos, (1, LANES // HEAD_DIM)), jnp.tile(sin, (1, LANES // HEAD_DIM))


def _dft_cos_sin(rows, cols, period, scale):
    r = jnp.arange(rows, dtype=jnp.int32)
    c = jnp.arange(cols, dtype=jnp.int32)
    ang = ((r[:, None] * c[None, :]) % period).astype(F32) * (2.0 * math.pi / period)
    return jnp.cos(ang) * scale, jnp.sin(ang) * scale


def kernel(x, c, ctx, c_ctx, ln1_g, ln2_g, w_ada, b_ada, w_in, qn_g, kn_g, lam_q, lam_k,
           subln_g, w_conv, w_br, w_o, w1, w2):
    n_batch, seq, d = x.shape
    ctx_len = ctx.shape[1]
    depth = w_ada.shape[0]
    assert seq == FFT_R * FFT_R and ctx_len == 256 and n_batch <= CTX_MOD_ROW and d == D_MODEL

    c_rows = jnp.zeros((MOD_ROWS, d), F32).at[:n_batch].set(c).at[CTX_MOD_ROW].set(c_ctx)
    mod = _mod_call(c_rows, w_ada, b_ada).reshape(depth, MOD_ROWS, 6, d)

    cos_l, sin_l = _rope_tables(seq)
    cos_c = jnp.ones((n_batch * ctx_len, LANES), F32)
    sin_c = jnp.zeros((n_batch * ctx_len, LANES), F32)
    qg = jnp.tile(qn_g, (1, MXU_W // HEAD_DIM)).reshape(depth, 1, MXU_W)
    kg = jnp.tile(kn_g, (1, MXU_W // HEAD_DIM)).reshape(depth, 1, MXU_W)

    ch_c, ch_s = _dft_cos_sin(GROUP_W, GROUP_W, GROUP_W, GROUP_W ** -0.5)
    chan_cos, chan_sin = ch_c.astype(BF16), ch_s.astype(BF16)
    chan_cs = jnp.concatenate([chan_cos, chan_sin], axis=0)
    pc_c, pc_s = _dft_cos_sin(ctx_len, ctx_len, ctx_len, ctx_len ** -0.5)
    pos_c = jnp.concatenate([pc_c, -pc_s], axis=1).astype(BF16)
    r_c, r_s = _dft_cos_sin(FFT_R, FFT_R, FFT_R, FFT_R ** -0.5)
    eye2 = jnp.eye(2, dtype=F32)
    wa = jnp.kron(jnp.concatenate([r_c, -r_s], axis=0), eye2).astype(BF16)
    wc = jnp.concatenate([jnp.concatenate([r_c, r_s], axis=1),
                          jnp.concatenate([-r_s, r_c], axis=1)], axis=0)
    wc = jnp.einsum("odib,pq->odpqib", wc.reshape(2, FFT_R, 2, FFT_R), eye2)
    wc = wc.reshape(4 * FFT_R, 4 * FFT_R).astype(BF16)
    tw_c, tw_s = _dft_cos_sin(FFT_R, FFT_R, seq, 1.0)
    pair_order = lambda t: jnp.broadcast_to(
        t.reshape(FFT_R // 2, 2, FFT_R).transpose(0, 2, 1).reshape(seq, 1), (seq, LANES))
    tw_cos, tw_sin = pair_order(tw_c), pair_order(tw_s)

    w_br_b = w_br.astype(BF16)
    w_o_b = w_o.astype(BF16)
    w1_b = w1.astype(BF16)
    w2_b = w2.astype(BF16)

    lat_row = lambda row: row // seq
    ctx_row = lambda row: CTX_MOD_ROW
    heads = COL_TILE // LANES

    xl = x.reshape(n_batch * seq, d)
    xc = ctx.reshape(n_batch * ctx_len, d)

    for l in range(depth):
        last = l == depth - 1
        lam_init = 0.8 - 0.6 * math.exp(-0.3 * l)

        hc = _normmod_call(xc, mod, l, ctx_row, ln1_g)
        hl = _normmod_call(xl, mod, l, lat_row, ln1_g)
        qk_c = _proj_call("qk", hc, w_in, l, QK_TILES, (qg, kg, cos_c, sin_c))
        qk_l = _proj_call("qk", hl, w_in, l, QK_TILES, (qg, kg, cos_l, sin_l))
        plain_c = _proj_call("plain", hc, w_in, l, (TILE_V,) if last else PLAIN_TILES)
        plain_l = _proj_call("plain", hl, w_in, l, PLAIN_TILES)
        gates_l = _proj_call("gate", hl, w_in, l, GATE_TILES)

        kv_ctx = (qk_c, heads, plain_c, (0 if last else PLAIN_V) * heads, ctx_len)
        kv_lat = (qk_l, heads, plain_l, PLAIN_V * heads, seq)
        attn_l = _attn_call(lam_q, lam_k, subln_g, l, qk_l, 0, [kv_ctx, kv_lat],
                            n_batch=n_batch, q_len=seq, lam_init=lam_init)
        four_l = _fourier_latent_call(plain_l, wa, wc, tw_cos, tw_sin, chan_cs, n_batch=n_batch)
        xl, h2l = _merge_call(xl, mod, l, lat_row, attn_l, plain_l, gates_l, four_l, w_conv,
                              w_br_b, w_o_b, ln2_g, seq_len=seq)
        xl = _mlp_call(xl, h2l, mod, l, lat_row, w1_b, w2_b)

        if not last:
            gates_c = _proj_call("gate", hc, w_in, l, GATE_TILES)
            attn_c = _attn_call(lam_q, lam_k, subln_g, l, qk_c, 0, [kv_ctx],
                                n_batch=n_batch, q_len=ctx_len, lam_init=lam_init)
            four_c = _fourier_dense_call(plain_c, PLAIN_F, chan_cos, chan_sin, pos_c,
                                         n_batch=n_batch, n_pos=ctx_len)
            xc, h2c = _merge_call(xc, mod, l, ctx_row, attn_c, plain_c, gates_c, four_c, w_conv,
                                  w_br_b, w_o_b, ln2_g, seq_len=ctx_len)
            xc = _mlp_call(xc, h2c, mod, l, ctx_row, w1_b, w2_b)

    return xl.reshape(n_batch, seq, d)
```

```python
import functools
import math

import jax
import jax.numpy as jnp
from jax import lax
from jax.experimental import pallas as pl
from jax.experimental.pallas import tpu as pltpu

D_MODEL = 2048
GRID_W = 64
N_HEADS = 8
HEAD_DIM = 64
V_HEAD_DIM = 128
BRANCH_W = 1024
FOURIER_GROUPS = 4
GROUP_W = BRANCH_W // FOURIER_GROUPS
N_BRANCH = 3
CONV_K = 3
EPS = 1e-6
ROPE_THETA = 10000.0

COL_TILE = 1024
PROJ_ROW_CHUNK = 1024
TILE_Q, TILE_K, TILE_V, TILE_CB, TILE_CC, TILE_CX, TILE_F, TILE_G = range(8)
QK_TILES = (TILE_Q, TILE_K)
PLAIN_TILES = (TILE_CC, TILE_CX, TILE_CB, TILE_V, TILE_F)
PLAIN_CC, PLAIN_CX, PLAIN_CB, PLAIN_V, PLAIN_F = range(5)
GATE_TILES = tuple(range(TILE_G, TILE_G + N_BRANCH * D_MODEL // COL_TILE))

LANES = 128
SUBLANES = 8
MXU_W = 256
ONES_ROWS = 16
LOG2_E = math.log2(math.e)
MOD_ROWS = 8
CTX_MOD_ROW = 4
FFT_R = 64
STRIDE_PAD = 4

F32 = jnp.float32
BF16 = jnp.bfloat16


def _vmem(nbytes):
    return pltpu.CompilerParams(vmem_limit_bytes=nbytes)


def _sigmoid(x):
    return 1.0 / (1.0 + jnp.exp(-x))


def _rms_modulate(x, g, shift, scale):
    y = x * lax.rsqrt(jnp.mean(x * x, axis=-1, keepdims=True) + EPS) * g
    return y * (1.0 + scale) + shift


def _dot(a, b):
    return jnp.dot(a, b, preferred_element_type=F32)


def _dot_t(a, b):
    return lax.dot_general(a, b, (((1,), (1,)), ((), ())), preferred_element_type=F32)


def _mod_kernel(c_ref, w_ref, b_ref, o_ref):
    c = c_ref[...]
    cs = c * _sigmoid(c)
    split = lambda v: (v.astype(BF16), (v - v.astype(BF16).astype(F32)).astype(BF16))
    cs_hi, cs_lo = split(cs)
    w_hi, w_lo = split(w_ref[0])
    o_ref[0] = (_dot(cs_hi, w_hi) + (_dot(cs_lo, w_hi) + _dot(cs_hi, w_lo))) + b_ref[0]


def _mod_call(c_rows, w_ada, b_ada):
    depth, d, n = w_ada.shape
    tn = 1536
    return pl.pallas_call(
        _mod_kernel,
        grid=(depth, n // tn),
        in_specs=[
            pl.BlockSpec((MOD_ROWS, d), lambda l, j: (0, 0)),
            pl.BlockSpec((1, d, tn), lambda l, j: (l, 0, j)),
            pl.BlockSpec((1, 1, tn), lambda l, j: (l, 0, j)),
        ],
        out_specs=pl.BlockSpec((1, MOD_ROWS, tn), lambda l, j: (l, 0, j)),
        out_shape=jax.ShapeDtypeStruct((depth, MOD_ROWS, n), F32),
        compiler_params=_vmem(40 << 20),
        name="mod",
    )(c_rows, w_ada, b_ada.reshape(depth, 1, n))


def _normmod_kernel(x_ref, mod_ref, g_ref, o_ref):
    h = _rms_modulate(x_ref[...], g_ref[0], mod_ref[0, 0, 0:1, :], mod_ref[0, 0, 1:2, :])
    o_ref[...] = h.astype(BF16)


def _normmod_call(x2d, mod, layer, mod_row, ln_g):
    m, d = x2d.shape
    tm = 1024
    return pl.pallas_call(
        _normmod_kernel,
        grid=(m // tm,),
        in_specs=[
            pl.BlockSpec((tm, d), lambda i: (i, 0)),
            pl.BlockSpec((1, 1, 6, d), lambda i: (layer, mod_row(i * tm), 0, 0)),
            pl.BlockSpec((1, 1, d), lambda i: (layer, 0, 0)),
        ],
        out_specs=pl.BlockSpec((tm, d), lambda i: (i, 0)),
        out_shape=jax.ShapeDtypeStruct((m, d), BF16),
        compiler_params=_vmem(32 << 20),
        name="normmod",
    )(x2d, mod, ln_g.reshape(ln_g.shape[0], 1, d))


def _cast_weights(w_ref, wb_ref):
    @pl.when(pl.program_id(1) == 0)
    def _():
        wb_ref[...] = w_ref[0].astype(BF16)


def _project_rows(h_ref, wb_ref, o_ref, epilogue):
    for r in range(0, h_ref.shape[0], PROJ_ROW_CHUNK):
        rows = slice(r, r + PROJ_ROW_CHUNK)
        o_ref[rows, :] = epilogue(_dot(h_ref[rows, :], wb_ref[...])).astype(BF16)


def _proj_plain_kernel(h_ref, w_ref, o_ref, wb_ref):
    _cast_weights(w_ref, wb_ref)
    _project_rows(h_ref, wb_ref, o_ref, lambda acc: acc)


def _proj_gate_kernel(h_ref, w_ref, o_ref, wb_ref):
    _cast_weights(w_ref, wb_ref)
    _project_rows(h_ref, wb_ref, o_ref, lambda acc: 0.5 * jnp.tanh(0.5 * acc) + 0.5)


def _proj_qk_kernel(h_ref, w_ref, qg_ref, kg_ref, cos_ref, sin_ref, o_ref, wb_ref):
    _cast_weights(w_ref, wb_ref)
    items = [(r, t) for r in range(0, h_ref.shape[0], PROJ_ROW_CHUNK)
             for t in range(COL_TILE // MXU_W)]
    project = lambda r, t: _dot(h_ref[r:r + PROJ_ROW_CHUNK, :],
                                wb_ref[:, t * MXU_W:(t + 1) * MXU_W])
    is_q = pl.program_id(0) == TILE_Q
    gain = jnp.where(is_q, qg_ref[0] * (LOG2_E * HEAD_DIM ** -0.5), kg_ref[0])
    lane = lax.broadcasted_iota(jnp.int32, (1, LANES), 1)
    first_half = (lane % (HEAD_DIM // 2)) < (HEAD_DIM // 4)
    gsum = (lax.broadcasted_iota(jnp.int32, (MXU_W, MXU_W), 0) // HEAD_DIM
            == lax.broadcasted_iota(jnp.int32, (MXU_W, MXU_W), 1) // HEAD_DIM).astype(BF16)
    a_next = project(*items[0])
    for idx, (r, t) in enumerate(items):
        a = a_next
        if idx + 1 < len(items):
            a_next = project(*items[idx + 1])
        rows = slice(r, r + PROJ_ROW_CHUNK)
        cos = cos_ref[rows, :]
        sin = sin_ref[rows, :]
        ssq = _dot((a * a).astype(BF16), gsum)
        y = a * lax.rsqrt(ssq * (1.0 / HEAD_DIM) + EPS) * gain
        for u in range(MXU_W // LANES):
            yh = y[:, u * LANES:(u + 1) * LANES]
            partner = jnp.where(first_half,
                                pltpu.roll(yh, LANES - HEAD_DIM // 4, 1),
                                pltpu.roll(yh, HEAD_DIM // 4, 1))
            c0 = t * MXU_W + u * LANES
            o_ref[rows, c0:c0 + LANES] = (yh * cos + partner * sin).astype(BF16)


def _proj_call(kind, h, w_in, layer, w_tiles, extra=()):
    m, d = h.shape
    tm = 2 * PROJ_ROW_CHUNK if m % (2 * PROJ_ROW_CHUNK) == 0 else PROJ_ROW_CHUNK
    n_tiles = len(w_tiles)

    def w_tile(j):
        tile = w_tiles[0]
        for idx in range(1, n_tiles):
            tile = jnp.where(j == idx, w_tiles[idx], tile)
        return tile

    in_specs = [
        pl.BlockSpec((tm, d), lambda j, i: (i, 0)),
        pl.BlockSpec((1, d, COL_TILE), lambda j, i: (layer, 0, w_tile(j))),
    ]
    args = [h, w_in]
    if kind == "qk":
        qg, kg, cos_t, sin_t = extra
        seq_tiles = cos_t.shape[0] // tm
        in_specs += [
            pl.BlockSpec((1, 1, MXU_W), lambda j, i: (layer, 0, 0)),
            pl.BlockSpec((1, 1, MXU_W), lambda j, i: (layer, 0, 0)),
            pl.BlockSpec((tm, LANES), lambda j, i: (i % seq_tiles, 0)),
            pl.BlockSpec((tm, LANES), lambda j, i: (i % seq_tiles, 0)),
        ]
        args += [qg, kg, cos_t, sin_t]
    body = {"qk": _proj_qk_kernel, "plain": _proj_plain_kernel, "gate": _proj_gate_kernel}[kind]
    return pl.pallas_call(
        body,
        grid=(n_tiles, m // tm),
        in_specs=in_specs,
        out_specs=pl.BlockSpec((tm, COL_TILE), lambda j, i: (i, j)),
        out_shape=jax.ShapeDtypeStruct((m, n_tiles * COL_TILE), BF16),
        scratch_shapes=[pltpu.VMEM((d, COL_TILE), BF16)],
        compiler_params=_vmem(56 << 20),
        name="proj_" + kind,
    )(*args)


def _attn_kernel(lq_ref, lk_ref, sg_ref, q_ref, *refs, lam_init, pieces, key_block):
    n = len(pieces)
    k_refs, v_refs = refs[0:2 * n:2], refs[1:2 * n:2]
    o_ref, vt_ref, kmax_ref = refs[2 * n:]
    tq = q_ref.shape[0]
    lane = lax.broadcasted_iota(jnp.int32, (1, LANES), 1)

    @pl.when(pl.program_id(2) == 0)
    def _():
        group = lax.broadcasted_iota(jnp.int32, (LANES, LANES), 0) // HEAD_DIM
        gsum = (group == lane // HEAD_DIM).astype(BF16)
        kmax = None
        off = 0
        for t in range(n):
            rows = pieces[t]
            vt_ref[0:V_HEAD_DIM, off:off + rows] = v_refs[t][...].astype(F32).T.astype(BF16)
            kf = k_refs[t][...].astype(F32)
            norms = _dot((kf * kf).astype(BF16), gsum)
            piece_max = jnp.max(norms, axis=0, keepdims=True)
            kmax = piece_max if kmax is None else jnp.maximum(kmax, piece_max)
            off += rows
        vt_ref[V_HEAD_DIM:, :] = jnp.ones((ONES_ROWS, off), BF16)
        kmax_ref[...] = kmax

    e = jnp.exp(jnp.sum(lq_ref[0] * lk_ref[0], axis=-1, keepdims=True))
    lam = e[0:1, :] - e[1:2, :] + lam_init

    q = q_ref[...]
    zero = jnp.zeros_like(q)
    lo = lane < HEAD_DIM
    q_both = jnp.concatenate([jnp.where(lo, q, zero), jnp.where(lo, zero, q)], axis=0)

    qf = q.astype(F32)
    sel = (lax.broadcasted_iota(jnp.int32, (SUBLANES, LANES), 0) == lane // HEAD_DIM).astype(BF16)
    q_norms = _dot_t(sel, (qf * qf).astype(BF16))
    kmax = kmax_ref[...]
    bounds = jnp.concatenate(
        [jnp.sqrt(q_norms[c:c + 1, :] * kmax[:, c * HEAD_DIM:c * HEAD_DIM + 1]) * 1.01 + 1e-6
         for c in range(2)], axis=1)

    def key_blocks():
        off = 0
        for t in range(n):
            for s in range(0, pieces[t], key_block):
                size = min(key_block, pieces[t] - s)
                yield k_refs[t], s, off + s, size
            off += pieces[t]

    def accumulate(shifts):
        blocks = list(key_blocks())
        scores = lambda blk: _dot_t(blk[0][blk[1]:blk[1] + blk[3], :], q_both)
        acc = jnp.zeros((V_HEAD_DIM + ONES_ROWS, 2 * tq), F32)
        st_next = scores(blocks[0])
        for idx, (_, _, pos, size) in enumerate(blocks):
            st = st_next
            if idx + 1 < len(blocks):
                st_next = scores(blocks[idx + 1])
            acc = acc + _dot(vt_ref[:, pos:pos + size], jnp.exp2(st - shifts).astype(BF16))
        return acc

    def finish(acc):
        sums = acc[V_HEAD_DIM:V_HEAD_DIM + 1, :]
        ot = (acc[0:V_HEAD_DIM, 0:tq] * (1.0 / sums[:, 0:tq])
              - acc[0:V_HEAD_DIM, tq:] * (lam / sums[:, tq:]))
        yt = ot * lax.rsqrt(jnp.mean(ot * ot, axis=0, keepdims=True) + EPS)
        o_ref[...] = (yt.T * (sg_ref[0] * (1.0 - lam_init))).astype(BF16)

    acc = accumulate(bounds)
    finish(acc)

    @pl.when(jnp.logical_not(jnp.min(acc[V_HEAD_DIM:V_HEAD_DIM + 1, :]) >= 1e-30))
    def _():
        m = None
        for k_ref, s, _, size in key_blocks():
            bm = jnp.max(_dot_t(k_ref[s:s + size, :], q_both), axis=0, keepdims=True)
            m = bm if m is None else jnp.maximum(m, bm)
        finish(accumulate(m))


def _attn_call(lam_q, lam_k, subln_g, layer, q_arr, q_tile0, kv_list, *, n_batch, q_len,
               lam_init):
    tq = min(1024, q_len)
    q_tiles = q_len // tq
    in_specs = [
        pl.BlockSpec((1, 2, HEAD_DIM), lambda b, h, i: (layer, 0, 0)),
        pl.BlockSpec((1, 2, HEAD_DIM), lambda b, h, i: (layer, 0, 0)),
        pl.BlockSpec((1, 1, V_HEAD_DIM), lambda b, h, i: (layer, 0, 0)),
        pl.BlockSpec((tq, LANES), lambda b, h, i: (b * q_tiles + i, q_tile0 + h)),
    ]
    args = [lam_q, lam_k, subln_g.reshape(subln_g.shape[0], 1, V_HEAD_DIM), q_arr]
    for k_arr, k0, v_arr, v0, rows in kv_list:
        in_specs.append(pl.BlockSpec((rows, LANES), lambda b, h, i, k0=k0: (b, k0 + h)))
        in_specs.append(pl.BlockSpec((rows, LANES), lambda b, h, i, v0=v0: (b, v0 + h)))
        args += [k_arr, v_arr]
    pieces = tuple(kv[4] for kv in kv_list)
    return pl.pallas_call(
        functools.partial(_attn_kernel, lam_init=lam_init, pieces=pieces, key_block=512),
        grid=(n_batch, N_HEADS, q_tiles),
        in_specs=in_specs,
        out_specs=pl.BlockSpec((tq, LANES), lambda b, h, i: (b * q_tiles + i, h)),
        out_shape=jax.ShapeDtypeStruct((n_batch * q_len, N_HEADS * V_HEAD_DIM), BF16),
        scratch_shapes=[pltpu.VMEM((V_HEAD_DIM + ONES_ROWS, sum(pieces)), BF16),
                        pltpu.VMEM((1, LANES), F32)],
        compiler_params=_vmem(52 << 20),
        name="attn",
    )(*args)


def _fourier_dense_kernel(x_ref, cc_ref, sc_ref, pos_ref, o_ref, y_ref, *, n_pos):
    @pl.when(pl.program_id(1) == 0)
    def _():
        cc = cc_ref[...]
        sc = sc_ref[...]
        for g in range(FOURIER_GROUPS):
            xg = x_ref[:, g * GROUP_W:(g + 1) * GROUP_W]
            y_ref[0:n_pos, g * GROUP_W:(g + 1) * GROUP_W] = _dot(xg, cc).astype(BF16)
            y_ref[n_pos:2 * n_pos, g * GROUP_W:(g + 1) * GROUP_W] = _dot(xg, sc).astype(BF16)

    o_ref[...] = _dot(pos_ref[...], y_ref[...]).astype(BF16)


def _fourier_dense_call(p_arr, f_tile, chan_cos, chan_sin, pos_mat, *, n_batch, n_pos):
    tm = min(256, n_pos)
    row_tiles = n_pos // tm
    return pl.pallas_call(
        functools.partial(_fourier_dense_kernel, n_pos=n_pos),
        grid=(n_batch, row_tiles),
        in_specs=[
            pl.BlockSpec((n_pos, BRANCH_W), lambda b, i: (b, f_tile),
                         pipeline_mode=pl.Buffered(1)),
            pl.BlockSpec((GROUP_W, GROUP_W), lambda b, i: (0, 0)),
            pl.BlockSpec((GROUP_W, GROUP_W), lambda b, i: (0, 0)),
            pl.BlockSpec((tm, 2 * n_pos), lambda b, i: (i, 0)),
        ],
        out_specs=pl.BlockSpec((tm, BRANCH_W), lambda b, i: (b * row_tiles + i, 0)),
        out_shape=jax.ShapeDtypeStruct((n_batch * n_pos, BRANCH_W), BF16),
        scratch_shapes=[pltpu.VMEM((2 * n_pos, BRANCH_W), BF16)],
        compiler_params=_vmem(32 << 20),
        name="fourier_dense",
    )(p_arr, chan_cos, chan_sin, pos_mat)


def _fourier_latent_kernel(x_ref, wa_ref, wc_ref, twc_ref, tws_ref, cs_ref, o_ref,
                           xs_ref, ar_ref, ai_ref, zr_ref, zi_ref):
    tiles = GROUP_W // LANES
    half = FFT_R // 2
    lane_tile = lambda v, t: v[:, t * LANES:(t + 1) * LANES]
    pack = lambda v: pltpu.bitcast(v.astype(BF16), jnp.int32)

    def gather(ref, rows):
        words = jnp.concatenate([ref[t, rows, :] for t in range(tiles)], axis=1)
        return pltpu.bitcast(words, BF16)

    def scatter(ref, rows, v):
        words = pack(v)
        for t in range(tiles):
            ref[t, rows, :] = lane_tile(words, t)

    pitch_h = half + STRIDE_PAD
    pitch_r = FFT_R + STRIDE_PAD

    for a in range(FFT_R):
        words = pltpu.bitcast(x_ref[a * FFT_R:(a + 1) * FFT_R, :], jnp.int32)
        for t in range(tiles):
            xs_ref[t, a * pitch_h:a * pitch_h + half, :] = lane_tile(words, t)
    wa = wa_ref[...]
    wc = wc_ref[...]

    def stage1(beta, carry):
        xb = gather(xs_ref, pl.ds(beta, FFT_R, stride=pitch_h))
        a = _dot(wa, xb)
        tw_rows = pl.ds(pl.multiple_of(beta * 2 * FFT_R, 2 * FFT_R), 2 * FFT_R)
        tc = twc_ref[tw_rows, :]
        ts = tws_ref[tw_rows, :]
        re, im = a[0:2 * FFT_R], a[2 * FFT_R:]
        tc = jnp.concatenate([tc] * tiles, axis=1)
        ts = jnp.concatenate([ts] * tiles, axis=1)
        rows = pl.ds(beta * pitch_r, FFT_R)
        scatter(ar_ref, rows, re * tc + im * ts)
        scatter(ai_ref, rows, im * tc - re * ts)
        return carry

    lax.fori_loop(0, half, stage1, 0, unroll=4)

    def stage2(gamma, carry):
        parts = []
        for p in range(2):
            rows = pl.ds(2 * gamma + p, half, stride=pitch_r)
            parts += [gather(ar_ref, rows), gather(ai_ref, rows)]
        z = _dot(wc, jnp.concatenate(parts, axis=0))
        rows = pl.ds(gamma, FFT_R, stride=pitch_h)
        scatter(zr_ref, rows, z[0:2 * FFT_R])
        scatter(zi_ref, rows, z[2 * FFT_R:])
        return carry

    lax.fori_loop(0, half, stage2, 0, unroll=4)

    cs = cs_ref[...]
    d_per_block = 8
    for d0 in range(0, FFT_R, d_per_block):
        z = jnp.concatenate(
            [jnp.concatenate([gather(ref, slice(d * pitch_h, d * pitch_h + half))
                              for ref in (zr_ref, zi_ref)], axis=1)
             for d in range(d0, d0 + d_per_block)], axis=0)
        o_ref[d0 * FFT_R:(d0 + d_per_block) * FFT_R, :] = _dot(z, cs).astype(BF16)


def _fourier_latent_call(plain, wa, wc, tw_cos, tw_sin, chan_cs, *, n_batch):
    n_pos = FFT_R * FFT_R
    tiles = GROUP_W // LANES
    const = lambda shape: pl.BlockSpec(shape, lambda n, q: (0,) * len(shape))
    half = FFT_R // 2
    by_half = pltpu.VMEM((tiles, FFT_R * (half + STRIDE_PAD), LANES), jnp.int32)
    by_r = pltpu.VMEM((tiles, half * (FFT_R + STRIDE_PAD), LANES), jnp.int32)
    return pl.pallas_call(
        _fourier_latent_kernel,
        grid=(n_batch, FOURIER_GROUPS),
        in_specs=[
            pl.BlockSpec((n_pos, GROUP_W), lambda n, q: (n, PLAIN_F * FOURIER_GROUPS + q)),
            const((4 * FFT_R, 2 * FFT_R)),
            const((4 * FFT_R, 4 * FFT_R)),
            const((n_pos, LANES)),
            const((n_pos, LANES)),
            const((2 * GROUP_W, GROUP_W)),
        ],
        out_specs=pl.BlockSpec((n_pos, GROUP_W), lambda n, q: (n, q)),
        out_shape=jax.ShapeDtypeStruct((n_batch * n_pos, BRANCH_W), BF16),
        scratch_shapes=[by_half, by_r, by_r, by_half, by_half],
        compiler_params=_vmem(48 << 20),
        name="fourier_latent",
    )(plain, wa, wc, tw_cos, tw_sin, chan_cs)


def _merge_kernel(x_ref, mod_ref, attn_ref, conv_ref, before_ref, after_ref, four_ref, gate_ref,
                  wconv_ref, wbr_ref, wo_ref, ln2_ref, o_ref, h2_ref, *, seq_len):
    tm, d = x_ref.shape
    i = pl.program_id(0)
    pos0 = (i * tm) % seq_len
    cc_cols = slice(PLAIN_CC * BRANCH_W, (PLAIN_CC + 1) * BRANCH_W)
    cx_cols = slice(PLAIN_CX * BRANCH_W, (PLAIN_CX + 1) * BRANCH_W)
    cb_cols = slice(PLAIN_CB * BRANCH_W, (PLAIN_CB + 1) * BRANCH_W)

    u = conv_ref[:, cc_cols].astype(F32) * conv_ref[:, cx_cols].astype(F32)
    last = slice(SUBLANES - 1, SUBLANES)
    u_before = before_ref[last, cc_cols].astype(F32) * before_ref[last, cx_cols].astype(F32)
    u_after = after_ref[0:1, cc_cols].astype(F32) * after_ref[0:1, cx_cols].astype(F32)
    u_before = jnp.where(pos0 == 0, 0.0, u_before)
    u_after = jnp.where(pos0 + tm == seq_len, 0.0, u_after)
    row = lax.broadcasted_iota(jnp.int32, (tm, 1), 0)
    u_prev = jnp.where(row == 0, u_before, pltpu.roll(u, 1, 0))
    u_next = jnp.where(row == tm - 1, u_after, pltpu.roll(u, tm - 1, 0))
    w = wconv_ref[0]
    conv = (conv_ref[:, cb_cols].astype(F32)
            * (u_prev * w[0:1, :] + u * w[1:2, :] + u_next * w[2:3, :]))

    gate = lambda branch: gate_ref[:, branch * d:(branch + 1) * d].astype(F32)
    mix = (gate(0) * _dot(attn_ref[...], wbr_ref[0, 0])
           + gate(1) * _dot(conv.astype(BF16), wbr_ref[0, 1])
           + gate(2) * _dot(four_ref[...], wbr_ref[0, 2]))
    y = _dot(mix.astype(BF16), wo_ref[0])
    x_new = x_ref[...] + mod_ref[0, 0, 2:3, :] * y
    o_ref[...] = x_new
    h2 = _rms_modulate(x_new, ln2_ref[0], mod_ref[0, 0, 3:4, :], mod_ref[0, 0, 4:5, :])
    h2_ref[...] = h2.astype(BF16)


def _merge_call(x2d, mod, layer, mod_row, attn_o, plain, gates, four_o, w_conv, w_br, w_o,
                ln2_g, *, seq_len):
    m, d = x2d.shape
    tm = 256
    halo_per_tile = tm // SUBLANES
    last_halo = m // SUBLANES - 1
    const = dict(pipeline_mode=pl.Buffered(1))
    assert (PLAIN_CC, PLAIN_CX, PLAIN_CB) == (0, 1, 2)

    return pl.pallas_call(
        functools.partial(_merge_kernel, seq_len=seq_len),
        grid=(m // tm,),
        in_specs=[
            pl.BlockSpec((tm, d), lambda i: (i, 0)),
            pl.BlockSpec((1, 1, 6, d), lambda i: (layer, mod_row(i * tm), 0, 0)),
            pl.BlockSpec((tm, BRANCH_W), lambda i: (i, 0)),
            pl.BlockSpec((tm, 3 * BRANCH_W), lambda i: (i, 0)),
            pl.BlockSpec((SUBLANES, 2 * BRANCH_W),
                         lambda i: (jnp.maximum(i * halo_per_tile - 1, 0), 0)),
            pl.BlockSpec((SUBLANES, 2 * BRANCH_W),
                         lambda i: (jnp.minimum((i + 1) * halo_per_tile, last_halo), 0)),
            pl.BlockSpec((tm, BRANCH_W), lambda i: (i, 0)),
            pl.BlockSpec((tm, N_BRANCH * d), lambda i: (i, 0)),
            pl.BlockSpec((1, CONV_K, BRANCH_W), lambda i: (layer, 0, 0)),
            pl.BlockSpec((1, N_BRANCH, BRANCH_W, d), lambda i: (layer, 0, 0, 0), **const),
            pl.BlockSpec((1, d, d), lambda i: (layer, 0, 0), **const),
            pl.BlockSpec((1, 1, d), lambda i: (layer, 0, 0)),
        ],
        out_specs=[pl.BlockSpec((tm, d), lambda i: (i, 0))] * 2,
        out_shape=[jax.ShapeDtypeStruct((m, d), F32), jax.ShapeDtypeStruct((m, d), BF16)],
        compiler_params=_vmem(56 << 20),
        name="merge",
    )(x2d, mod, attn_o, plain, plain, plain, four_o, gates, w_conv, w_br, w_o,
      ln2_g.reshape(ln2_g.shape[0], 1, d))


def _mlp_kernel(x_ref, h_ref, mod_ref, w1_ref, w2_ref, o_ref, *, out_chunk, x_parts):
    j = pl.program_id(1)

    @pl.when(j == 0)
    def _():
        o_ref[...] = jnp.zeros_like(o_ref)

    h = h_ref[...]
    half = w1_ref.shape[2] // 2
    up = lambda s: jnp.maximum(_dot(h, w1_ref[0, :, s * half:(s + 1) * half]), 0.0)
    hid = [up(0), up(1)]
    hid2 = [(v * v).astype(BF16) for v in hid]
    for n0 in range(0, o_ref.shape[1], out_chunk):
        cols = slice(n0, n0 + out_chunk)
        down = (_dot(hid2[0], w2_ref[0, 0:half, cols])
                + _dot(hid2[1], w2_ref[0, half:, cols]))
        o_ref[:, cols] += mod_ref[0, 0, 5:6, cols] * down

    part = j - (pl.num_programs(1) - x_parts)

    @pl.when(part >= 0)
    def _():
        part_rows = x_ref.shape[0]
        rows = pl.ds(pl.multiple_of(part * part_rows, part_rows), part_rows)
        o_ref[rows, :] += x_ref[...]


def _mlp_call(x2d, h2, mod, layer, mod_row, w1, w2):
    m, d = x2d.shape
    d_ff = w1.shape[2]
    tm, tf = 1024, 1024
    steps = d_ff // tf
    x_parts = 4

    def x_rows(i, j):
        part = j - (steps - x_parts)
        return jnp.where(part >= 0, i * x_parts + part, jnp.maximum(i * x_parts - 1, 0)), 0

    return pl.pallas_call(
        functools.partial(_mlp_kernel, out_chunk=512, x_parts=x_parts),
        grid=(m // tm, steps),
        in_specs=[
            pl.BlockSpec((tm // x_parts, d), x_rows),
            pl.BlockSpec((tm, d), lambda i, j: (i, 0)),
            pl.BlockSpec((1, 1, 6, d), lambda i, j: (layer, mod_row(i * tm), 0, 0)),
            pl.BlockSpec((1, d, tf), lambda i, j: (layer, 0, j)),
            pl.BlockSpec((1, tf, d), lambda i, j: (layer, j, 0)),
        ],
        out_specs=pl.BlockSpec((tm, d), lambda i, j: (i, 0)),
        out_shape=jax.ShapeDtypeStruct((m, d), F32),
        compiler_params=_vmem(56 << 20),
        name="mlp",
    )(x2d, h2, mod, w1, w2)


def _rope_tables(seq_len):
    rows = seq_len // GRID_W
    t_row = jnp.repeat(jnp.arange(rows), GRID_W).astype(F32)
    t_col = jnp.tile(jnp.arange(GRID_W), rows).astype(F32)
    axis_dim = HEAD_DIM // 2
    inv_freq = 1.0 / (ROPE_THETA ** (jnp.arange(0, axis_dim, 2, dtype=F32) / axis_dim))
    ang_r = t_row[:, None] * inv_freq[None, :]
    ang_c = t_col[:, None] * inv_freq[None, :]
    cos = jnp.concatenate([jnp.cos(ang_r)] * 2 + [jnp.cos(ang_c)] * 2, axis=-1)
    sin = jnp.concatenate([-jnp.sin(ang_r), jnp.sin(ang_r), -jnp.sin(ang_c), jnp.sin(ang_c)],
                          axis=-1)
    return jnp.tile(cos, (1, LANES // HEAD_DIM)), jnp.tile(sin, (1, LANES // HEAD_DIM))


def _dft_cos_sin(rows, cols, period, scale):
    r = jnp.arange(rows, dtype=jnp.int32)
    c = jnp.arange(cols, dtype=jnp.int32)
    ang = ((r[:, None] * c[None, :]) % period).astype(F32) * (2.0 * math.pi / period)
    return jnp.cos(ang) * scale, jnp.sin(ang) * scale


def kernel(x, c, ctx, c_ctx, ln1_g, ln2_g, w_ada, b_ada, w_in, qn_g, kn_g, lam_q, lam_k,
           subln_g, w_conv, w_br, w_o, w1, w2):
    n_batch, seq, d = x.shape
    ctx_len = ctx.shape[1]
    depth = w_ada.shape[0]
    assert seq == FFT_R * FFT_R and ctx_len == 256 and n_batch <= CTX_MOD_ROW and d == D_MODEL

    c_rows = jnp.zeros((MOD_ROWS, d), F32).at[:n_batch].set(c).at[CTX_MOD_ROW].set(c_ctx)
    mod = _mod_call(c_rows, w_ada, b_ada).reshape(depth, MOD_ROWS, 6, d)

    cos_l, sin_l = _rope_tables(seq)
    cos_c = jnp.ones((n_batch * ctx_len, LANES), F32)
    sin_c = jnp.zeros((n_batch * ctx_len, LANES), F32)
    qg = jnp.tile(qn_g, (1, MXU_W // HEAD_DIM)).reshape(depth, 1, MXU_W)
    kg = jnp.tile(kn_g, (1, MXU_W // HEAD_DIM)).reshape(depth, 1, MXU_W)

    ch_c, ch_s = _dft_cos_sin(GROUP_W, GROUP_W, GROUP_W, GROUP_W ** -0.5)
    chan_cos, chan_sin = ch_c.astype(BF16), ch_s.astype(BF16)
    chan_cs = jnp.concatenate([chan_cos, chan_sin], axis=0)
    pc_c, pc_s = _dft_cos_sin(ctx_len, ctx_len, ctx_len, ctx_len ** -0.5)
    pos_c = jnp.concatenate([pc_c, -pc_s], axis=1).astype(BF16)
    r_c, r_s = _dft_cos_sin(FFT_R, FFT_R, FFT_R, FFT_R ** -0.5)
    eye2 = jnp.eye(2, dtype=F32)
    wa = jnp.kron(jnp.concatenate([r_c, -r_s], axis=0), eye2).astype(BF16)
    wc = jnp.concatenate([jnp.concatenate([r_c, r_s], axis=1),
                          jnp.concatenate([-r_s, r_c], axis=1)], axis=0)
    wc = jnp.einsum("odib,pq->odpqib", wc.reshape(2, FFT_R, 2, FFT_R), eye2)
    wc = wc.reshape(4 * FFT_R, 4 * FFT_R).astype(BF16)
    tw_c, tw_s = _dft_cos_sin(FFT_R, FFT_R, seq, 1.0)
    pair_order = lambda t: jnp.broadcast_to(
        t.reshape(FFT_R // 2, 2, FFT_R).transpose(0, 2, 1).reshape(seq, 1), (seq, LANES))
    tw_cos, tw_sin = pair_order(tw_c), pair_order(tw_s)

    w_br_b = w_br.astype(BF16)
    w_o_b = w_o.astype(BF16)
    w1_b = w1.astype(BF16)
    w2_b = w2.astype(BF16)

    lat_row = lambda row: row // seq
    ctx_row = lambda row: CTX_MOD_ROW
    heads = COL_TILE // LANES

    xl = x.reshape(n_batch * seq, d)
    xc = ctx.reshape(n_batch * ctx_len, d)

    for l in range(depth):
        last = l == depth - 1
        lam_init = 0.8 - 0.6 * math.exp(-0.3 * l)

        hc = _normmod_call(xc, mod, l, ctx_row, ln1_g)
        hl = _normmod_call(xl, mod, l, lat_row, ln1_g)
        qk_c = _proj_call("qk", hc, w_in, l, QK_TILES, (qg, kg, cos_c, sin_c))
        qk_l = _proj_call("qk", hl, w_in, l, QK_TILES, (qg, kg, cos_l, sin_l))
        plain_c = _proj_call("plain", hc, w_in, l, (TILE_V,) if last else PLAIN_TILES)
        plain_l = _proj_call("plain", hl, w_in, l, PLAIN_TILES)
        gates_l = _proj_call("gate", hl, w_in, l, GATE_TILES)

        kv_ctx = (qk_c, heads, plain_c, (0 if last else PLAIN_V) * heads, ctx_len)
        kv_lat = (qk_l, heads, plain_l, PLAIN_V * heads, seq)
        attn_l = _attn_call(lam_q, lam_k, subln_g, l, qk_l, 0, [kv_ctx, kv_lat],
                            n_batch=n_batch, q_len=seq, lam_init=lam_init)
        four_l = _fourier_latent_call(plain_l, wa, wc, tw_cos, tw_sin, chan_cs, n_batch=n_batch)
        xl, h2l = _merge_call(xl, mod, l, lat_row, attn_l, plain_l, gates_l, four_l, w_conv,
                              w_br_b, w_o_b, ln2_g, seq_len=seq)
        xl = _mlp_call(xl, h2l, mod, l, lat_row, w1_b, w2_b)

        if not last:
            gates_c = _proj_call("gate", hc, w_in, l, GATE_TILES)
            attn_c = _attn_call(lam_q, lam_k, subln_g, l, qk_c, 0, [kv_ctx],
                                n_batch=n_batch, q_len=ctx_len, lam_init=lam_init)
            four_c = _fourier_dense_call(plain_c, PLAIN_F, chan_cos, chan_sin, pos_c,
                                         n_batch=n_batch, n_pos=ctx_len)
            xc, h2c = _merge_call(xc, mod, l, ctx_row, attn_c, plain_c, gates_c, four_c, w_conv,
                                  w_br_b, w_o_b, ln2_g, seq_len=ctx_len)
            xc = _mlp_call(xc, h2c, mod, l, ctx_row, w1_b, w2_b)

    return xl.reshape(n_batch, seq, d)
```

```python
import functools
import math

import jax
import jax.numpy as jnp
from jax import lax
from jax.experimental import pallas as pl
from jax.experimental.pallas import tpu as pltpu

D_MODEL = 2048
GRID_W = 64
N_HEADS = 8
HEAD_DIM = 64
V_HEAD_DIM = 128
BRANCH_W = 1024
FOURIER_GROUPS = 4
GROUP_W = BRANCH_W // FOURIER_GROUPS
N_BRANCH = 3
CONV_K = 3
EPS = 1e-6
ROPE_THETA = 10000.0

COL_TILE = 1024
PROJ_ROW_CHUNK = 1024
TILE_Q, TILE_K, TILE_V, TILE_CB, TILE_CC, TILE_CX, TILE_F, TILE_G = range(8)
QK_TILES = (TILE_Q, TILE_K)
PLAIN_TILES = (TILE_CC, TILE_CX, TILE_CB, TILE_V, TILE_F)
PLAIN_CC, PLAIN_CX, PLAIN_CB, PLAIN_V, PLAIN_F = range(5)
GATE_TILES = tuple(range(TILE_G, TILE_G + N_BRANCH * D_MODEL // COL_TILE))

LANES = 128
SUBLANES = 8
MXU_W = 256
ONES_ROWS = 16
LOG2_E = math.log2(math.e)
MOD_ROWS = 8
CTX_MOD_ROW = 4
FFT_R = 64
STRIDE_PAD = 4

F32 = jnp.float32
BF16 = jnp.bfloat16


def _vmem(nbytes):
    return pltpu.CompilerParams(vmem_limit_bytes=nbytes)


def _sigmoid(x):
    return 1.0 / (1.0 + jnp.exp(-x))


def _rms_modulate(x, g, shift, scale):
    y = x * lax.rsqrt(jnp.mean(x * x, axis=-1, keepdims=True) + EPS) * g
    return y * (1.0 + scale) + shift


def _dot(a, b):
    return jnp.dot(a, b, preferred_element_type=F32)


def _dot_t(a, b):
    return lax.dot_general(a, b, (((1,), (1,)), ((), ())), preferred_element_type=F32)


def _mod_kernel(c_ref, w_ref, b_ref, o_ref):
    c = c_ref[...]
    cs = c * _sigmoid(c)
    split = lambda v: (v.astype(BF16), (v - v.astype(BF16).astype(F32)).astype(BF16))
    cs_hi, cs_lo = split(cs)
    w_hi, w_lo = split(w_ref[0])
    o_ref[0] = (_dot(cs_hi, w_hi) + (_dot(cs_lo, w_hi) + _dot(cs_hi, w_lo))) + b_ref[0]


def _mod_call(c_rows, w_ada, b_ada):
    depth, d, n = w_ada.shape
    tn = 1536
    return pl.pallas_call(
        _mod_kernel,
        grid=(depth, n // tn),
        in_specs=[
            pl.BlockSpec((MOD_ROWS, d), lambda l, j: (0, 0)),
            pl.BlockSpec((1, d, tn), lambda l, j: (l, 0, j)),
            pl.BlockSpec((1, 1, tn), lambda l, j: (l, 0, j)),
        ],
        out_specs=pl.BlockSpec((1, MOD_ROWS, tn), lambda l, j: (l, 0, j)),
        out_shape=jax.ShapeDtypeStruct((depth, MOD_ROWS, n), F32),
        compiler_params=_vmem(40 << 20),
        name="mod",
    )(c_rows, w_ada, b_ada.reshape(depth, 1, n))


def _normmod_kernel(x_ref, mod_ref, g_ref, o_ref):
    h = _rms_modulate(x_ref[...], g_ref[0], mod_ref[0, 0, 0:1, :], mod_ref[0, 0, 1:2, :])
    o_ref[...] = h.astype(BF16)


def _normmod_call(x2d, mod, layer, mod_row, ln_g):
    m, d = x2d.shape
    tm = 1024
    return pl.pallas_call(
        _normmod_kernel,
        grid=(m // tm,),
        in_specs=[
            pl.BlockSpec((tm, d), lambda i: (i, 0)),
            pl.BlockSpec((1, 1, 6, d), lambda i: (layer, mod_row(i * tm), 0, 0)),
            pl.BlockSpec((1, 1, d), lambda i: (layer, 0, 0)),
        ],
        out_specs=pl.BlockSpec((tm, d), lambda i: (i, 0)),
        out_shape=jax.ShapeDtypeStruct((m, d), BF16),
        compiler_params=_vmem(32 << 20),
        name="normmod",
    )(x2d, mod, ln_g.reshape(ln_g.shape[0], 1, d))


def _cast_weights(w_ref, wb_ref):
    @pl.when(pl.program_id(1) == 0)
    def _():
        wb_ref[...] = w_ref[0].astype(BF16)


def _project_rows(h_ref, wb_ref, o_ref, epilogue):
    for r in range(0, h_ref.shape[0], PROJ_ROW_CHUNK):
        rows = slice(r, r + PROJ_ROW_CHUNK)
        o_ref[rows, :] = epilogue(_dot(h_ref[rows, :], wb_ref[...])).astype(BF16)


def _proj_plain_kernel(h_ref, w_ref, o_ref, wb_ref):
    _cast_weights(w_ref, wb_ref)
    _project_rows(h_ref, wb_ref, o_ref, lambda acc: acc)


def _proj_gate_kernel(h_ref, w_ref, o_ref, wb_ref):
    _cast_weights(w_ref, wb_ref)
    _project_rows(h_ref, wb_ref, o_ref, lambda acc: 0.5 * jnp.tanh(0.5 * acc) + 0.5)


def _proj_qk_kernel(h_ref, w_ref, qg_ref, kg_ref, cos_ref, sin_ref, o_ref, wb_ref):
    _cast_weights(w_ref, wb_ref)
    items = [(r, t) for r in range(0, h_ref.shape[0], PROJ_ROW_CHUNK)
             for t in range(COL_TILE // MXU_W)]
    project = lambda r, t: _dot(h_ref[r:r + PROJ_ROW_CHUNK, :],
                                wb_ref[:, t * MXU_W:(t + 1) * MXU_W])
    is_q = pl.program_id(0) == TILE_Q
    gain = jnp.where(is_q, qg_ref[0] * (LOG2_E * HEAD_DIM ** -0.5), kg_ref[0])
    lane = lax.broadcasted_iota(jnp.int32, (1, LANES), 1)
    first_half = (lane % (HEAD_DIM // 2)) < (HEAD_DIM // 4)
    gsum = (lax.broadcasted_iota(jnp.int32, (MXU_W, MXU_W), 0) // HEAD_DIM
            == lax.broadcasted_iota(jnp.int32, (MXU_W, MXU_W), 1) // HEAD_DIM).astype(BF16)
    a_next = project(*items[0])
    for idx, (r, t) in enumerate(items):
        a = a_next
        if idx + 1 < len(items):
            a_next = project(*items[idx + 1])
        rows = slice(r, r + PROJ_ROW_CHUNK)
        cos = cos_ref[rows, :]
        sin = sin_ref[rows, :]
        ssq = _dot((a * a).astype(BF16), gsum)
        y = a * lax.rsqrt(ssq * (1.0 / HEAD_DIM) + EPS) * gain
        for u in range(MXU_W // LANES):
            yh = y[:, u * LANES:(u + 1) * LANES]
            partner = jnp.where(first_half,
                                pltpu.roll(yh, LANES - HEAD_DIM // 4, 1),
                                pltpu.roll(yh, HEAD_DIM // 4, 1))
            c0 = t * MXU_W + u * LANES
            o_ref[rows, c0:c0 + LANES] = (yh * cos + partner * sin).astype(BF16)


def _proj_call(kind, h, w_in, layer, w_tiles, extra=()):
    m, d = h.shape
    tm = 2 * PROJ_ROW_CHUNK if m % (2 * PROJ_ROW_CHUNK) == 0 else PROJ_ROW_CHUNK
    n_tiles = len(w_tiles)

    def w_tile(j):
        tile = w_tiles[0]
        for idx in range(1, n_tiles):
            tile = jnp.where(j == idx, w_tiles[idx], tile)
        return tile

    in_specs = [
        pl.BlockSpec((tm, d), lambda j, i: (i, 0)),
        pl.BlockSpec((1, d, COL_TILE), lambda j, i: (layer, 0, w_tile(j))),
    ]
    args = [h, w_in]
    if kind == "qk":
        qg, kg, cos_t, sin_t = extra
        seq_tiles = cos_t.shape[0] // tm
        in_specs += [
            pl.BlockSpec((1, 1, MXU_W), lambda j, i: (layer, 0, 0)),
            pl.BlockSpec((1, 1, MXU_W), lambda j, i: (layer, 0, 0)),
            pl.BlockSpec((tm, LANES), lambda j, i: (i % seq_tiles, 0)),
            pl.BlockSpec((tm, LANES), lambda j, i: (i % seq_tiles, 0)),
        ]
        args += [qg, kg, cos_t, sin_t]
    body = {"qk": _proj_qk_kernel, "plain": _proj_plain_kernel, "gate": _proj_gate_kernel}[kind]
    return pl.pallas_call(
        body,
        grid=(n_tiles, m // tm),
        in_specs=in_specs,
        out_specs=pl.BlockSpec((tm, COL_TILE), lambda j, i: (i, j)),
        out_shape=jax.ShapeDtypeStruct((m, n_tiles * COL_TILE), BF16),
        scratch_shapes=[pltpu.VMEM((d, COL_TILE), BF16)],
        compiler_params=_vmem(56 << 20),
        name="proj_" + kind,
    )(*args)


def _attn_kernel(lq_ref, lk_ref, sg_ref, q_ref, *refs, lam_init, pieces, key_block):
    n = len(pieces)
    k_refs, v_refs = refs[0:2 * n:2], refs[1:2 * n:2]
    o_ref, vt_ref, kmax_ref = refs[2 * n:]
    tq = q_ref.shape[0]
    lane = lax.broadcasted_iota(jnp.int32, (1, LANES), 1)

    @pl.when(pl.program_id(2) == 0)
    def _():
        group = lax.broadcasted_iota(jnp.int32, (LANES, LANES), 0) // HEAD_DIM
        gsum = (group == lane // HEAD_DIM).astype(BF16)
        kmax = None
        off = 0
        for t in range(n):
            rows = pieces[t]
            vt_ref[0:V_HEAD_DIM, off:off + rows] = v_refs[t][...].astype(F32).T.astype(BF16)
            kf = k_refs[t][...].astype(F32)
            norms = _dot((kf * kf).astype(BF16), gsum)
            piece_max = jnp.max(norms, axis=0, keepdims=True)
            kmax = piece_max if kmax is None else jnp.maximum(kmax, piece_max)
            off += rows
        vt_ref[V_HEAD_DIM:, :] = jnp.ones((ONES_ROWS, off), BF16)
        kmax_ref[...] = kmax

    e = jnp.exp(jnp.sum(lq_ref[0] * lk_ref[0], axis=-1, keepdims=True))
    lam = e[0:1, :] - e[1:2, :] + lam_init

    q = q_ref[...]
    zero = jnp.zeros_like(q)
    lo = lane < HEAD_DIM
    q_both = jnp.concatenate([jnp.where(lo, q, zero), jnp.where(lo, zero, q)], axis=0)

    qf = q.astype(F32)
    sel = (lax.broadcasted_iota(jnp.int32, (SUBLANES, LANES), 0) == lane // HEAD_DIM).astype(BF16)
    q_norms = _dot_t(sel, (qf * qf).astype(BF16))
    kmax = kmax_ref[...]
    bounds = jnp.concatenate(
        [jnp.sqrt(q_norms[c:c + 1, :] * kmax[:, c * HEAD_DIM:c * HEAD_DIM + 1]) * 1.01 + 1e-6
         for c in range(2)], axis=1)

    def key_blocks():
        off = 0
        for t in range(n):
            for s in range(0, pieces[t], key_block):
                size = min(key_block, pieces[t] - s)
                yield k_refs[t], s, off + s, size
            off += pieces[t]

    def accumulate(shifts):
        blocks = list(key_blocks())
        scores = lambda blk: _dot_t(blk[0][blk[1]:blk[1] + blk[3], :], q_both)
        acc = jnp.zeros((V_HEAD_DIM + ONES_ROWS, 2 * tq), F32)
        st_next = scores(blocks[0])
        for idx, (_, _, pos, size) in enumerate(blocks):
            st = st_next
            if idx + 1 < len(blocks):
                st_next = scores(blocks[idx + 1])
            acc = acc + _dot(vt_ref[:, pos:pos + size], jnp.exp2(st - shifts).astype(BF16))
        return acc

    def finish(acc):
        sums = acc[V_HEAD_DIM:V_HEAD_DIM + 1, :]
        ot = (acc[0:V_HEAD_DIM, 0:tq] * (1.0 / sums[:, 0:tq])
              - acc[0:V_HEAD_DIM, tq:] * (lam / sums[:, tq:]))
        yt = ot * lax.rsqrt(jnp.mean(ot * ot, axis=0, keepdims=True) + EPS)
        o_ref[...] = (yt.T * (sg_ref[0] * (1.0 - lam_init))).astype(BF16)

    acc = accumulate(bounds)
    finish(acc)

    @pl.when(jnp.logical_not(jnp.min(acc[V_HEAD_DIM:V_HEAD_DIM + 1, :]) >= 1e-30))
    def _():
        m = None
        for k_ref, s, _, size in key_blocks():
            bm = jnp.max(_dot_t(k_ref[s:s + size, :], q_both), axis=0, keepdims=True)
            m = bm if m is None else jnp.maximum(m, bm)
        finish(accumulate(m))


def _attn_call(lam_q, lam_k, subln_g, layer, q_arr, q_tile0, kv_list, *, n_batch, q_len,
               lam_init):
    tq = min(1024, q_len)
    q_tiles = q_len // tq
    in_specs = [
        pl.BlockSpec((1, 2, HEAD_DIM), lambda b, h, i: (layer, 0, 0)),
        pl.BlockSpec((1, 2, HEAD_DIM), lambda b, h, i: (layer, 0, 0)),
        pl.BlockSpec((1, 1, V_HEAD_DIM), lambda b, h, i: (layer, 0, 0)),
        pl.BlockSpec((tq, LANES), lambda b, h, i: (b * q_tiles + i, q_tile0 + h)),
    ]
    args = [lam_q, lam_k, subln_g.reshape(subln_g.shape[0], 1, V_HEAD_DIM), q_arr]
    for k_arr, k0, v_arr, v0, rows in kv_list:
        in_specs.append(pl.BlockSpec((rows, LANES), lambda b, h, i, k0=k0: (b, k0 + h)))
        in_specs.append(pl.BlockSpec((rows, LANES), lambda b, h, i, v0=v0: (b, v0 + h)))
        args += [k_arr, v_arr]
    pieces = tuple(kv[4] for kv in kv_list)
    return pl.pallas_call(
        functools.partial(_attn_kernel, lam_init=lam_init, pieces=pieces, key_block=512),
        grid=(n_batch, N_HEADS, q_tiles),
        in_specs=in_specs,
        out_specs=pl.BlockSpec((tq, LANES), lambda b, h, i: (b * q_tiles + i, h)),
        out_shape=jax.ShapeDtypeStruct((n_batch * q_len, N_HEADS * V_HEAD_DIM), BF16),
        scratch_shapes=[pltpu.VMEM((V_HEAD_DIM + ONES_ROWS, sum(pieces)), BF16),
                        pltpu.VMEM((1, LANES), F32)],
        compiler_params=_vmem(52 << 20),
        name="attn",
    )(*args)


def _fourier_dense_kernel(x_ref, cc_ref, sc_ref, pos_ref, o_ref, y_ref, *, n_pos):
    @pl.when(pl.program_id(1) == 0)
    def _():
        cc = cc_ref[...]
        sc = sc_ref[...]
        for g in range(FOURIER_GROUPS):
            xg = x_ref[:, g * GROUP_W:(g + 1) * GROUP_W]
            y_ref[0:n_pos, g * GROUP_W:(g + 1) * GROUP_W] = _dot(xg, cc).astype(BF16)
            y_ref[n_pos:2 * n_pos, g * GROUP_W:(g + 1) * GROUP_W] = _dot(xg, sc).astype(BF16)

    o_ref[...] = _dot(pos_ref[...], y_ref[...]).astype(BF16)


def _fourier_dense_call(p_arr, f_tile, chan_cos, chan_sin, pos_mat, *, n_batch, n_pos):
    tm = min(256, n_pos)
    row_tiles = n_pos // tm
    return pl.pallas_call(
        functools.partial(_fourier_dense_kernel, n_pos=n_pos),
        grid=(n_batch, row_tiles),
        in_specs=[
            pl.BlockSpec((n_pos, BRANCH_W), lambda b, i: (b, f_tile),
                         pipeline_mode=pl.Buffered(1)),
            pl.BlockSpec((GROUP_W, GROUP_W), lambda b, i: (0, 0)),
            pl.BlockSpec((GROUP_W, GROUP_W), lambda b, i: (0, 0)),
            pl.BlockSpec((tm, 2 * n_pos), lambda b, i: (i, 0)),
        ],
        out_specs=pl.BlockSpec((tm, BRANCH_W), lambda b, i: (b * row_tiles + i, 0)),
        out_shape=jax.ShapeDtypeStruct((n_batch * n_pos, BRANCH_W), BF16),
        scratch_shapes=[pltpu.VMEM((2 * n_pos, BRANCH_W), BF16)],
        compiler_params=_vmem(32 << 20),
        name="fourier_dense",
    )(p_arr, chan_cos, chan_sin, pos_mat)


def _fourier_latent_kernel(x_ref, wa_ref, wc_ref, twc_ref, tws_ref, cs_ref, o_ref,
                           xs_ref, ar_ref, ai_ref, zr_ref, zi_ref):
    tiles = GROUP_W // LANES
    half = FFT_R // 2
    lane_tile = lambda v, t: v[:, t * LANES:(t + 1) * LANES]
    pack = lambda v: pltpu.bitcast(v.astype(BF16), jnp.int32)

    def gather(ref, rows):
        words = jnp.concatenate([ref[t, rows, :] for t in range(tiles)], axis=1)
        return pltpu.bitcast(words, BF16)

    def scatter(ref, rows, v):
        words = pack(v)
        for t in range(tiles):
            ref[t, rows, :] = lane_tile(words, t)

    pitch_h = half + STRIDE_PAD
    pitch_r = FFT_R + STRIDE_PAD

    for a in range(FFT_R):
        words = pltpu.bitcast(x_ref[a * FFT_R:(a + 1) * FFT_R, :], jnp.int32)
        for t in range(tiles):
            xs_ref[t, a * pitch_h:a * pitch_h + half, :] = lane_tile(words, t)
    wa = wa_ref[...]
    wc = wc_ref[...]

    def stage1(beta, carry):
        xb = gather(xs_ref, pl.ds(beta, FFT_R, stride=pitch_h))
        a = _dot(wa, xb)
        tw_rows = pl.ds(pl.multiple_of(beta * 2 * FFT_R, 2 * FFT_R), 2 * FFT_R)
        tc = twc_ref[tw_rows, :]
        ts = tws_ref[tw_rows, :]
        re, im = a[0:2 * FFT_R], a[2 * FFT_R:]
        tc = jnp.concatenate([tc] * tiles, axis=1)
        ts = jnp.concatenate([ts] * tiles, axis=1)
        rows = pl.ds(beta * pitch_r, FFT_R)
        scatter(ar_ref, rows, re * tc + im * ts)
        scatter(ai_ref, rows, im * tc - re * ts)
        return carry

    lax.fori_loop(0, half, stage1, 0, unroll=True)

    def stage2(gamma, carry):
        parts = []
        for p in range(2):
            rows = pl.ds(2 * gamma + p, half, stride=pitch_r)
            parts += [gather(ar_ref, rows), gather(ai_ref, rows)]
        z = _dot(wc, jnp.concatenate(parts, axis=0))
        rows = pl.ds(gamma, FFT_R, stride=pitch_h)
        scatter(zr_ref, rows, z[0:2 * FFT_R])
        scatter(zi_ref, rows, z[2 * FFT_R:])
        return carry

    lax.fori_loop(0, half, stage2, 0, unroll=True)

    cs = cs_ref[...]
    d_per_block = 8
    for d0 in range(0, FFT_R, d_per_block):
        z = jnp.concatenate(
            [jnp.concatenate([gather(ref, slice(d * pitch_h, d * pitch_h + half))
                              for ref in (zr_ref, zi_ref)], axis=1)
             for d in range(d0, d0 + d_per_block)], axis=0)
        o_ref[d0 * FFT_R:(d0 + d_per_block) * FFT_R, :] = _dot(z, cs).astype(BF16)


def _fourier_latent_call(plain, wa, wc, tw_cos, tw_sin, chan_cs, *, n_batch):
    n_pos = FFT_R * FFT_R
    tiles = GROUP_W // LANES
    const = lambda shape: pl.BlockSpec(shape, lambda n, q: (0,) * len(shape))
    half = FFT_R // 2
    by_half = pltpu.VMEM((tiles, FFT_R * (half + STRIDE_PAD), LANES), jnp.int32)
    by_r = pltpu.VMEM((tiles, half * (FFT_R + STRIDE_PAD), LANES), jnp.int32)
    return pl.pallas_call(
        _fourier_latent_kernel,
        grid=(n_batch, FOURIER_GROUPS),
        in_specs=[
            pl.BlockSpec((n_pos, GROUP_W), lambda n, q: (n, PLAIN_F * FOURIER_GROUPS + q)),
            const((4 * FFT_R, 2 * FFT_R)),
            const((4 * FFT_R, 4 * FFT_R)),
            const((n_pos, LANES)),
            const((n_pos, LANES)),
            const((2 * GROUP_W, GROUP_W)),
        ],
        out_specs=pl.BlockSpec((n_pos, GROUP_W), lambda n, q: (n, q)),
        out_shape=jax.ShapeDtypeStruct((n_batch * n_pos, BRANCH_W), BF16),
        scratch_shapes=[by_half, by_r, by_r, by_half, by_half],
        compiler_params=_vmem(48 << 20),
        name="fourier_latent",
    )(plain, wa, wc, tw_cos, tw_sin, chan_cs)


def _merge_kernel(x_ref, mod_ref, attn_ref, conv_ref, before_ref, after_ref, four_ref, gate_ref,
                  wconv_ref, wbr_ref, wo_ref, ln2_ref, o_ref, h2_ref, *, seq_len):
    tm, d = x_ref.shape
    i = pl.program_id(0)
    pos0 = (i * tm) % seq_len
    cc_cols = slice(PLAIN_CC * BRANCH_W, (PLAIN_CC + 1) * BRANCH_W)
    cx_cols = slice(PLAIN_CX * BRANCH_W, (PLAIN_CX + 1) * BRANCH_W)
    cb_cols = slice(PLAIN_CB * BRANCH_W, (PLAIN_CB + 1) * BRANCH_W)

    u = conv_ref[:, cc_cols].astype(F32) * conv_ref[:, cx_cols].astype(F32)
    last = slice(SUBLANES - 1, SUBLANES)
    u_before = before_ref[last, cc_cols].astype(F32) * before_ref[last, cx_cols].astype(F32)
    u_after = after_ref[0:1, cc_cols].astype(F32) * after_ref[0:1, cx_cols].astype(F32)
    u_before = jnp.where(pos0 == 0, 0.0, u_before)
    u_after = jnp.where(pos0 + tm == seq_len, 0.0, u_after)
    row = lax.broadcasted_iota(jnp.int32, (tm, 1), 0)
    u_prev = jnp.where(row == 0, u_before, pltpu.roll(u, 1, 0))
    u_next = jnp.where(row == tm - 1, u_after, pltpu.roll(u, tm - 1, 0))
    w = wconv_ref[0]
    conv = (conv_ref[:, cb_cols].astype(F32)
            * (u_prev * w[0:1, :] + u * w[1:2, :] + u_next * w[2:3, :]))

    gate = lambda branch: gate_ref[:, branch * d:(branch + 1) * d].astype(F32)
    mix = (gate(0) * _dot(attn_ref[...], wbr_ref[0, 0])
           + gate(1) * _dot(conv.astype(BF16), wbr_ref[0, 1])
           + gate(2) * _dot(four_ref[...], wbr_ref[0, 2]))
    y = _dot(mix.astype(BF16), wo_ref[0])
    x_new = x_ref[...] + mod_ref[0, 0, 2:3, :] * y
    o_ref[...] = x_new
    h2 = _rms_modulate(x_new, ln2_ref[0], mod_ref[0, 0, 3:4, :], mod_ref[0, 0, 4:5, :])
    h2_ref[...] = h2.astype(BF16)


def _merge_call(x2d, mod, layer, mod_row, attn_o, plain, gates, four_o, w_conv, w_br, w_o,
                ln2_g, *, seq_len):
    m, d = x2d.shape
    tm = 256
    halo_per_tile = tm // SUBLANES
    last_halo = m // SUBLANES - 1
    const = dict(pipeline_mode=pl.Buffered(1))
    assert (PLAIN_CC, PLAIN_CX, PLAIN_CB) == (0, 1, 2)

    return pl.pallas_call(
        functools.partial(_merge_kernel, seq_len=seq_len),
        grid=(m // tm,),
        in_specs=[
            pl.BlockSpec((tm, d), lambda i: (i, 0)),
            pl.BlockSpec((1, 1, 6, d), lambda i: (layer, mod_row(i * tm), 0, 0)),
            pl.BlockSpec((tm, BRANCH_W), lambda i: (i, 0)),
            pl.BlockSpec((tm, 3 * BRANCH_W), lambda i: (i, 0)),
            pl.BlockSpec((SUBLANES, 2 * BRANCH_W),
                         lambda i: (jnp.maximum(i * halo_per_tile - 1, 0), 0)),
            pl.BlockSpec((SUBLANES, 2 * BRANCH_W),
                         lambda i: (jnp.minimum((i + 1) * halo_per_tile, last_halo), 0)),
            pl.BlockSpec((tm, BRANCH_W), lambda i: (i, 0)),
            pl.BlockSpec((tm, N_BRANCH * d), lambda i: (i, 0)),
            pl.BlockSpec((1, CONV_K, BRANCH_W), lambda i: (layer, 0, 0)),
            pl.BlockSpec((1, N_BRANCH, BRANCH_W, d), lambda i: (layer, 0, 0, 0), **const),
            pl.BlockSpec((1, d, d), lambda i: (layer, 0, 0), **const),
            pl.BlockSpec((1, 1, d), lambda i: (layer, 0, 0)),
        ],
        out_specs=[pl.BlockSpec((tm, d), lambda i: (i, 0))] * 2,
        out_shape=[jax.ShapeDtypeStruct((m, d), F32), jax.ShapeDtypeStruct((m, d), BF16)],
        compiler_params=_vmem(56 << 20),
        name="merge",
    )(x2d, mod, attn_o, plain, plain, plain, four_o, gates, w_conv, w_br, w_o,
      ln2_g.reshape(ln2_g.shape[0], 1, d))


def _mlp_kernel(x_ref, h_ref, mod_ref, w1_ref, w2_ref, o_ref, *, out_chunk, x_parts):
    j = pl.program_id(1)

    @pl.when(j == 0)
    def _():
        o_ref[...] = jnp.zeros_like(o_ref)

    h = h_ref[...]
    half = w1_ref.shape[2] // 2
    up = lambda s: jnp.maximum(_dot(h, w1_ref[0, :, s * half:(s + 1) * half]), 0.0)
    hid = [up(0), up(1)]
    hid2 = [(v * v).astype(BF16) for v in hid]
    for n0 in range(0, o_ref.shape[1], out_chunk):
        cols = slice(n0, n0 + out_chunk)
        down = (_dot(hid2[0], w2_ref[0, 0:half, cols])
                + _dot(hid2[1], w2_ref[0, half:, cols]))
        o_ref[:, cols] += mod_ref[0, 0, 5:6, cols] * down

    part = j - (pl.num_programs(1) - x_parts)

    @pl.when(part >= 0)
    def _():
        part_rows = x_ref.shape[0]
        rows = pl.ds(pl.multiple_of(part * part_rows, part_rows), part_rows)
        o_ref[rows, :] += x_ref[...]


def _mlp_call(x2d, h2, mod, layer, mod_row, w1, w2):
    m, d = x2d.shape
    d_ff = w1.shape[2]
    tm, tf = 1024, 1024
    steps = d_ff // tf
    x_parts = 4

    def x_rows(i, j):
        part = j - (steps - x_parts)
        return jnp.where(part >= 0, i * x_parts + part, jnp.maximum(i * x_parts - 1, 0)), 0

    return pl.pallas_call(
        functools.partial(_mlp_kernel, out_chunk=512, x_parts=x_parts),
        grid=(m // tm, steps),
        in_specs=[
            pl.BlockSpec((tm // x_parts, d), x_rows),
            pl.BlockSpec((tm, d), lambda i, j: (i, 0)),
            pl.BlockSpec((1, 1, 6, d), lambda i, j: (layer, mod_row(i * tm), 0, 0)),
            pl.BlockSpec((1, d, tf), lambda i, j: (layer, 0, j)),
            pl.BlockSpec((1, tf, d), lambda i, j: (layer, j, 0)),
        ],
        out_specs=pl.BlockSpec((tm, d), lambda i, j: (i, 0)),
        out_shape=jax.ShapeDtypeStruct((m, d), F32),
        compiler_params=_vmem(56 << 20),
        name="mlp",
    )(x2d, h2, mod, w1, w2)


def _rope_tables(seq_len):
    rows = seq_len // GRID_W
    t_row = jnp.repeat(jnp.arange(rows), GRID_W).astype(F32)
    t_col = jnp.tile(jnp.arange(GRID_W), rows).astype(F32)
    axis_dim = HEAD_DIM // 2
    inv_freq = 1.0 / (ROPE_THETA ** (jnp.arange(0, axis_dim, 2, dtype=F32) / axis_dim))
    ang_r = t_row[:, None] * inv_freq[None, :]
    ang_c = t_col[:, None] * inv_freq[None, :]
    cos = jnp.concatenate([jnp.cos(ang_r)] * 2 + [jnp.cos(ang_c)] * 2, axis=-1)
    sin = jnp.concatenate([-jnp.sin(ang_r), jnp.sin(ang_r), -jnp.sin(ang_c), jnp.sin(ang_c)],
                          axis=-1)
    return jnp.tile(cos, (1, LANES // HEAD_DIM)), jnp.tile(sin, (1, LANES // HEAD_DIM))


def _dft_cos_sin(rows, cols, period, scale):
    r = jnp.arange(rows, dtype=jnp.int32)
    c = jnp.arange(cols, dtype=jnp.int32)
    ang = ((r[:, None] * c[None, :]) % period).astype(F32) * (2.0 * math.pi / period)
    return jnp.cos(ang) * scale, jnp.sin(ang) * scale


def kernel(x, c, ctx, c_ctx, ln1_g, ln2_g, w_ada, b_ada, w_in, qn_g, kn_g, lam_q, lam_k,
           subln_g, w_conv, w_br, w_o, w1, w2):
    n_batch, seq, d = x.shape
    ctx_len = ctx.shape[1]
    depth = w_ada.shape[0]
    assert seq == FFT_R * FFT_R and ctx_len == 256 and n_batch <= CTX_MOD_ROW and d == D_MODEL

    c_rows = jnp.zeros((MOD_ROWS, d), F32).at[:n_batch].set(c).at[CTX_MOD_ROW].set(c_ctx)
    mod = _mod_call(c_rows, w_ada, b_ada).reshape(depth, MOD_ROWS, 6, d)

    cos_l, sin_l = _rope_tables(seq)
    cos_c = jnp.ones((n_batch * ctx_len, LANES), F32)
    sin_c = jnp.zeros((n_batch * ctx_len, LANES), F32)
    qg = jnp.tile(qn_g, (1, MXU_W // HEAD_DIM)).reshape(depth, 1, MXU_W)
    kg = jnp.tile(kn_g, (1, MXU_W // HEAD_DIM)).reshape(depth, 1, MXU_W)

    ch_c, ch_s = _dft_cos_sin(GROUP_W, GROUP_W, GROUP_W, GROUP_W ** -0.5)
    chan_cos, chan_sin = ch_c.astype(BF16), ch_s.astype(BF16)
    chan_cs = jnp.concatenate([chan_cos, chan_sin], axis=0)
    pc_c, pc_s = _dft_cos_sin(ctx_len, ctx_len, ctx_len, ctx_len ** -0.5)
    pos_c = jnp.concatenate([pc_c, -pc_s], axis=1).astype(BF16)
    r_c, r_s = _dft_cos_sin(FFT_R, FFT_R, FFT_R, FFT_R ** -0.5)
    eye2 = jnp.eye(2, dtype=F32)
    wa = jnp.kron(jnp.concatenate([r_c, -r_s], axis=0), eye2).astype(BF16)
    wc = jnp.concatenate([jnp.concatenate([r_c, r_s], axis=1),
                          jnp.concatenate([-r_s, r_c], axis=1)], axis=0)
    wc = jnp.einsum("odib,pq->odpqib", wc.reshape(2, FFT_R, 2, FFT_R), eye2)
    wc = wc.reshape(4 * FFT_R, 4 * FFT_R).astype(BF16)
    tw_c, tw_s = _dft_cos_sin(FFT_R, FFT_R, seq, 1.0)
    pair_order = lambda t: jnp.broadcast_to(
        t.reshape(FFT_R // 2, 2, FFT_R).transpose(0, 2, 1).reshape(seq, 1), (seq, LANES))
    tw_cos, tw_sin = pair_order(tw_c), pair_order(tw_s)

    w_br_b = w_br.astype(BF16)
    w_o_b = w_o.astype(BF16)
    w1_b = w1.astype(BF16)
    w2_b = w2.astype(BF16)

    lat_row = lambda row: row // seq
    ctx_row = lambda row: CTX_MOD_ROW
    heads = COL_TILE // LANES

    xl = x.reshape(n_batch * seq, d)
    xc = ctx.reshape(n_batch * ctx_len, d)

    for l in range(depth):
        last = l == depth - 1
        lam_init = 0.8 - 0.6 * math.exp(-0.3 * l)

        hc = _normmod_call(xc, mod, l, ctx_row, ln1_g)
        hl = _normmod_call(xl, mod, l, lat_row, ln1_g)
        qk_c = _proj_call("qk", hc, w_in, l, QK_TILES, (qg, kg, cos_c, sin_c))
        qk_l = _proj_call("qk", hl, w_in, l, QK_TILES, (qg, kg, cos_l, sin_l))
        plain_c = _proj_call("plain", hc, w_in, l, (TILE_V,) if last else PLAIN_TILES)
        plain_l = _proj_call("plain", hl, w_in, l, PLAIN_TILES)
        gates_l = _proj_call("gate", hl, w_in, l, GATE_TILES)

        kv_ctx = (qk_c, heads, plain_c, (0 if last else PLAIN_V) * heads, ctx_len)
        kv_lat = (qk_l, heads, plain_l, PLAIN_V * heads, seq)
        attn_l = _attn_call(lam_q, lam_k, subln_g, l, qk_l, 0, [kv_ctx, kv_lat],
                            n_batch=n_batch, q_len=seq, lam_init=lam_init)
        four_l = _fourier_latent_call(plain_l, wa, wc, tw_cos, tw_sin, chan_cs, n_batch=n_batch)
        xl, h2l = _merge_call(xl, mod, l, lat_row, attn_l, plain_l, gates_l, four_l, w_conv,
                              w_br_b, w_o_b, ln2_g, seq_len=seq)
        xl = _mlp_call(xl, h2l, mod, l, lat_row, w1_b, w2_b)

        if not last:
            gates_c = _proj_call("gate", hc, w_in, l, GATE_TILES)
            attn_c = _attn_call(lam_q, lam_k, subln_g, l, qk_c, 0, [kv_ctx],
                                n_batch=n_batch, q_len=ctx_len, lam_init=lam_init)
            four_c = _fourier_dense_call(plain_c, PLAIN_F, chan_cos, chan_sin, pos_c,
                                         n_batch=n_batch, n_pos=ctx_len)
            xc, h2c = _merge_call(xc, mod, l, ctx_row, attn_c, plain_c, gates_c, four_c, w_conv,
                                  w_br_b, w_o_b, ln2_g, seq_len=ctx_len)
            xc = _mlp_call(xc, h2c, mod, l, ctx_row, w1_b, w2_b)

    return xl.reshape(n_batch, seq, d)
```

```python
import functools
import math

import jax
import jax.numpy as jnp
from jax import lax
from jax.experimental import pallas as pl
from jax.experimental.pallas import tpu as pltpu

D_MODEL = 2048
GRID_W = 64
N_HEADS = 8
HEAD_DIM = 64
V_HEAD_DIM = 128
BRANCH_W = 1024
FOURIER_GROUPS = 4
GROUP_W = BRANCH_W // FOURIER_GROUPS
N_BRANCH = 3
CONV_K = 3
EPS = 1e-6
ROPE_THETA = 10000.0

COL_TILE = 1024
PROJ_ROW_CHUNK = 1024
TILE_Q, TILE_K, TILE_V, TILE_CB, TILE_CC, TILE_CX, TILE_F, TILE_G = range(8)
QK_TILES = (TILE_Q, TILE_K)
PLAIN_TILES = (TILE_CC, TILE_CX, TILE_CB, TILE_F)
PLAIN_CC, PLAIN_CX, PLAIN_CB, PLAIN_F = range(4)
GATE_TILES = tuple(range(TILE_G, TILE_G + N_BRANCH * D_MODEL // COL_TILE))

LANES = 128
SUBLANES = 8
MXU_W = 256
ONES_ROWS = 16
LOG2_E = math.log2(math.e)
MOD_ROWS = 8
CTX_MOD_ROW = 4
FFT_R = 64
STRIDE_PAD = 4

F32 = jnp.float32
BF16 = jnp.bfloat16


def _vmem(nbytes):
    return pltpu.CompilerParams(vmem_limit_bytes=nbytes)


def _sigmoid(x):
    return 1.0 / (1.0 + jnp.exp(-x))


def _rms_modulate(x, g, shift, scale):
    y = x * lax.rsqrt(jnp.mean(x * x, axis=-1, keepdims=True) + EPS) * g
    return y * (1.0 + scale) + shift


def _dot(a, b):
    return jnp.dot(a, b, preferred_element_type=F32)


def _dot_t(a, b):
    return lax.dot_general(a, b, (((1,), (1,)), ((), ())), preferred_element_type=F32)


def _mod_kernel(c_ref, w_ref, b_ref, o_ref):
    c = c_ref[...]
    cs = c * _sigmoid(c)
    split = lambda v: (v.astype(BF16), (v - v.astype(BF16).astype(F32)).astype(BF16))
    cs_hi, cs_lo = split(cs)
    w_hi, w_lo = split(w_ref[0])
    o_ref[0] = (_dot(cs_hi, w_hi) + (_dot(cs_lo, w_hi) + _dot(cs_hi, w_lo))) + b_ref[0]


def _mod_call(c_rows, w_ada, b_ada):
    depth, d, n = w_ada.shape
    tn = 1536
    return pl.pallas_call(
        _mod_kernel,
        grid=(depth, n // tn),
        in_specs=[
            pl.BlockSpec((MOD_ROWS, d), lambda l, j: (0, 0)),
            pl.BlockSpec((1, d, tn), lambda l, j: (l, 0, j)),
            pl.BlockSpec((1, 1, tn), lambda l, j: (l, 0, j)),
        ],
        out_specs=pl.BlockSpec((1, MOD_ROWS, tn), lambda l, j: (l, 0, j)),
        out_shape=jax.ShapeDtypeStruct((depth, MOD_ROWS, n), F32),
        compiler_params=_vmem(40 << 20),
        name="mod",
    )(c_rows, w_ada, b_ada.reshape(depth, 1, n))


def _normmod_kernel(x_ref, mod_ref, g_ref, o_ref):
    h = _rms_modulate(x_ref[...], g_ref[0], mod_ref[0, 0, 0:1, :], mod_ref[0, 0, 1:2, :])
    o_ref[...] = h.astype(BF16)


def _normmod_call(x2d, mod, layer, mod_row, ln_g):
    m, d = x2d.shape
    tm = 1024
    return pl.pallas_call(
        _normmod_kernel,
        grid=(m // tm,),
        in_specs=[
            pl.BlockSpec((tm, d), lambda i: (i, 0)),
            pl.BlockSpec((1, 1, 6, d), lambda i: (layer, mod_row(i * tm), 0, 0)),
            pl.BlockSpec((1, 1, d), lambda i: (layer, 0, 0)),
        ],
        out_specs=pl.BlockSpec((tm, d), lambda i: (i, 0)),
        out_shape=jax.ShapeDtypeStruct((m, d), BF16),
        compiler_params=_vmem(32 << 20),
        name="normmod",
    )(x2d, mod, ln_g.reshape(ln_g.shape[0], 1, d))


def _cast_weights(w_ref, wb_ref):
    @pl.when(pl.program_id(1) == 0)
    def _():
        wb_ref[...] = w_ref[0].astype(BF16)


def _project_rows(h_ref, wb_ref, o_ref, epilogue):
    for r in range(0, h_ref.shape[0], PROJ_ROW_CHUNK):
        rows = slice(r, r + PROJ_ROW_CHUNK)
        o_ref[rows, :] = epilogue(_dot(h_ref[rows, :], wb_ref[...])).astype(BF16)


def _proj_vt_kernel(h_ref, w_ref, o_ref, wb_ref):
    _cast_weights(w_ref, wb_ref)
    for r in range(0, h_ref.shape[0], PROJ_ROW_CHUNK):
        rows = slice(r, r + PROJ_ROW_CHUNK)
        o_ref[:, rows] = _dot(h_ref[rows, :], wb_ref[...]).T.astype(BF16)


def _proj_plain_kernel(h_ref, w_ref, o_ref, wb_ref):
    _cast_weights(w_ref, wb_ref)
    _project_rows(h_ref, wb_ref, o_ref, lambda acc: acc)


def _proj_gate_kernel(h_ref, w_ref, o_ref, wb_ref):
    _cast_weights(w_ref, wb_ref)
    _project_rows(h_ref, wb_ref, o_ref, lambda acc: 0.5 * jnp.tanh(0.5 * acc) + 0.5)


def _proj_qk_kernel(h_ref, w_ref, qg_ref, kg_ref, cos_ref, sin_ref, o_ref, wb_ref):
    _cast_weights(w_ref, wb_ref)
    items = [(r, t) for r in range(0, h_ref.shape[0], PROJ_ROW_CHUNK)
             for t in range(COL_TILE // MXU_W)]
    project = lambda r, t: _dot(h_ref[r:r + PROJ_ROW_CHUNK, :],
                                wb_ref[:, t * MXU_W:(t + 1) * MXU_W])
    is_q = pl.program_id(0) == TILE_Q
    gain = jnp.where(is_q, qg_ref[0] * (LOG2_E * HEAD_DIM ** -0.5), kg_ref[0])
    lane = lax.broadcasted_iota(jnp.int32, (1, LANES), 1)
    first_half = (lane % (HEAD_DIM // 2)) < (HEAD_DIM // 4)
    gsum = (lax.broadcasted_iota(jnp.int32, (MXU_W, MXU_W), 0) // HEAD_DIM
            == lax.broadcasted_iota(jnp.int32, (MXU_W, MXU_W), 1) // HEAD_DIM).astype(BF16)
    a_next = project(*items[0])
    for idx, (r, t) in enumerate(items):
        a = a_next
        if idx + 1 < len(items):
            a_next = project(*items[idx + 1])
        rows = slice(r, r + PROJ_ROW_CHUNK)
        cos = cos_ref[rows, :]
        sin = sin_ref[rows, :]
        ssq = _dot((a * a).astype(BF16), gsum)
        y = a * lax.rsqrt(ssq * (1.0 / HEAD_DIM) + EPS) * gain
        for u in range(MXU_W // LANES):
            yh = y[:, u * LANES:(u + 1) * LANES]
            partner = jnp.where(first_half,
                                pltpu.roll(yh, LANES - HEAD_DIM // 4, 1),
                                pltpu.roll(yh, HEAD_DIM // 4, 1))
            c0 = t * MXU_W + u * LANES
            o_ref[rows, c0:c0 + LANES] = (yh * cos + partner * sin).astype(BF16)


def _proj_call(kind, h, w_in, layer, w_tiles, extra=()):
    m, d = h.shape
    tm = 2 * PROJ_ROW_CHUNK if m % (2 * PROJ_ROW_CHUNK) == 0 else PROJ_ROW_CHUNK
    n_tiles = len(w_tiles)

    def w_tile(j):
        tile = w_tiles[0]
        for idx in range(1, n_tiles):
            tile = jnp.where(j == idx, w_tiles[idx], tile)
        return tile

    in_specs = [
        pl.BlockSpec((tm, d), lambda j, i: (i, 0)),
        pl.BlockSpec((1, d, COL_TILE), lambda j, i: (layer, 0, w_tile(j))),
    ]
    args = [h, w_in]
    if kind == "qk":
        qg, kg, cos_t, sin_t = extra
        seq_tiles = cos_t.shape[0] // tm
        in_specs += [
            pl.BlockSpec((1, 1, MXU_W), lambda j, i: (layer, 0, 0)),
            pl.BlockSpec((1, 1, MXU_W), lambda j, i: (layer, 0, 0)),
            pl.BlockSpec((tm, LANES), lambda j, i: (i % seq_tiles, 0)),
            pl.BlockSpec((tm, LANES), lambda j, i: (i % seq_tiles, 0)),
        ]
        args += [qg, kg, cos_t, sin_t]
    body = {"qk": _proj_qk_kernel, "plain": _proj_plain_kernel, "gate": _proj_gate_kernel,
            "vt": _proj_vt_kernel}[kind]
    if kind == "vt":
        out_spec = pl.BlockSpec((COL_TILE, tm), lambda j, i: (j, i))
        out_shape = jax.ShapeDtypeStruct((n_tiles * COL_TILE, m), BF16)
    else:
        out_spec = pl.BlockSpec((tm, COL_TILE), lambda j, i: (i, j))
        out_shape = jax.ShapeDtypeStruct((m, n_tiles * COL_TILE), BF16)
    return pl.pallas_call(
        body,
        grid=(n_tiles, m // tm),
        in_specs=in_specs,
        out_specs=out_spec,
        out_shape=out_shape,
        scratch_shapes=[pltpu.VMEM((d, COL_TILE), BF16)],
        compiler_params=_vmem(56 << 20),
        name="proj_" + kind,
    )(*args)


def _attn_kernel(lq_ref, lk_ref, sg_ref, q_ref, *refs, lam_init, pieces, key_block):
    n = len(pieces)
    k_refs, v_refs = refs[0:2 * n:2], refs[1:2 * n:2]
    o_ref, vt_ref, kmax_ref = refs[2 * n:]
    tq = q_ref.shape[0]
    lane = lax.broadcasted_iota(jnp.int32, (1, LANES), 1)

    @pl.when(pl.program_id(2) == 0)
    def _():
        group = lax.broadcasted_iota(jnp.int32, (LANES, LANES), 0) // HEAD_DIM
        gsum = (group == lane // HEAD_DIM).astype(BF16)
        kmax = None
        off = 0
        for t in range(n):
            rows = pieces[t]
            vt_ref[0:V_HEAD_DIM, off:off + rows] = v_refs[t][...]
            kf = k_refs[t][...].astype(F32)
            norms = _dot((kf * kf).astype(BF16), gsum)
            piece_max = jnp.max(norms, axis=0, keepdims=True)
            kmax = piece_max if kmax is None else jnp.maximum(kmax, piece_max)
            off += rows
        vt_ref[V_HEAD_DIM:, :] = jnp.ones((ONES_ROWS, off), BF16)
        kmax_ref[...] = kmax

    e = jnp.exp(jnp.sum(lq_ref[0] * lk_ref[0], axis=-1, keepdims=True))
    lam = e[0:1, :] - e[1:2, :] + lam_init

    q = q_ref[...]
    zero = jnp.zeros_like(q)
    lo = lane < HEAD_DIM
    q_both = jnp.concatenate([jnp.where(lo, q, zero), jnp.where(lo, zero, q)], axis=0)

    qf = q.astype(F32)
    sel = (lax.broadcasted_iota(jnp.int32, (SUBLANES, LANES), 0) == lane // HEAD_DIM).astype(BF16)
    q_norms = _dot_t(sel, (qf * qf).astype(BF16))
    kmax = kmax_ref[...]
    bounds = jnp.concatenate(
        [jnp.sqrt(q_norms[c:c + 1, :] * kmax[:, c * HEAD_DIM:c * HEAD_DIM + 1]) * 1.01 + 1e-6
         for c in range(2)], axis=1)

    def key_blocks():
        off = 0
        for t in range(n):
            for s in range(0, pieces[t], key_block):
                size = min(key_block, pieces[t] - s)
                yield k_refs[t], s, off + s, size
            off += pieces[t]

    def accumulate(shifts):
        blocks = list(key_blocks())
        scores = lambda blk: _dot_t(blk[0][blk[1]:blk[1] + blk[3], :], q_both)
        acc = jnp.zeros((V_HEAD_DIM + ONES_ROWS, 2 * tq), F32)
        st_next = scores(blocks[0])
        for idx, (_, _, pos, size) in enumerate(blocks):
            st = st_next
            if idx + 1 < len(blocks):
                st_next = scores(blocks[idx + 1])
            acc = acc + _dot(vt_ref[:, pos:pos + size], jnp.exp2(st - shifts).astype(BF16))
        return acc

    def finish(acc):
        sums = acc[V_HEAD_DIM:V_HEAD_DIM + 1, :]
        ot = (acc[0:V_HEAD_DIM, 0:tq] * (1.0 / sums[:, 0:tq])
              - acc[0:V_HEAD_DIM, tq:] * (lam / sums[:, tq:]))
        yt = ot * lax.rsqrt(jnp.mean(ot * ot, axis=0, keepdims=True) + EPS)
        o_ref[...] = (yt.T * (sg_ref[0] * (1.0 - lam_init))).astype(BF16)

    acc = accumulate(bounds)
    finish(acc)

    @pl.when(jnp.logical_not(jnp.min(acc[V_HEAD_DIM:V_HEAD_DIM + 1, :]) >= 1e-30))
    def _():
        m = None
        for k_ref, s, _, size in key_blocks():
            bm = jnp.max(_dot_t(k_ref[s:s + size, :], q_both), axis=0, keepdims=True)
            m = bm if m is None else jnp.maximum(m, bm)
        finish(accumulate(m))


def _attn_call(lam_q, lam_k, subln_g, layer, q_arr, q_tile0, kv_list, *, n_batch, q_len,
               lam_init):
    tq = min(1024, q_len)
    q_tiles = q_len // tq
    in_specs = [
        pl.BlockSpec((1, 2, HEAD_DIM), lambda b, h, i: (layer, 0, 0)),
        pl.BlockSpec((1, 2, HEAD_DIM), lambda b, h, i: (layer, 0, 0)),
        pl.BlockSpec((1, 1, V_HEAD_DIM), lambda b, h, i: (layer, 0, 0)),
        pl.BlockSpec((tq, LANES), lambda b, h, i: (b * q_tiles + i, q_tile0 + h)),
    ]
    args = [lam_q, lam_k, subln_g.reshape(subln_g.shape[0], 1, V_HEAD_DIM), q_arr]
    for k_arr, k0, v_arr, v0, rows in kv_list:
        in_specs.append(pl.BlockSpec((rows, LANES), lambda b, h, i, k0=k0: (b, k0 + h)))
        in_specs.append(pl.BlockSpec((V_HEAD_DIM, rows), lambda b, h, i, v0=v0: (v0 + h, b)))
        args += [k_arr, v_arr]
    pieces = tuple(kv[4] for kv in kv_list)
    return pl.pallas_call(
        functools.partial(_attn_kernel, lam_init=lam_init, pieces=pieces, key_block=512),
        grid=(n_batch, N_HEADS, q_tiles),
        in_specs=in_specs,
        out_specs=pl.BlockSpec((tq, LANES), lambda b, h, i: (b * q_tiles + i, h)),
        out_shape=jax.ShapeDtypeStruct((n_batch * q_len, N_HEADS * V_HEAD_DIM), BF16),
        scratch_shapes=[pltpu.VMEM((V_HEAD_DIM + ONES_ROWS, sum(pieces)), BF16),
                        pltpu.VMEM((1, LANES), F32)],
        compiler_params=_vmem(52 << 20),
        name="attn",
    )(*args)


def _fourier_dense_kernel(x_ref, cc_ref, sc_ref, pos_ref, o_ref, y_ref, *, n_pos):
    @pl.when(pl.program_id(1) == 0)
    def _():
        cc = cc_ref[...]
        sc = sc_ref[...]
        for g in range(FOURIER_GROUPS):
            xg = x_ref[:, g * GROUP_W:(g + 1) * GROUP_W]
            y_ref[0:n_pos, g * GROUP_W:(g + 1) * GROUP_W] = _dot(xg, cc).astype(BF16)
            y_ref[n_pos:2 * n_pos, g * GROUP_W:(g + 1) * GROUP_W] = _dot(xg, sc).astype(BF16)

    o_ref[...] = _dot(pos_ref[...], y_ref[...]).astype(BF16)


def _fourier_dense_call(p_arr, f_tile, chan_cos, chan_sin, pos_mat, *, n_batch, n_pos):
    tm = min(256, n_pos)
    row_tiles = n_pos // tm
    return pl.pallas_call(
        functools.partial(_fourier_dense_kernel, n_pos=n_pos),
        grid=(n_batch, row_tiles),
        in_specs=[
            pl.BlockSpec((n_pos, BRANCH_W), lambda b, i: (b, f_tile),
                         pipeline_mode=pl.Buffered(1)),
            pl.BlockSpec((GROUP_W, GROUP_W), lambda b, i: (0, 0)),
            pl.BlockSpec((GROUP_W, GROUP_W), lambda b, i: (0, 0)),
            pl.BlockSpec((tm, 2 * n_pos), lambda b, i: (i, 0)),
        ],
        out_specs=pl.BlockSpec((tm, BRANCH_W), lambda b, i: (b * row_tiles + i, 0)),
        out_shape=jax.ShapeDtypeStruct((n_batch * n_pos, BRANCH_W), BF16),
        scratch_shapes=[pltpu.VMEM((2 * n_pos, BRANCH_W), BF16)],
        compiler_params=_vmem(32 << 20),
        name="fourier_dense",
    )(p_arr, chan_cos, chan_sin, pos_mat)


def _fourier_latent_kernel(x_ref, wa_ref, wc_ref, twc_ref, tws_ref, cs_ref, o_ref,
                           xs_ref, ar_ref, ai_ref, zr_ref, zi_ref):
    tiles = GROUP_W // LANES
    half = FFT_R // 2
    lane_tile = lambda v, t: v[:, t * LANES:(t + 1) * LANES]
    pack = lambda v: pltpu.bitcast(v.astype(BF16), jnp.int32)

    def gather(ref, rows):
        words = jnp.concatenate([ref[t, rows, :] for t in range(tiles)], axis=1)
        return pltpu.bitcast(words, BF16)

    def scatter(ref, rows, v):
        words = pack(v)
        for t in range(tiles):
            ref[t, rows, :] = lane_tile(words, t)

    pitch_h = half + STRIDE_PAD
    pitch_r = FFT_R + STRIDE_PAD

    for a in range(FFT_R):
        words = pltpu.bitcast(x_ref[a * FFT_R:(a + 1) * FFT_R, :], jnp.int32)
        for t in range(tiles):
            xs_ref[t, a * pitch_h:a * pitch_h + half, :] = lane_tile(words, t)
    wa = wa_ref[...]
    wc = wc_ref[...]

    def stage1(beta, carry):
        xb = gather(xs_ref, pl.ds(beta, FFT_R, stride=pitch_h))
        a = _dot(wa, xb)
        tw_rows = pl.ds(pl.multiple_of(beta * 2 * FFT_R, 2 * FFT_R), 2 * FFT_R)
        tc = twc_ref[tw_rows, :]
        ts = tws_ref[tw_rows, :]
        re, im = a[0:2 * FFT_R], a[2 * FFT_R:]
        tc = jnp.concatenate([tc] * tiles, axis=1)
        ts = jnp.concatenate([ts] * tiles, axis=1)
        rows = pl.ds(beta * pitch_r, FFT_R)
        scatter(ar_ref, rows, re * tc + im * ts)
        scatter(ai_ref, rows, im * tc - re * ts)
        return carry

    lax.fori_loop(0, half, stage1, 0, unroll=True)

    def stage2(gamma, carry):
        parts = []
        for p in range(2):
            rows = pl.ds(2 * gamma + p, half, stride=pitch_r)
            parts += [gather(ar_ref, rows), gather(ai_ref, rows)]
        z = _dot(wc, jnp.concatenate(parts, axis=0))
        rows = pl.ds(gamma, FFT_R, stride=pitch_h)
        scatter(zr_ref, rows, z[0:2 * FFT_R])
        scatter(zi_ref, rows, z[2 * FFT_R:])
        return carry

    lax.fori_loop(0, half, stage2, 0, unroll=True)

    cs = cs_ref[...]
    d_per_block = 8
    for d0 in range(0, FFT_R, d_per_block):
        z = jnp.concatenate(
            [jnp.concatenate([gather(ref, slice(d * pitch_h, d * pitch_h + half))
                              for ref in (zr_ref, zi_ref)], axis=1)
             for d in range(d0, d0 + d_per_block)], axis=0)
        o_ref[d0 * FFT_R:(d0 + d_per_block) * FFT_R, :] = _dot(z, cs).astype(BF16)


def _fourier_latent_call(plain, wa, wc, tw_cos, tw_sin, chan_cs, *, n_batch):
    n_pos = FFT_R * FFT_R
    tiles = GROUP_W // LANES
    const = lambda shape: pl.BlockSpec(shape, lambda n, q: (0,) * len(shape))
    half = FFT_R // 2
    by_half = pltpu.VMEM((tiles, FFT_R * (half + STRIDE_PAD), LANES), jnp.int32)
    by_r = pltpu.VMEM((tiles, half * (FFT_R + STRIDE_PAD), LANES), jnp.int32)
    return pl.pallas_call(
        _fourier_latent_kernel,
        grid=(n_batch, FOURIER_GROUPS),
        in_specs=[
            pl.BlockSpec((n_pos, GROUP_W), lambda n, q: (n, PLAIN_F * FOURIER_GROUPS + q)),
            const((4 * FFT_R, 2 * FFT_R)),
            const((4 * FFT_R, 4 * FFT_R)),
            const((n_pos, LANES)),
            const((n_pos, LANES)),
            const((2 * GROUP_W, GROUP_W)),
        ],
        out_specs=pl.BlockSpec((n_pos, GROUP_W), lambda n, q: (n, q)),
        out_shape=jax.ShapeDtypeStruct((n_batch * n_pos, BRANCH_W), BF16),
        scratch_shapes=[by_half, by_r, by_r, by_half, by_half],
        compiler_params=_vmem(48 << 20),
        name="fourier_latent",
    )(plain, wa, wc, tw_cos, tw_sin, chan_cs)


def _merge_kernel(x_ref, mod_ref, attn_ref, conv_ref, before_ref, after_ref, four_ref, gate_ref,
                  wconv_ref, wbr_ref, wo_ref, ln2_ref, o_ref, h2_ref, *, seq_len):
    tm, d = x_ref.shape
    i = pl.program_id(0)
    pos0 = (i * tm) % seq_len
    cc_cols = slice(PLAIN_CC * BRANCH_W, (PLAIN_CC + 1) * BRANCH_W)
    cx_cols = slice(PLAIN_CX * BRANCH_W, (PLAIN_CX + 1) * BRANCH_W)
    cb_cols = slice(PLAIN_CB * BRANCH_W, (PLAIN_CB + 1) * BRANCH_W)

    u = conv_ref[:, cc_cols].astype(F32) * conv_ref[:, cx_cols].astype(F32)
    last = slice(SUBLANES - 1, SUBLANES)
    u_before = before_ref[last, cc_cols].astype(F32) * before_ref[last, cx_cols].astype(F32)
    u_after = after_ref[0:1, cc_cols].astype(F32) * after_ref[0:1, cx_cols].astype(F32)
    u_before = jnp.where(pos0 == 0, 0.0, u_before)
    u_after = jnp.where(pos0 + tm == seq_len, 0.0, u_after)
    row = lax.broadcasted_iota(jnp.int32, (tm, 1), 0)
    u_prev = jnp.where(row == 0, u_before, pltpu.roll(u, 1, 0))
    u_next = jnp.where(row == tm - 1, u_after, pltpu.roll(u, tm - 1, 0))
    w = wconv_ref[0]
    conv = (conv_ref[:, cb_cols].astype(F32)
            * (u_prev * w[0:1, :] + u * w[1:2, :] + u_next * w[2:3, :]))

    gate = lambda branch: gate_ref[:, branch * d:(branch + 1) * d].astype(F32)
    mix = (gate(0) * _dot(attn_ref[...], wbr_ref[0, 0])
           + gate(1) * _dot(conv.astype(BF16), wbr_ref[0, 1])
           + gate(2) * _dot(four_ref[...], wbr_ref[0, 2]))
    y = _dot(mix.astype(BF16), wo_ref[0])
    x_new = x_ref[...] + mod_ref[0, 0, 2:3, :] * y
    o_ref[...] = x_new
    h2 = _rms_modulate(x_new, ln2_ref[0], mod_ref[0, 0, 3:4, :], mod_ref[0, 0, 4:5, :])
    h2_ref[...] = h2.astype(BF16)


def _merge_call(x2d, mod, layer, mod_row, attn_o, plain, gates, four_o, w_conv, w_br, w_o,
                ln2_g, *, seq_len):
    m, d = x2d.shape
    tm = 256
    halo_per_tile = tm // SUBLANES
    last_halo = m // SUBLANES - 1
    const = dict(pipeline_mode=pl.Buffered(1))
    assert (PLAIN_CC, PLAIN_CX, PLAIN_CB) == (0, 1, 2)

    return pl.pallas_call(
        functools.partial(_merge_kernel, seq_len=seq_len),
        grid=(m // tm,),
        in_specs=[
            pl.BlockSpec((tm, d), lambda i: (i, 0)),
            pl.BlockSpec((1, 1, 6, d), lambda i: (layer, mod_row(i * tm), 0, 0)),
            pl.BlockSpec((tm, BRANCH_W), lambda i: (i, 0)),
            pl.BlockSpec((tm, 3 * BRANCH_W), lambda i: (i, 0)),
            pl.BlockSpec((SUBLANES, 2 * BRANCH_W),
                         lambda i: (jnp.maximum(i * halo_per_tile - 1, 0), 0)),
            pl.BlockSpec((SUBLANES, 2 * BRANCH_W),
                         lambda i: (jnp.minimum((i + 1) * halo_per_tile, last_halo), 0)),
            pl.BlockSpec((tm, BRANCH_W), lambda i: (i, 0)),
            pl.BlockSpec((tm, N_BRANCH * d), lambda i: (i, 0)),
            pl.BlockSpec((1, CONV_K, BRANCH_W), lambda i: (layer, 0, 0)),
            pl.BlockSpec((1, N_BRANCH, BRANCH_W, d), lambda i: (layer, 0, 0, 0), **const),
            pl.BlockSpec((1, d, d), lambda i: (layer, 0, 0), **const),
            pl.BlockSpec((1, 1, d), lambda i: (layer, 0, 0)),
        ],
        out_specs=[pl.BlockSpec((tm, d), lambda i: (i, 0))] * 2,
        out_shape=[jax.ShapeDtypeStruct((m, d), F32), jax.ShapeDtypeStruct((m, d), BF16)],
        compiler_params=_vmem(56 << 20),
        name="merge",
    )(x2d, mod, attn_o, plain, plain, plain, four_o, gates, w_conv, w_br, w_o,
      ln2_g.reshape(ln2_g.shape[0], 1, d))


def _mlp_kernel(x_ref, h_ref, mod_ref, w1_ref, w2_ref, o_ref, *, out_chunk, x_parts):
    j = pl.program_id(1)

    @pl.when(j == 0)
    def _():
        o_ref[...] = jnp.zeros_like(o_ref)

    h = h_ref[...]
    half = w1_ref.shape[2] // 2
    up = lambda s: jnp.maximum(_dot(h, w1_ref[0, :, s * half:(s + 1) * half]), 0.0)
    hid = [up(0), up(1)]
    hid2 = [(v * v).astype(BF16) for v in hid]
    for n0 in range(0, o_ref.shape[1], out_chunk):
        cols = slice(n0, n0 + out_chunk)
        down = (_dot(hid2[0], w2_ref[0, 0:half, cols])
                + _dot(hid2[1], w2_ref[0, half:, cols]))
        o_ref[:, cols] += mod_ref[0, 0, 5:6, cols] * down

    part = j - (pl.num_programs(1) - x_parts)

    @pl.when(part >= 0)
    def _():
        part_rows = x_ref.shape[0]
        rows = pl.ds(pl.multiple_of(part * part_rows, part_rows), part_rows)
        o_ref[rows, :] += x_ref[...]


def _mlp_call(x2d, h2, mod, layer, mod_row, w1, w2):
    m, d = x2d.shape
    d_ff = w1.shape[2]
    tm, tf = 1024, 1024
    steps = d_ff // tf
    x_parts = 4

    def x_rows(i, j):
        part = j - (steps - x_parts)
        return jnp.where(part >= 0, i * x_parts + part, jnp.maximum(i * x_parts - 1, 0)), 0

    return pl.pallas_call(
        functools.partial(_mlp_kernel, out_chunk=512, x_parts=x_parts),
        grid=(m // tm, steps),
        in_specs=[
            pl.BlockSpec((tm // x_parts, d), x_rows),
            pl.BlockSpec((tm, d), lambda i, j: (i, 0)),
            pl.BlockSpec((1, 1, 6, d), lambda i, j: (layer, mod_row(i * tm), 0, 0)),
            pl.BlockSpec((1, d, tf), lambda i, j: (layer, 0, j)),
            pl.BlockSpec((1, tf, d), lambda i, j: (layer, j, 0)),
        ],
        out_specs=pl.BlockSpec((tm, d), lambda i, j: (i, 0)),
        out_shape=jax.ShapeDtypeStruct((m, d), F32),
        compiler_params=_vmem(56 << 20),
        name="mlp",
    )(x2d, h2, mod, w1, w2)


def _rope_tables(seq_len):
    rows = seq_len // GRID_W
    t_row = jnp.repeat(jnp.arange(rows), GRID_W).astype(F32)
    t_col = jnp.tile(jnp.arange(GRID_W), rows).astype(F32)
    axis_dim = HEAD_DIM // 2
    inv_freq = 1.0 / (ROPE_THETA ** (jnp.arange(0, axis_dim, 2, dtype=F32) / axis_dim))
    ang_r = t_row[:, None] * inv_freq[None, :]
    ang_c = t_col[:, None] * inv_freq[None, :]
    cos = jnp.concatenate([jnp.cos(ang_r)] * 2 + [jnp.cos(ang_c)] * 2, axis=-1)
    sin = jnp.concatenate([-jnp.sin(ang_r), jnp.sin(ang_r), -jnp.sin(ang_c), jnp.sin(ang_c)],
                          axis=-1)
    return jnp.tile(cos, (1, LANES // HEAD_DIM)), jnp.tile(sin, (1, LANES // HEAD_DIM))


def _dft_cos_sin(rows, cols, period, scale):
    r = jnp.arange(rows, dtype=jnp.int32)
    c = jnp.arange(cols, dtype=jnp.int32)
    ang = ((r[:, None] * c[None, :]) % period).astype(F32) * (2.0 * math.pi / period)
    return jnp.cos(ang) * scale, jnp.sin(ang) * scale


def kernel(x, c, ctx, c_ctx, ln1_g, ln2_g, w_ada, b_ada, w_in, qn_g, kn_g, lam_q, lam_k,
           subln_g, w_conv, w_br, w_o, w1, w2):
    n_batch, seq, d = x.shape
    ctx_len = ctx.shape[1]
    depth = w_ada.shape[0]
    assert seq == FFT_R * FFT_R and ctx_len == 256 and n_batch <= CTX_MOD_ROW and d == D_MODEL

    c_rows = jnp.zeros((MOD_ROWS, d), F32).at[:n_batch].set(c).at[CTX_MOD_ROW].set(c_ctx)
    mod = _mod_call(c_rows, w_ada, b_ada).reshape(depth, MOD_ROWS, 6, d)

    cos_l, sin_l = _rope_tables(seq)
    cos_c = jnp.ones((n_batch * ctx_len, LANES), F32)
    sin_c = jnp.zeros((n_batch * ctx_len, LANES), F32)
    qg = jnp.tile(qn_g, (1, MXU_W // HEAD_DIM)).reshape(depth, 1, MXU_W)
    kg = jnp.tile(kn_g, (1, MXU_W // HEAD_DIM)).reshape(depth, 1, MXU_W)

    ch_c, ch_s = _dft_cos_sin(GROUP_W, GROUP_W, GROUP_W, GROUP_W ** -0.5)
    chan_cos, chan_sin = ch_c.astype(BF16), ch_s.astype(BF16)
    chan_cs = jnp.concatenate([chan_cos, chan_sin], axis=0)
    pc_c, pc_s = _dft_cos_sin(ctx_len, ctx_len, ctx_len, ctx_len ** -0.5)
    pos_c = jnp.concatenate([pc_c, -pc_s], axis=1).astype(BF16)
    r_c, r_s = _dft_cos_sin(FFT_R, FFT_R, FFT_R, FFT_R ** -0.5)
    eye2 = jnp.eye(2, dtype=F32)
    wa = jnp.kron(jnp.concatenate([r_c, -r_s], axis=0), eye2).astype(BF16)
    wc = jnp.concatenate([jnp.concatenate([r_c, r_s], axis=1),
                          jnp.concatenate([-r_s, r_c], axis=1)], axis=0)
    wc = jnp.einsum("odib,pq->odpqib", wc.reshape(2, FFT_R, 2, FFT_R), eye2)
    wc = wc.reshape(4 * FFT_R, 4 * FFT_R).astype(BF16)
    tw_c, tw_s = _dft_cos_sin(FFT_R, FFT_R, seq, 1.0)
    pair_order = lambda t: jnp.broadcast_to(
        t.reshape(FFT_R // 2, 2, FFT_R).transpose(0, 2, 1).reshape(seq, 1), (seq, LANES))
    tw_cos, tw_sin = pair_order(tw_c), pair_order(tw_s)

    w_br_b = w_br.astype(BF16)
    w_o_b = w_o.astype(BF16)
    w1_b = w1.astype(BF16)
    w2_b = w2.astype(BF16)

    lat_row = lambda row: row // seq
    ctx_row = lambda row: CTX_MOD_ROW
    heads = COL_TILE // LANES

    xl = x.reshape(n_batch * seq, d)
    xc = ctx.reshape(n_batch * ctx_len, d)

    for l in range(depth):
        last = l == depth - 1
        lam_init = 0.8 - 0.6 * math.exp(-0.3 * l)

        hc = _normmod_call(xc, mod, l, ctx_row, ln1_g)
        hl = _normmod_call(xl, mod, l, lat_row, ln1_g)
        qk_c = _proj_call("qk", hc, w_in, l, QK_TILES, (qg, kg, cos_c, sin_c))
        qk_l = _proj_call("qk", hl, w_in, l, QK_TILES, (qg, kg, cos_l, sin_l))
        vt_c = _proj_call("vt", hc, w_in, l, (TILE_V,))
        vt_l = _proj_call("vt", hl, w_in, l, (TILE_V,))
        plain_l = _proj_call("plain", hl, w_in, l, PLAIN_TILES)
        gates_l = _proj_call("gate", hl, w_in, l, GATE_TILES)

        kv_ctx = (qk_c, heads, vt_c, 0, ctx_len)
        kv_lat = (qk_l, heads, vt_l, 0, seq)
        attn_l = _attn_call(lam_q, lam_k, subln_g, l, qk_l, 0, [kv_ctx, kv_lat],
                            n_batch=n_batch, q_len=seq, lam_init=lam_init)
        four_l = _fourier_latent_call(plain_l, wa, wc, tw_cos, tw_sin, chan_cs, n_batch=n_batch)
        xl, h2l = _merge_call(xl, mod, l, lat_row, attn_l, plain_l, gates_l, four_l, w_conv,
                              w_br_b, w_o_b, ln2_g, seq_len=seq)
        xl = _mlp_call(xl, h2l, mod, l, lat_row, w1_b, w2_b)

        if not last:
            plain_c = _proj_call("plain", hc, w_in, l, PLAIN_TILES)
            gates_c = _proj_call("gate", hc, w_in, l, GATE_TILES)
            attn_c = _attn_call(lam_q, lam_k, subln_g, l, qk_c, 0, [kv_ctx],
                                n_batch=n_batch, q_len=ctx_len, lam_init=lam_init)
            four_c = _fourier_dense_call(plain_c, PLAIN_F, chan_cos, chan_sin, pos_c,
                                         n_batch=n_batch, n_pos=ctx_len)
            xc, h2c = _merge_call(xc, mod, l, ctx_row, attn_c, plain_c, gates_c, four_c, w_conv,
                                  w_br_b, w_o_b, ln2_g, seq_len=ctx_len)
            xc = _mlp_call(xc, h2c, mod, l, ctx_row, w1_b, w2_b)

    return xl.reshape(n_batch, seq, d)
```
